```python
import jax, jax.numpy as jnp
from jax import lax
import numpy as np

D_MODEL = 1024
BATCH = 4
SEQ = 4096
DEPTH = 2

CTX_LEN = 256
GRID_W = 64
HEAD_DIM = 64
ROPE_BASE = 10000.0
BLOCK = 128
A_HEADS = 8
A_KV = 2
WINDOW = 128
B_HEADS = 8
B_KV = 2
LRU_WIDTH = D_MODEL
LRU_BLOCKS = 16
LRU_BLOCK_DIM = LRU_WIDTH // LRU_BLOCKS
CONV_W = 4
LRU_C = 8.0
D_FF = 4 * D_MODEL
N_BRANCH = 3
ALPHA = (2 * DEPTH) ** 0.25
BETA = (8 * DEPTH) ** -0.25
LN_EPS = 1e-5
RMS_EPS = 1e-6
NEG_INF = -1e30
IN_COLS = (A_HEADS + 2 * A_KV + B_HEADS + 2 * B_KV) * HEAD_DIM + 2 * LRU_WIDTH + N_BRANCH * D_MODEL

kernel_name = "hybrid_parallel_gated_dit_block"


def _split_cols(z):
    sizes = (A_HEADS * HEAD_DIM, A_KV * HEAD_DIM, A_KV * HEAD_DIM,
             B_HEADS * HEAD_DIM, B_KV * HEAD_DIM, B_KV * HEAD_DIM,
             LRU_WIDTH, LRU_WIDTH, N_BRANCH * D_MODEL)
    return jnp.split(z, np.cumsum(sizes)[:-1].tolist(), axis=-1)


def _layernorm(x, g, b):
    xf = x.astype(jnp.float32)
    mu = jnp.mean(xf, -1, keepdims=True)
    xc = xf - mu
    var = jnp.mean(xc * xc, -1, keepdims=True)
    return (xc * lax.rsqrt(var + LN_EPS) * g + b).astype(x.dtype)


def _rmsnorm(x, g):
    xf = x.astype(jnp.float32)
    return (xf * lax.rsqrt(jnp.mean(xf * xf, -1, keepdims=True) + RMS_EPS) * g).astype(x.dtype)


def _axial_rope_tables(rows):
    row = jnp.repeat(jnp.arange(rows), GRID_W).astype(jnp.float32)
    col = jnp.tile(jnp.arange(GRID_W), rows).astype(jnp.float32)
    nf = HEAD_DIM // 4
    inv = ROPE_BASE ** (-jnp.arange(nf, dtype=jnp.float32) / nf)
    ang_r = row[:, None] * inv
    ang_c = col[:, None] * inv
    return (jnp.cos(ang_r), jnp.sin(ang_r), jnp.cos(ang_c), jnp.sin(ang_c))


def _rotate(x, cos, sin):
    x1, x2 = jnp.split(x, 2, axis=-1)
    return jnp.concatenate([x1 * cos - x2 * sin, x1 * sin + x2 * cos], axis=-1)


def _axial_rope(x, tabs):
    cr, sr, cc, sc = tabs
    half = HEAD_DIM // 2
    xr = _rotate(x[..., :half], cr[:, None, :], sr[:, None, :])
    xcol = _rotate(x[..., half:], cc[:, None, :], sc[:, None, :])
    return jnp.concatenate([xr, xcol], axis=-1).astype(x.dtype)


def _gqa_attend(q, k, v, sink, mask):
    s = jnp.einsum('bqkgd,bjkd->bkgqj', q, k).astype(jnp.float32) * (HEAD_DIM ** -0.5)
    if mask is not None:
        s = jnp.where(mask, s, NEG_INF)
    m = jnp.max(s, -1, keepdims=True)
    if sink is not None:
        sk = sink.astype(jnp.float32)[None, :, :, None, None]
        m = jnp.maximum(m, sk)
    p = jnp.exp(s - m)
    den = jnp.sum(p, -1, keepdims=True)
    if sink is not None:
        den = den + jnp.exp(sk - m)
    o = jnp.einsum('bkgqj,bjkd->bkgqd', p, v.astype(jnp.float32)) / den
    return o.transpose(0, 3, 1, 2, 4).astype(q.dtype)


def _banded_attention(q, k, v, k_ctx, v_ctx, sink):
    bsz, n = q.shape[:2]
    nb = n // BLOCK
    m = k_ctx.shape[1]
    qb = q.reshape(bsz, nb, BLOCK, *q.shape[2:])

    def band(t):
        tp = jnp.pad(t, ((0, 0), (BLOCK, BLOCK), (0, 0), (0, 0))).reshape(bsz, nb + 2, BLOCK, *t.shape[2:])
        return jnp.concatenate([tp[:, :-2], tp[:, 1:-1], tp[:, 2:]], axis=2)

    kb = jnp.concatenate([band(k), jnp.broadcast_to(k_ctx[:, None], (bsz, nb) + k_ctx.shape[1:])], axis=2)
    vb = jnp.concatenate([band(v), jnp.broadcast_to(v_ctx[:, None], (bsz, nb) + v_ctx.shape[1:])], axis=2)
    blk = jnp.arange(nb)[:, None, None]
    qpos = blk * BLOCK + jnp.arange(BLOCK)[None, :, None]
    kpos = blk * BLOCK - BLOCK + jnp.arange(3 * BLOCK)[None, None, :]
    lat_mask = (jnp.abs(kpos - qpos) <= WINDOW) & (kpos >= 0) & (kpos < n)
    mask = jnp.concatenate([lat_mask, jnp.ones((nb, BLOCK, m), dtype=bool)], axis=-1)
    o = jax.vmap(_gqa_attend, in_axes=(1, 1, 1, None, 0), out_axes=1)(qb, kb, vb, sink, mask)
    return o.reshape(bsz, n, -1)


def _blocked_global_attention(q, k_all, v_all):
    bsz, n = q.shape[:2]
    nb = n // BLOCK
    qb = jnp.moveaxis(q.reshape(bsz, nb, BLOCK, *q.shape[2:]), 1, 0)
    o = lax.map(lambda qq: _gqa_attend(qq, k_all, v_all, None, None), qb)
    return jnp.moveaxis(o, 0, 1).reshape(bsz, n, -1)


def _short_conv(x, w, b):
    out = lax.conv_general_dilated(x, w[:, None, :], window_strides=(1,),
                                   padding=[((CONV_W - 1) // 2, CONV_W // 2)],
                                   dimension_numbers=('NWC', 'WIO', 'NWC'),
                                   feature_group_count=x.shape[-1])
    return out + b


def _rglru_gates(x, w_r, b_r, w_i, b_i, lam):
    bsz, n, _ = x.shape
    xb = x.reshape(bsz, n, LRU_BLOCKS, LRU_BLOCK_DIM)
    r = jax.nn.sigmoid(jnp.einsum('bnhd,hde->bnhe', xb, w_r).reshape(bsz, n, LRU_WIDTH) + b_r)
    i = jax.nn.sigmoid(jnp.einsum('bnhd,hde->bnhe', xb, w_i).reshape(bsz, n, LRU_WIDTH) + b_i)
    log_a = -LRU_C * r * jax.nn.softplus(-lam.astype(jnp.float32))
    a = jnp.exp(log_a)
    u = jnp.sqrt(-jnp.expm1(2.0 * log_a)) * (i * x)
    return a, u


def _linear_scan(a, u, h0):
    u = u.at[:, 0].add(a[:, 0] * h0)

    def combine(left, right):
        return left[0] * right[0], right[0] * left[1] + right[1]

    return lax.associative_scan(combine, (a, u), axis=1)[1]


def _rglru_direction(x_ctx, x_lat, w_r, b_r, w_i, b_i, lam, reverse):
    flip = (lambda t: jnp.flip(t, 1)) if reverse else (lambda t: t)
    a_c, u_c = _rglru_gates(x_ctx, w_r, b_r, w_i, b_i, lam)
    h_c = _linear_scan(flip(a_c), flip(u_c), jnp.zeros_like(x_ctx[:, 0]))
    a_l, u_l = _rglru_gates(x_lat, w_r, b_r, w_i, b_i, lam)
    h_l = _linear_scan(flip(a_l), flip(u_l), h_c[:, -1])
    return flip(h_c), flip(h_l)


def _merge(ya, yb, yc, g, w_br_a, w_br_b, w_br_c, w_out):
    ga, gb, gc = jnp.split(jax.nn.sigmoid(g), N_BRANCH, axis=-1)
    return (ga * (ya @ w_br_a) + gb * (yb @ w_br_b) + gc * (yc @ w_br_c)) @ w_out


def _token_mixer(h_lat, h_ctx, rope, w_in, a_sink, b_q_gain, b_k_gain, c_conv_w, c_conv_b,
                 c_wr, c_br, c_wi, c_bi, c_lam, w_br_a, w_br_b, w_br_c, w_out, with_ctx):
    bsz, n, _ = h_lat.shape
    m = h_ctx.shape[1]
    ga, gb = A_HEADS // A_KV, B_HEADS // B_KV
    qa_l, ka_l, va_l, qb_l, kb_l, vb_l, xr_l, yr_l, g_l = _split_cols(h_lat @ w_in)
    qa_c, ka_c, va_c, qb_c, kb_c, vb_c, xr_c, yr_c, g_c = _split_cols(h_ctx @ w_in)

    sink = a_sink.reshape(A_KV, ga)
    qa_l = _axial_rope(qa_l.reshape(bsz, n, A_HEADS, HEAD_DIM), rope).reshape(bsz, n, A_KV, ga, HEAD_DIM)
    ka_l = _axial_rope(ka_l.reshape(bsz, n, A_KV, HEAD_DIM), rope)
    va_l = va_l.reshape(bsz, n, A_KV, HEAD_DIM)
    ka_c = ka_c.reshape(bsz, m, A_KV, HEAD_DIM)
    va_c = va_c.reshape(bsz, m, A_KV, HEAD_DIM)
    ya_l = _banded_attention(qa_l, ka_l, va_l, ka_c, va_c, sink)

    qb_l = _axial_rope(_rmsnorm(qb_l.reshape(bsz, n, B_HEADS, HEAD_DIM), b_q_gain), rope)
    qb_l = qb_l.reshape(bsz, n, B_KV, gb, HEAD_DIM)
    kb_l = _axial_rope(_rmsnorm(kb_l.reshape(bsz, n, B_KV, HEAD_DIM), b_k_gain), rope)
    kb_c = _rmsnorm(kb_c.reshape(bsz, m, B_KV, HEAD_DIM), b_k_gain)
    vb_c = vb_c.reshape(bsz, m, B_KV, HEAD_DIM)
    kb_all = jnp.concatenate([kb_l, kb_c], axis=1)
    vb_all = jnp.concatenate([vb_l.reshape(bsz, n, B_KV, HEAD_DIM), vb_c], axis=1)
    yb_l = _blocked_global_attention(qb_l, kb_all, vb_all)

    xr_l = _short_conv(xr_l, c_conv_w, c_conv_b).astype(jnp.float32)
    xr_c = _short_conv(xr_c, c_conv_w, c_conv_b).astype(jnp.float32)
    hc_f, hl_f = _rglru_direction(xr_c, xr_l, c_wr[0], c_br[0], c_wi[0], c_bi[0], c_lam[0], False)
    hc_b, hl_b = _rglru_direction(xr_c, xr_l, c_wr[1], c_br[1], c_wi[1], c_bi[1], c_lam[1], True)
    yc_l = (hl_f + hl_b).astype(h_lat.dtype) * jax.nn.gelu(yr_l)

    out_lat = _merge(ya_l, yb_l, yc_l, g_l, w_br_a, w_br_b, w_br_c, w_out)
    if not with_ctx:
        return out_lat, None

    ya_c = _gqa_attend(qa_c.reshape(bsz, m, A_KV, ga, HEAD_DIM), ka_c, va_c, sink, None).reshape(bsz, m, -1)
    qb_c = _rmsnorm(qb_c.reshape(bsz, m, B_HEADS, HEAD_DIM), b_q_gain).reshape(bsz, m, B_KV, gb, HEAD_DIM)
    yb_c = _gqa_attend(qb_c, kb_c, vb_c, None, None).reshape(bsz, m, -1)
    yc_c = (hc_f + hc_b).astype(h_ctx.dtype) * jax.nn.gelu(yr_c)
    out_ctx = _merge(ya_c, yb_c, yc_c, g_c, w_br_a, w_br_b, w_br_c, w_out)
    return out_lat, out_ctx


def _sq_relu_mlp(h, w1, w2):
    return jnp.square(jax.nn.relu(h @ w1)) @ w2


def setup_inputs(seed: int = 0) -> dict:
    key = jax.random.key(seed)
    ks = jax.random.split(key, 32)
    f32 = jnp.float32
    L = DEPTH
    bd = LRU_BLOCK_DIM

    def nrm(k, shape, scale):
        return jax.random.normal(k, shape, f32) * scale

    u = jax.random.uniform(ks[17], (L, 2, LRU_WIDTH), f32, 0.9, 0.999)
    p = u ** (1.0 / LRU_C)
    c_lam = jnp.log(p) - jnp.log1p(-p)
    return {
        "x": nrm(ks[0], (BATCH, SEQ, D_MODEL), 1.0),
        "c": nrm(ks[1], (BATCH, D_MODEL), 1.0),
        "ctx": nrm(ks[2], (BATCH, CTX_LEN, D_MODEL), 1.0),
        "c_ctx": nrm(ks[3], (D_MODEL,), 1.0),
        "w_ada": nrm(ks[4], (L, D_MODEL, 6 * D_MODEL), D_MODEL ** -0.5),
        "b_ada": nrm(ks[5], (L, 6 * D_MODEL), 0.02),
        "w_in": nrm(ks[6], (L, D_MODEL, IN_COLS), D_MODEL ** -0.5),
        "a_sink": nrm(ks[7], (L, A_HEADS), 0.5),
        "b_q_gain": 1.0 + nrm(ks[8], (L, HEAD_DIM), 0.02),
        "b_k_gain": 1.0 + nrm(ks[9], (L, HEAD_DIM), 0.02),
        "c_conv_w": nrm(ks[10], (L, CONV_W, LRU_WIDTH), CONV_W ** -0.5),
        "c_conv_b": nrm(ks[11], (L, LRU_WIDTH), 0.02),
        "c_wr": nrm(ks[12], (L, 2, LRU_BLOCKS, bd, bd), bd ** -0.5),
        "c_br": nrm(ks[13], (L, 2, LRU_WIDTH), 0.02),
        "c_wi": nrm(ks[14], (L, 2, LRU_BLOCKS, bd, bd), bd ** -0.5),
        "c_bi": nrm(ks[15], (L, 2, LRU_WIDTH), 0.02),
        "c_lam": c_lam,
        "w_br_a": nrm(ks[18], (L, A_HEADS * HEAD_DIM, D_MODEL), BETA * (A_HEADS * HEAD_DIM) ** -0.5),
        "w_br_b": nrm(ks[19], (L, B_HEADS * HEAD_DIM, D_MODEL), BETA * (B_HEADS * HEAD_DIM) ** -0.5),
        "w_br_c": nrm(ks[20], (L, LRU_WIDTH, D_MODEL), BETA * LRU_WIDTH ** -0.5),
        "w_out": nrm(ks[21], (L, D_MODEL, D_MODEL), BETA * D_MODEL ** -0.5),
        "ln1_g": 1.0 + nrm(ks[22], (L, D_MODEL), 0.02),
        "ln1_b": nrm(ks[23], (L, D_MODEL), 0.02),
        "w_ff1": nrm(ks[24], (L, D_MODEL, D_FF), D_MODEL ** -0.5),
        "w_ff2": nrm(ks[25], (L, D_FF, D_MODEL), BETA * D_FF ** -0.5),
        "ln2_g": 1.0 + nrm(ks[26], (L, D_MODEL), 0.02),
        "ln2_b": nrm(ks[27], (L, D_MODEL), 0.02),
    }


def reference(x, c, ctx, c_ctx, w_ada, b_ada, w_in, a_sink, b_q_gain, b_k_gain, c_conv_w, c_conv_b,
              c_wr, c_br, c_wi, c_bi, c_lam, w_br_a, w_br_b, w_br_c, w_out, ln1_g, ln1_b,
              w_ff1, w_ff2, ln2_g, ln2_b):
    rows = x.shape[1] // GRID_W
    rope = _axial_rope_tables(rows)
    for l in range(DEPTH):
        with_ctx = l < DEPTH - 1
        mod_lat = (jax.nn.silu(c) @ w_ada[l] + b_ada[l])[:, None, :]
        mod_ctx = (jax.nn.silu(c_ctx) @ w_ada[l] + b_ada[l])[None, None, :]
        sh1, sc1, g1, sh2, sc2, g2 = jnp.split(mod_lat, 6, axis=-1)
        csh1, csc1, cg1, csh2, csc2, cg2 = jnp.split(mod_ctx, 6, axis=-1)

        o_lat, o_ctx = _token_mixer(x * (1.0 + sc1) + sh1, ctx * (1.0 + csc1) + csh1, rope,
                                    w_in[l], a_sink[l], b_q_gain[l], b_k_gain[l], c_conv_w[l], c_conv_b[l],
                                    c_wr[l], c_br[l], c_wi[l], c_bi[l], c_lam[l],
                                    w_br_a[l], w_br_b[l], w_br_c[l], w_out[l], with_ctx)
        x = _layernorm(ALPHA * x + g1 * o_lat, ln1_g[l], ln1_b[l])
        x = _layernorm(ALPHA * x + g2 * _sq_relu_mlp(x * (1.0 + sc2) + sh2, w_ff1[l], w_ff2[l]),
                       ln2_g[l], ln2_b[l])
        if with_ctx:
            ctx = _layernorm(ALPHA * ctx + cg1 * o_ctx, ln1_g[l], ln1_b[l])
            ctx = _layernorm(ALPHA * ctx + cg2 * _sq_relu_mlp(ctx * (1.0 + csc2) + csh2, w_ff1[l], w_ff2[l]),
                             ln2_g[l], ln2_b[l])
    return x
```

```python
import functools

import jax
import jax.numpy as jnp
from jax import lax
from jax.experimental import pallas as pl
from jax.experimental.pallas import tpu as pltpu

D_MODEL = 1024
DEPTH = 2
GRID_W = 64
HEAD_DIM = 64
ROPE_BASE = 10000.0
BLOCK = 128
A_HEADS = 8
A_KV = 2
B_HEADS = 8
B_KV = 2
LRU_WIDTH = D_MODEL
LRU_BLOCKS = 16
LRU_BLOCK_DIM = LRU_WIDTH // LRU_BLOCKS
CONV_W = 4
LRU_C = 8.0
D_FF = 4 * D_MODEL
ALPHA = (2 * DEPTH) ** 0.25
LN_EPS = 1e-5
RMS_EPS = 1e-6
NEG_INF = -1e30
Q_COLS = A_HEADS * HEAD_DIM
KV_COLS = A_KV * HEAD_DIM
ATTN_COLS = 2 * (Q_COLS + 2 * KV_COLS)
GROUP = A_HEADS // A_KV

LANES = 128
SUBLANES = 8
V7X_VMEM_BYTES = 64 * 1024 * 1024
VMEM_LIMIT_BYTES = V7X_VMEM_BYTES - 8 * 1024 * 1024

F32 = jnp.float32
BF16 = jnp.bfloat16
NT_DIMS = (((1,), (1,)), ((), ()))


def _params(*semantics):
    return pltpu.CompilerParams(dimension_semantics=semantics, vmem_limit_bytes=VMEM_LIMIT_BYTES)


def _resident(block_shape, index_map):
    return pl.BlockSpec(block_shape, index_map, pipeline_mode=pl.Buffered(1))


def _modulate(x, sc_ref, sh_ref):
    return (x * (1.0 + sc_ref[...]) + sh_ref[...]).astype(BF16)


def _layernorm(y, g, b):
    mu = jnp.mean(y, axis=-1, keepdims=True)
    yc = y - mu
    var = jnp.mean(yc * yc, axis=-1, keepdims=True)
    return yc * lax.rsqrt(var + LN_EPS) * g + b


def _ada_kernel(c_ref, w_ref, b_ref, o_ref):
    c = c_ref[...]
    s = c * jax.nn.sigmoid(c)
    o_ref[...] = jnp.dot(s, w_ref[...], preferred_element_type=F32,
                         precision=lax.Precision.HIGHEST) + b_ref[...]


def _ada(cc, w_ada, b_ada):
    depth, d, cols = w_ada.shape
    tn = cols // 4
    return pl.pallas_call(
        _ada_kernel,
        grid=(depth, cols // tn),
        in_specs=[pl.BlockSpec((SUBLANES, d), lambda l, j: (0, 0)),
                  pl.BlockSpec((None, d, tn), lambda l, j: (l, 0, j)),
                  pl.BlockSpec((None, 1, tn), lambda l, j: (l, 0, j))],
        out_specs=pl.BlockSpec((None, SUBLANES, tn), lambda l, j: (l, 0, j)),
        out_shape=jax.ShapeDtypeStruct((depth, SUBLANES, cols), F32),
        compiler_params=_params("arbitrary", "arbitrary"),
        name="ada_mod",
    )(cc, w_ada, b_ada.reshape(depth, 1, cols))


def _inproj_attn_kernel(*refs, rope):
    if rope:
        (x_ref, sc_ref, sh_ref, w_ref, gq_ref, gk_ref, ones_ref, cos_ref, sin_ref,
         qa_ref, ka_ref, va_ref, qb_ref, kb_ref, vb_ref) = refs
    else:
        (x_ref, sc_ref, sh_ref, w_ref, gq_ref, gk_ref, ones_ref,
         qa_ref, ka_ref, va_ref, qb_ref, kb_ref, vb_ref) = refs
    h = _modulate(x_ref[...], sc_ref, sh_ref)
    z = jnp.dot(h, w_ref[...], preferred_element_type=F32)
    tm = z.shape[0]
    lane = lax.broadcasted_iota(jnp.int32, (tm, LANES), 1)
    low_half = lane < HEAD_DIM
    first16 = (lane & 31) < 16
    ones = ones_ref[...]
    scale = HEAD_DIM ** -0.5

    def rot(zc):
        if not rope:
            return zc
        partner = jnp.where(first16, pltpu.roll(zc, LANES - 16, 1), pltpu.roll(zc, 16, 1))
        return zc * cos_ref[...] + partner * sin_ref[...]

    def rms(zc, g_ref):
        sq = zc * zc
        hi = sq.astype(BF16)
        lo = (sq - hi.astype(F32)).astype(BF16)
        ss = (jnp.dot(hi, ones, preferred_element_type=F32)
              + jnp.dot(lo, ones, preferred_element_type=F32))
        return zc * lax.rsqrt(ss * (1.0 / HEAD_DIM) + RMS_EPS) * g_ref[...]

    def put_dup(ref, zc):
        sw = pltpu.roll(zc, HEAD_DIM, 1)
        ref[:, 0:LANES] = jnp.where(low_half, zc, sw).astype(BF16)
        ref[:, LANES:2 * LANES] = jnp.where(low_half, sw, zc).astype(BF16)

    def col(start):
        return z[:, start:start + LANES]

    for c in range(Q_COLS // LANES):
        qa_ref[:, c * LANES:(c + 1) * LANES] = (rot(col(c * LANES)) * scale).astype(BF16)
    base = Q_COLS
    put_dup(ka_ref, rot(col(base)))
    put_dup(va_ref, col(base + KV_COLS))
    base = Q_COLS + 2 * KV_COLS
    for c in range(Q_COLS // LANES):
        qb_ref[:, c * LANES:(c + 1) * LANES] = (
            rot(rms(col(base + c * LANES), gq_ref)) * scale).astype(BF16)
    base += Q_COLS
    put_dup(kb_ref, rot(rms(col(base), gk_ref)))
    put_dup(vb_ref, col(base + KV_COLS))


def _inproj_attn(x, sc, sh, w, gq, gk, ones, rope_tabs, tm):
    bsz, rows, d = x.shape
    rope = rope_tabs is not None
    mod_spec = pl.BlockSpec((None, 1, d), lambda b, i: (b, 0, 0))
    vec_spec = _resident((1, LANES), lambda b, i: (0, 0))
    in_specs = [pl.BlockSpec((None, tm, d), lambda b, i: (b, i, 0)), mod_spec, mod_spec,
                _resident((d, ATTN_COLS), lambda b, i: (0, 0)), vec_spec, vec_spec,
                _resident((LANES, LANES), lambda b, i: (0, 0))]
    args = [x, sc, sh, w, gq, gk, ones]
    if rope:
        tab_spec = pl.BlockSpec((tm, LANES), lambda b, i: (i, 0))
        in_specs += [tab_spec, tab_spec]
        args += list(rope_tabs)

    def out(cols):
        return (pl.BlockSpec((None, tm, cols), lambda b, i: (b, i, 0)),
                jax.ShapeDtypeStruct((bsz, rows, cols), BF16))

    outs = [out(Q_COLS), out(2 * KV_COLS), out(2 * KV_COLS), out(Q_COLS), out(2 * KV_COLS), out(2 * KV_COLS)]
    return pl.pallas_call(
        functools.partial(_inproj_attn_kernel, rope=rope),
        grid=(bsz, rows // tm),
        in_specs=in_specs,
        out_specs=[o[0] for o in outs],
        out_shape=[o[1] for o in outs],
        compiler_params=_params("arbitrary", "arbitrary"),
        name="inproj_attn_rope" if rope else "inproj_attn_ctx",
    )(*args)


def _inproj_rnn_kernel(x_ref, sc_ref, sh_ref, w_ref, o_ref):
    h = _modulate(x_ref[...], sc_ref, sh_ref)
    o_ref[...] = jnp.dot(h, w_ref[...], preferred_element_type=F32)


def _inproj_rnn(x, sc, sh, w, tm):
    bsz, rows, d = x.shape
    cols = w.shape[1]
    mod_spec = pl.BlockSpec((None, 1, d), lambda b, i: (b, 0, 0))
    return pl.pallas_call(
        _inproj_rnn_kernel,
        grid=(bsz, rows // tm),
        in_specs=[pl.BlockSpec((None, tm, d), lambda b, i: (b, i, 0)), mod_spec, mod_spec,
                  _resident((d, cols), lambda b, i: (0, 0))],
        out_specs=pl.BlockSpec((None, tm, cols), lambda b, i: (b, i, 0)),
        out_shape=jax.ShapeDtypeStruct((bsz, rows, cols), F32),
        compiler_params=_params("arbitrary", "arbitrary"),
        name="inproj_rnn",
    )(x, sc, sh, w)


def _attn_kernel(*refs, segs, has_sink, tq, kc):
    refs = list(refs)
    sink_ref = refs.pop(0) if has_sink else None
    q_ref = refs.pop(0)
    kv_refs = [(refs.pop(0), refs.pop(0)) for _ in segs]
    o_ref, m_scr, l_scr, acc_scr = refs
    i = pl.program_id(1)
    nb = pl.num_programs(1)
    lane = lax.broadcasted_iota(jnp.int32, (tq, LANES), 1)
    low_half = lane < HEAD_DIM
    rows = GROUP * tq
    zero = jnp.zeros((tq, LANES), BF16)

    def seg_mask(kind):
        if kind is None:
            return None
        r = lax.broadcasted_iota(jnp.int32, (rows, BLOCK), 0) & (tq - 1)
        j = lax.broadcasted_iota(jnp.int32, (rows, BLOCK), 1)
        far = jnp.int32(4 * BLOCK)
        if kind == "prev":
            return (j - r) >= jnp.where(i > 0, 0, far)
        return (r - j) >= jnp.where(i < nb - 1, 0, far)

    for kvi in range(A_KV):
        c0 = GROUP // 2 * kvi * LANES
        q0 = q_ref[:, c0:c0 + LANES]
        q1 = q_ref[:, c0 + LANES:c0 + 2 * LANES]
        qs = jnp.concatenate([jnp.where(low_half, q0, zero), jnp.where(low_half, zero, q0),
                              jnp.where(low_half, q1, zero), jnp.where(low_half, zero, q1)], axis=0)
        if has_sink:
            m_scr[...] = jnp.concatenate(
                [jnp.full((tq, 1), sink_ref[GROUP * kvi + hh], F32) for hh in range(GROUP)], axis=0)
            l_scr[...] = jnp.ones((rows, 1), F32)
        else:
            m_scr[...] = jnp.full((rows, 1), NEG_INF, F32)
            l_scr[...] = jnp.zeros((rows, 1), F32)
        acc_scr[...] = jnp.zeros((rows, LANES), F32)
        ksl = slice(kvi * LANES, (kvi + 1) * LANES)

        def chunk(k, v, mask):
            s = lax.dot_general(qs, k, NT_DIMS, preferred_element_type=F32)
            if mask is not None:
                s = jnp.where(mask, s, NEG_INF)
            m_prev = m_scr[...]
            m_new = jnp.maximum(m_prev, jnp.max(s, axis=-1, keepdims=True))
            p = jnp.exp(s - m_new)
            alpha = jnp.exp(m_prev - m_new)
            l_scr[...] = alpha * l_scr[...] + jnp.sum(p, axis=-1, keepdims=True)
            acc_scr[...] = alpha * acc_scr[...] + jnp.dot(p.astype(BF16), v, preferred_element_type=F32)
            m_scr[...] = m_new

        for (nk, kind), (k_ref, v_ref) in zip(segs, kv_refs):
            if nk <= kc:
                chunk(k_ref[:, ksl], v_ref[:, ksl], seg_mask(kind))
            else:
                def body(ci, carry, k_ref=k_ref, v_ref=v_ref):
                    r0 = pl.multiple_of(ci * kc, kc)
                    chunk(k_ref[pl.ds(r0, kc), ksl], v_ref[pl.ds(r0, kc), ksl], None)
                    return carry
                lax.fori_loop(0, nk // kc, body, 0)

        o = acc_scr[...] / l_scr[...]
        o_ref[:, c0:c0 + LANES] = jnp.where(low_half, o[0:tq], o[tq:2 * tq]).astype(BF16)
        o_ref[:, c0 + LANES:c0 + 2 * LANES] = jnp.where(
            low_half, o[2 * tq:3 * tq], o[3 * tq:4 * tq]).astype(BF16)


def _attention(q, seg_arrays, seg_specs, segs, sink, tq, kc, name):
    bsz, rows, _ = q.shape
    has_sink = sink is not None
    in_specs, args = [], []
    if has_sink:
        in_specs.append(pl.BlockSpec(memory_space=pltpu.SMEM))
        args.append(sink)
    in_specs.append(pl.BlockSpec((None, tq, Q_COLS), lambda b, i: (b, i, 0)))
    args.append(q)
    for (k, v), spec in zip(seg_arrays, seg_specs):
        in_specs += [spec, spec]
        args += [k, v]
    return pl.pallas_call(
        functools.partial(_attn_kernel, segs=tuple(segs), has_sink=has_sink, tq=tq, kc=kc),
        grid=(bsz, rows // tq),
        in_specs=in_specs,
        out_specs=pl.BlockSpec((None, tq, Q_COLS), lambda b, i: (b, i, 0)),
        out_shape=jax.ShapeDtypeStruct((bsz, rows, Q_COLS), BF16),
        scratch_shapes=[pltpu.VMEM((GROUP * tq, 1), F32), pltpu.VMEM((GROUP * tq, 1), F32),
                        pltpu.VMEM((GROUP * tq, LANES), F32)],
        compiler_params=_params("arbitrary", "arbitrary"),
        name=name,
    )(*args)


def _whole_seg_spec(nk):
    return pl.BlockSpec((None, nk, 2 * KV_COLS), lambda b, i: (b, 0, 0))


def _attn_window(q, k, v, k_ctx, v_ctx, sink):
    nb = q.shape[1] // BLOCK
    m = k_ctx.shape[1]
    blk = (None, BLOCK, 2 * KV_COLS)
    specs = [pl.BlockSpec(blk, lambda b, i: (b, i, 0)),
             pl.BlockSpec(blk, lambda b, i: (b, jnp.maximum(i - 1, 0), 0)),
             pl.BlockSpec(blk, lambda b, i: (b, jnp.minimum(i + 1, nb - 1), 0)),
             _whole_seg_spec(m)]
    segs = [(BLOCK, None), (BLOCK, "prev"), (BLOCK, "next"), (m, None)]
    return _attention(q, [(k, v)] * 3 + [(k_ctx, v_ctx)], specs, segs, sink, BLOCK, max(BLOCK, m), "attn_window")


def _attn_global(q, kv_list, sink, tq, kc, name):
    specs = [_whole_seg_spec(k.shape[1]) for k, _ in kv_list]
    segs = [(k.shape[1], None) for k, _ in kv_list]
    return _attention(q, kv_list, specs, segs, sink, tq, kc, name)


def _rglru_kernel(xp_ref, xc_ref, xn_ref, cw_ref, cb_ref, wg_ref, br_ref, bi_ref, lam_ref, h0_ref,
                  h_ref, hfin_ref, carry_scr, a_scr, u_scr, *, reverse):
    j = pl.program_id(1)
    nt = pl.num_programs(1)
    t = nt - 1 - j if reverse else j
    rows_t, width = xc_ref.shape

    @pl.when(j == 0)
    def _():
        carry_scr[...] = h0_ref[...]

    prev = jnp.where(t > 0, xp_ref[...], 0.0)
    nxt = jnp.where(t < nt - 1, xn_ref[...], 0.0)
    ext = jnp.concatenate([prev, xc_ref[...], nxt], axis=0)
    xconv = cb_ref[...]
    for tap in range(CONV_W):
        off = SUBLANES - (CONV_W - 1) // 2 + tap
        xconv = xconv + cw_ref[tap:tap + 1, :] * ext[off:off + rows_t, :]

    lam = lam_ref[...]
    neg_softplus = -(jnp.maximum(-lam, 0.0) + jnp.log1p(jnp.exp(-jnp.abs(lam))))
    sub = lax.broadcasted_iota(jnp.int32, (rows_t, LANES), 0) & (SUBLANES - 1)

    for c in range(width // LANES):
        cs = slice(c * LANES, (c + 1) * LANES)
        xcc = xconv[:, cs]
        g = jnp.dot(xcc.astype(BF16), wg_ref[c], preferred_element_type=F32)
        r = jax.nn.sigmoid(g[:, :LANES] + br_ref[:, cs])
        ig = jax.nn.sigmoid(g[:, LANES:] + bi_ref[:, cs])
        log_a = (LRU_C * r) * neg_softplus[:, cs]
        a = jnp.exp(log_a)
        u = jnp.sqrt(-jnp.tanh(log_a) * (a * a + 1.0)) * (ig * xcc)
        for s in (1, 2, 4):
            if reverse:
                ok = sub < SUBLANES - s
                a_sh = jnp.where(ok, pltpu.roll(a, rows_t - s, 0), 1.0)
                u_sh = jnp.where(ok, pltpu.roll(u, rows_t - s, 0), 0.0)
            else:
                ok = sub >= s
                a_sh = jnp.where(ok, pltpu.roll(a, s, 0), 1.0)
                u_sh = jnp.where(ok, pltpu.roll(u, s, 0), 0.0)
            u = u + a * u_sh
            a = a * a_sh
        a_scr[:, cs] = a
        u_scr[:, cs] = u

    ngroups = rows_t // SUBLANES
    h_last = carry_scr[...]
    for gi in range(ngroups):
        r0 = (ngroups - 1 - gi if reverse else gi) * SUBLANES
        hg = u_scr[r0:r0 + SUBLANES, :] + a_scr[r0:r0 + SUBLANES, :] * h_last
        h_ref[r0:r0 + SUBLANES, :] = hg
        h_last = hg[0:1, :] if reverse else hg[SUBLANES - 1:SUBLANES, :]
    carry_scr[...] = h_last

    @pl.when(j == nt - 1)
    def _():
        hfin_ref[...] = h_last


def _rglru(xr, h0, cw, cb, wg, br, bi, lam, reverse, tile):
    bsz, rows, width = xr.shape
    nt = rows // tile
    hb = tile // SUBLANES

    def tpos(j):
        return nt - 1 - j if reverse else j

    vec = _resident((1, width), lambda b, j: (0, 0))
    in_specs = [
        pl.BlockSpec((None, SUBLANES, width), lambda b, j: (b, jnp.maximum(tpos(j) * hb - 1, 0), 0)),
        pl.BlockSpec((None, tile, width), lambda b, j: (b, tpos(j), 0)),
        pl.BlockSpec((None, SUBLANES, width),
                     lambda b, j: (b, jnp.minimum((tpos(j) + 1) * hb, rows // SUBLANES - 1), 0)),
        _resident((CONV_W, width), lambda b, j: (0, 0)), vec,
        _resident(wg.shape, lambda b, j: (0, 0, 0)), vec, vec, vec,
        pl.BlockSpec((None, 1, width), lambda b, j: (b, 0, 0)),
    ]
    return pl.pallas_call(
        functools.partial(_rglru_kernel, reverse=reverse),
        grid=(bsz, nt),
        in_specs=in_specs,
        out_specs=[pl.BlockSpec((None, tile, width), lambda b, j: (b, tpos(j), 0)),
                   pl.BlockSpec((None, 1, width), lambda b, j: (b, 0, 0))],
        out_shape=[jax.ShapeDtypeStruct((bsz, rows, width), F32),
                   jax.ShapeDtypeStruct((bsz, 1, width), F32)],
        scratch_shapes=[pltpu.VMEM((1, width), F32), pltpu.VMEM((tile, width), F32),
                        pltpu.VMEM((tile, width), F32)],
        compiler_params=_params("arbitrary", "arbitrary"),
        name="rglru_bwd" if reverse else "rglru_fwd",
    )(xr, xr, xr, cw, cb, wg, br, bi, lam, h0)


def _merge_kernel(x_ref, sc_ref, sh_ref, gate_ref, ya_ref, yb_ref, hf_ref, hb_ref,
                  wyg_ref, wa_ref, wb_ref, wc_ref, wo_ref, lng_ref, lnb_ref, o_ref):
    x = x_ref[...]
    d = x.shape[1]
    h = _modulate(x, sc_ref, sh_ref)
    zz = jnp.dot(h, wyg_ref[...], preferred_element_type=F32)
    yc = ((hf_ref[...] + hb_ref[...]) * jax.nn.gelu(zz[:, 0:d])).astype(BF16)
    mix = (jax.nn.sigmoid(zz[:, d:2 * d]) * jnp.dot(ya_ref[...], wa_ref[...], preferred_element_type=F32)
           + jax.nn.sigmoid(zz[:, 2 * d:3 * d]) * jnp.dot(yb_ref[...], wb_ref[...], preferred_element_type=F32)
           + jax.nn.sigmoid(zz[:, 3 * d:4 * d]) * jnp.dot(yc, wc_ref[...], preferred_element_type=F32))
    o = jnp.dot(mix.astype(BF16), wo_ref[...], preferred_element_type=F32)
    o_ref[...] = _layernorm(ALPHA * x + gate_ref[...] * o, lng_ref[...], lnb_ref[...])


def _merge(x, sc, sh, gate, ya, yb, hf, hb, wyg, wa, wb, wc, wo, lng, lnb, tm):
    bsz, rows, d = x.shape
    mod_spec = pl.BlockSpec((None, 1, d), lambda b, i: (b, 0, 0))
    vec = _resident((1, d), lambda b, i: (0, 0))

    def tile(cols):
        return pl.BlockSpec((None, tm, cols), lambda b, i: (b, i, 0))

    def weight(w):
        return _resident(w.shape, lambda b, i: (0, 0))

    return pl.pallas_call(
        _merge_kernel,
        grid=(bsz, rows // tm),
        in_specs=[tile(d), mod_spec, mod_spec, mod_spec, tile(Q_COLS), tile(Q_COLS), tile(d), tile(d),
                  weight(wyg), weight(wa), weight(wb), weight(wc), weight(wo), vec, vec],
        out_specs=tile(d),
        out_shape=jax.ShapeDtypeStruct((bsz, rows, d), F32),
        compiler_params=_params("arbitrary", "arbitrary"),
        name="merge_ln",
    )(x, sc, sh, gate, ya, yb, hf, hb, wyg, wa, wb, wc, wo, lng, lnb)


def _mlp_kernel(x_ref, sc_ref, sh_ref, gate_ref, w1_ref, w2_ref, lng_ref, lnb_ref, o_ref):
    x = x_ref[...]
    h = _modulate(x, sc_ref, sh_ref)
    u = jnp.dot(h, w1_ref[...], preferred_element_type=F32)
    u = jnp.square(jnp.maximum(u, 0.0)).astype(BF16)
    o = jnp.dot(u, w2_ref[...], preferred_element_type=F32)
    o_ref[...] = _layernorm(ALPHA * x + gate_ref[...] * o, lng_ref[...], lnb_ref[...])


def _mlp(x, sc, sh, gate, w1, w2, lng, lnb, tm):
    bsz, rows, d = x.shape
    mod_spec = pl.BlockSpec((None, 1, d), lambda b, i: (b, 0, 0))
    vec = _resident((1, d), lambda b, i: (0, 0))
    tile = pl.BlockSpec((None, tm, d), lambda b, i: (b, i, 0))
    return pl.pallas_call(
        _mlp_kernel,
        grid=(bsz, rows // tm),
        in_specs=[tile, mod_spec, mod_spec, mod_spec,
                  _resident(w1.shape, lambda b, i: (0, 0)), _resident(w2.shape, lambda b, i: (0, 0)), vec, vec],
        out_specs=tile,
        out_shape=jax.ShapeDtypeStruct((bsz, rows, d), F32),
        compiler_params=_params("arbitrary", "arbitrary"),
        name="mlp_ln",
    )(x, sc, sh, gate, w1, w2, lng, lnb)


def _rope_tables(n):
    pos = jnp.arange(n)
    row = (pos // GRID_W).astype(F32)
    colp = (pos % GRID_W).astype(F32)
    nf = HEAD_DIM // 4
    inv = ROPE_BASE ** (-jnp.arange(nf, dtype=F32) / nf)
    ang_r = row[:, None] * inv
    ang_c = colp[:, None] * inv
    cos_t = jnp.concatenate([jnp.cos(ang_r), jnp.cos(ang_r), jnp.cos(ang_c), jnp.cos(ang_c)], axis=-1)
    sin_t = jnp.concatenate([-jnp.sin(ang_r), jnp.sin(ang_r), -jnp.sin(ang_c), jnp.sin(ang_c)], axis=-1)
    return jnp.tile(cos_t, (1, 2)), jnp.tile(sin_t, (1, 2))


def _gate_weights(w_r, w_i):
    def pair(w):
        w = w.reshape(LRU_BLOCKS // 2, 2, LRU_BLOCK_DIM, LRU_BLOCK_DIM)
        z = jnp.zeros_like(w[:, 0])
        top = jnp.concatenate([w[:, 0], z], axis=-1)
        bot = jnp.concatenate([z, w[:, 1]], axis=-1)
        return jnp.concatenate([top, bot], axis=-2)
    return jnp.concatenate([pair(w_r), pair(w_i)], axis=-1).astype(BF16)


def kernel(x, c, ctx, c_ctx, w_ada, b_ada, w_in, a_sink, b_q_gain, b_k_gain, c_conv_w, c_conv_b,
           c_wr, c_br, c_wi, c_bi, c_lam, w_br_a, w_br_b, w_br_c, w_out, ln1_g, ln1_b,
           w_ff1, w_ff2, ln2_g, ln2_b):
    bsz, n, d = x.shape
    m = ctx.shape[1]
    depth = w_ada.shape[0]

    cc = jnp.zeros((SUBLANES, d), F32).at[:bsz].set(c).at[bsz].set(c_ctx)
    mod = _ada(cc, w_ada, b_ada)
    rope_tabs = _rope_tables(n)
    blk = jnp.arange(LANES) // HEAD_DIM
    ones = (blk[:, None] == blk[None, :]).astype(BF16)
    h_zero = jnp.zeros((bsz, 1, d), F32)

    for l in range(depth):
        with_ctx = l < depth - 1
        mod_lat = mod[l, :bsz].reshape(bsz, 1, 6 * d)
        mod_ctx = jnp.broadcast_to(mod[l, bsz].reshape(1, 1, 6 * d), (bsz, 1, 6 * d))
        sh1, sc1, g1, sh2, sc2, g2 = [mod_lat[:, :, k * d:(k + 1) * d] for k in range(6)]
        csh1, csc1, cg1, csh2, csc2, cg2 = [mod_ctx[:, :, k * d:(k + 1) * d] for k in range(6)]

        w_attn = w_in[l, :, :ATTN_COLS].astype(BF16)
        w_rnn = w_in[l, :, ATTN_COLS:ATTN_COLS + LRU_WIDTH].astype(BF16)
        w_yg = w_in[l, :, ATTN_COLS + LRU_WIDTH:].astype(BF16)
        gq = jnp.tile(b_q_gain[l], 2).reshape(1, LANES)
        gk = jnp.tile(b_k_gain[l], 2).reshape(1, LANES)

        qa, ka, va, qb, kb, vb = _inproj_attn(x, sc1, sh1, w_attn, gq, gk, ones, rope_tabs, 512)
        qa_c, ka_c, va_c, qb_c, kb_c, vb_c = _inproj_attn(ctx, csc1, csh1, w_attn, gq, gk, ones, None, m)

        ya = _attn_window(qa, ka, va, ka_c, va_c, a_sink[l])
        yb = _attn_global(qb, [(kb, vb), (kb_c, vb_c)], None, BLOCK, 512, "attn_global")

        xr = _inproj_rnn(x, sc1, sh1, w_rnn, 512)
        xr_c = _inproj_rnn(ctx, csc1, csh1, w_rnn, m)
        cb = c_conv_b[l].reshape(1, d)
        hs, hs_c = [], []
        for di, reverse in enumerate((False, True)):
            wg = _gate_weights(c_wr[l, di], c_wi[l, di])
            vecs = [v[l, di].reshape(1, d) for v in (c_br, c_bi, c_lam)]
            h_c, h_fin = _rglru(xr_c, h_zero, c_conv_w[l], cb, wg, *vecs, reverse, m)
            h_l, _ = _rglru(xr, h_fin, c_conv_w[l], cb, wg, *vecs, reverse, 256)
            hs.append(h_l)
            hs_c.append(h_c)

        merge_w = [w_yg] + [w[l].astype(BF16) for w in (w_br_a, w_br_b, w_br_c, w_out)]
        ln1 = [ln1_g[l].reshape(1, d), ln1_b[l].reshape(1, d)]
        ln2 = [ln2_g[l].reshape(1, d), ln2_b[l].reshape(1, d)]
        w1 = w_ff1[l].astype(BF16)
        w2 = w_ff2[l].astype(BF16)

        x = _merge(x, sc1, sh1, g1, ya, yb, hs[0], hs[1], *merge_w, *ln1, 256)
        x = _mlp(x, sc2, sh2, g2, w1, w2, *ln2, 256)
        if with_ctx:
            ya_c = _attn_global(qa_c, [(ka_c, va_c)], a_sink[l], BLOCK, m, "attn_ctx_a")
            yb_c = _attn_global(qb_c, [(kb_c, vb_c)], None, BLOCK, m, "attn_ctx_b")
            ctx = _merge(ctx, csc1, csh1, cg1, ya_c, yb_c, hs_c[0], hs_c[1], *merge_w, *ln1, m)
            ctx = _mlp(ctx, csc2, csh2, cg2, w1, w2, *ln2, m)
    return x
```

```python
import functools

import jax
import jax.numpy as jnp
from jax import lax
from jax.experimental import pallas as pl
from jax.experimental.pallas import tpu as pltpu

D_MODEL = 1024
DEPTH = 2
GRID_W = 64
HEAD_DIM = 64
ROPE_BASE = 10000.0
BLOCK = 128
A_HEADS = 8
A_KV = 2
B_HEADS = 8
B_KV = 2
LRU_WIDTH = D_MODEL
LRU_BLOCKS = 16
LRU_BLOCK_DIM = LRU_WIDTH // LRU_BLOCKS
CONV_W = 4
LRU_C = 8.0
D_FF = 4 * D_MODEL
ALPHA = (2 * DEPTH) ** 0.25
LN_EPS = 1e-5
RMS_EPS = 1e-6
NEG_INF = -1e30
LOG2_E = 1.4426950408889634
Q_COLS = A_HEADS * HEAD_DIM
KV_COLS = A_KV * HEAD_DIM
ATTN_COLS = 2 * (Q_COLS + 2 * KV_COLS)
GROUP = A_HEADS // A_KV

LANES = 128
SUBLANES = 8
V7X_VMEM_BYTES = 64 * 1024 * 1024
VMEM_LIMIT_BYTES = V7X_VMEM_BYTES - 8 * 1024 * 1024

F32 = jnp.float32
BF16 = jnp.bfloat16


def _params(*semantics):
    return pltpu.CompilerParams(dimension_semantics=semantics, vmem_limit_bytes=VMEM_LIMIT_BYTES)


def _resident(block_shape, index_map):
    return pl.BlockSpec(block_shape, index_map, pipeline_mode=pl.Buffered(1))


def _modulate(x, sc_ref, sh_ref):
    return (x * (1.0 + sc_ref[...]) + sh_ref[...]).astype(BF16)


def _layernorm(y, g, b):
    mu = jnp.mean(y, axis=-1, keepdims=True)
    yc = y - mu
    var = jnp.mean(yc * yc, axis=-1, keepdims=True)
    return yc * lax.rsqrt(var + LN_EPS) * g + b


def _ada_kernel(c_ref, w_ref, b_ref, o_ref):
    c = c_ref[...]
    s = c * jax.nn.sigmoid(c)
    o_ref[...] = jnp.dot(s, w_ref[...], preferred_element_type=F32,
                         precision=lax.Precision.HIGHEST) + b_ref[...]


def _ada(cc, w_ada, b_ada):
    depth, d, cols = w_ada.shape
    tn = cols // 4
    return pl.pallas_call(
        _ada_kernel,
        grid=(depth, cols // tn),
        in_specs=[pl.BlockSpec((SUBLANES, d), lambda l, j: (0, 0)),
                  pl.BlockSpec((None, d, tn), lambda l, j: (l, 0, j)),
                  pl.BlockSpec((None, 1, tn), lambda l, j: (l, 0, j))],
        out_specs=pl.BlockSpec((None, SUBLANES, tn), lambda l, j: (l, 0, j)),
        out_shape=jax.ShapeDtypeStruct((depth, SUBLANES, cols), F32),
        compiler_params=_params("arbitrary", "arbitrary"),
        name="ada_mod",
    )(cc, w_ada, b_ada.reshape(depth, 1, cols))


def _inproj_attn_kernel(*refs, rope):
    if rope:
        (x_ref, sc_ref, sh_ref, w_ref, gq_ref, gk_ref, ones_ref, cos_ref, sin_ref,
         qa_ref, ka_ref, va_ref, qb_ref, kb_ref, vb_ref) = refs
    else:
        (x_ref, sc_ref, sh_ref, w_ref, gq_ref, gk_ref, ones_ref,
         qa_ref, ka_ref, va_ref, qb_ref, kb_ref, vb_ref) = refs
    h = _modulate(x_ref[...], sc_ref, sh_ref)
    z = jnp.dot(h, w_ref[...], preferred_element_type=F32)
    tm = z.shape[0]
    lane = lax.broadcasted_iota(jnp.int32, (tm, LANES), 1)
    first16 = (lane & 31) < 16
    ones = ones_ref[...]
    scale = HEAD_DIM ** -0.5 * LOG2_E

    def rot(zc):
        if not rope:
            return zc
        partner = jnp.where(first16, pltpu.roll(zc, LANES - 16, 1), pltpu.roll(zc, 16, 1))
        return zc * cos_ref[...] + partner * sin_ref[...]

    def rms(zc, g_ref):
        sq = zc * zc
        hi = sq.astype(BF16)
        lo = (sq - hi.astype(F32)).astype(BF16)
        ss = (jnp.dot(hi, ones, preferred_element_type=F32)
              + jnp.dot(lo, ones, preferred_element_type=F32))
        return zc * lax.rsqrt(ss * (1.0 / HEAD_DIM) + RMS_EPS) * g_ref[...]

    def col(start):
        return z[:, start:start + LANES]

    for c in range(Q_COLS // LANES):
        qa_ref[:, c * LANES:(c + 1) * LANES] = (rot(col(c * LANES)) * scale).astype(BF16)
    base = Q_COLS
    ka_ref[...] = rot(col(base)).astype(BF16)
    va_ref[...] = col(base + KV_COLS).T.astype(BF16)
    base = Q_COLS + 2 * KV_COLS
    for c in range(Q_COLS // LANES):
        qb_ref[:, c * LANES:(c + 1) * LANES] = (
            rot(rms(col(base + c * LANES), gq_ref)) * scale).astype(BF16)
    base += Q_COLS
    kb_ref[...] = rot(rms(col(base), gk_ref)).astype(BF16)
    vb_ref[...] = col(base + KV_COLS).T.astype(BF16)


def _inproj_attn(x, sc, sh, w, gq, gk, ones, rope_tabs, tm):
    bsz, rows, d = x.shape
    rope = rope_tabs is not None
    mod_spec = pl.BlockSpec((None, 1, d), lambda b, i: (b, 0, 0))
    vec_spec = _resident((1, LANES), lambda b, i: (0, 0))
    in_specs = [pl.BlockSpec((None, tm, d), lambda b, i: (b, i, 0)), mod_spec, mod_spec,
                _resident((d, ATTN_COLS), lambda b, i: (0, 0)), vec_spec, vec_spec,
                _resident((LANES, LANES), lambda b, i: (0, 0))]
    args = [x, sc, sh, w, gq, gk, ones]
    if rope:
        tab_spec = pl.BlockSpec((tm, LANES), lambda b, i: (i, 0))
        in_specs += [tab_spec, tab_spec]
        args += list(rope_tabs)

    def out(cols):
        return (pl.BlockSpec((None, tm, cols), lambda b, i: (b, i, 0)),
                jax.ShapeDtypeStruct((bsz, rows, cols), BF16))

    out_t = (pl.BlockSpec((None, None, KV_COLS, tm), lambda b, i: (b, i, 0, 0)),
             jax.ShapeDtypeStruct((bsz, rows // tm, KV_COLS, tm), BF16))
    outs = [out(Q_COLS), out(KV_COLS), out_t, out(Q_COLS), out(KV_COLS), out_t]
    return pl.pallas_call(
        functools.partial(_inproj_attn_kernel, rope=rope),
        grid=(bsz, rows // tm),
        in_specs=in_specs,
        out_specs=[o[0] for o in outs],
        out_shape=[o[1] for o in outs],
        compiler_params=_params("arbitrary", "arbitrary"),
        name="inproj_attn_rope" if rope else "inproj_attn_ctx",
    )(*args)


def _inproj_rnn_kernel(x_ref, sc_ref, sh_ref, w_ref, o_ref):
    h = _modulate(x_ref[...], sc_ref, sh_ref)
    o_ref[...] = jnp.dot(h, w_ref[...], preferred_element_type=F32)


def _inproj_rnn(x, sc, sh, w, tm):
    bsz, rows, d = x.shape
    cols = w.shape[1]
    mod_spec = pl.BlockSpec((None, 1, d), lambda b, i: (b, 0, 0))
    return pl.pallas_call(
        _inproj_rnn_kernel,
        grid=(bsz, rows // tm),
        in_specs=[pl.BlockSpec((None, tm, d), lambda b, i: (b, i, 0)), mod_spec, mod_spec,
                  _resident((d, cols), lambda b, i: (0, 0))],
        out_specs=pl.BlockSpec((None, tm, cols), lambda b, i: (b, i, 0)),
        out_shape=jax.ShapeDtypeStruct((bsz, rows, cols), F32),
        compiler_params=_params("arbitrary", "arbitrary"),
        name="inproj_rnn",
    )(x, sc, sh, w)


def _attn_kernel(*refs, segs, has_sink, tq):
    refs = list(refs)
    sink_ref = refs.pop(0) if has_sink else None
    q_ref = refs.pop(0)
    kv_refs = [(refs.pop(0), refs.pop(0)) for _ in segs]
    o_ref = refs.pop(0)
    s_scr, m_scr, l_scr, acc_scr = [refs[g * A_KV:(g + 1) * A_KV] for g in range(4)]
    i = pl.program_id(1)
    nb = pl.num_programs(1)
    width = GROUP * tq

    def seg_mask(kind):
        if kind is None:
            return None
        j = lax.broadcasted_iota(jnp.int32, (BLOCK, width), 0)
        r = lax.broadcasted_iota(jnp.int32, (BLOCK, width), 1) & (tq - 1)
        far = jnp.int32(4 * BLOCK)
        if kind == "prev":
            return (j - r) >= jnp.where(i > 0, 0, far)
        return (r - j) >= jnp.where(i < nb - 1, 0, far)

    def sink_row(kvi):
        return jnp.concatenate(
            [jnp.full((1, tq), sink_ref[GROUP * kvi + hh] * LOG2_E, F32) for hh in range(GROUP)], axis=1)

    qt = q_ref[...].T
    zeros = jnp.zeros((HEAD_DIM, width), BF16)
    qz = []
    for kvi in range(A_KV):
        heads = jnp.concatenate(
            [qt[(GROUP * kvi + hh) * HEAD_DIM:(GROUP * kvi + hh + 1) * HEAD_DIM, :] for hh in range(GROUP)],
            axis=1)
        parts = [zeros] * A_KV
        parts[kvi] = heads
        qz.append(jnp.concatenate(parts, axis=0))

    def for_each_chunk(fn):
        base = 0
        for (nchunks, kc, kind), (k_ref, vt_ref) in zip(segs, kv_refs):
            if nchunks == 1:
                fn(k_ref, vt_ref, 0, 0, base, kc, kind)
            else:
                def body(ci, carry, k_ref=k_ref, vt_ref=vt_ref, kc=kc, base=base):
                    r0 = pl.multiple_of(ci * kc, kc)
                    fn(k_ref, vt_ref, r0, ci, base + r0, kc, None)
                    return carry
                lax.fori_loop(0, nchunks, body, 0, unroll=4)
            base += nchunks * kc

    def start_scores(kvi):
        m_scr[kvi][...] = sink_row(kvi) if has_sink else jnp.full((1, width), NEG_INF, F32)

    def scores(kvi, k_ref, vt_ref, r0, ci, srow, kc, kind):
        s = jnp.dot(k_ref[pl.ds(r0, kc), :], qz[kvi], preferred_element_type=F32)
        mask = seg_mask(kind)
        if mask is not None:
            s = jnp.where(mask, s, NEG_INF)
        s_scr[kvi][pl.ds(srow, kc), :] = s
        m_scr[kvi][...] = jnp.maximum(m_scr[kvi][...], jnp.max(s, axis=0, keepdims=True))

    def start_values(kvi):
        if has_sink:
            l_scr[kvi][...] = jnp.exp2(sink_row(kvi) - m_scr[kvi][...])
        else:
            l_scr[kvi][...] = jnp.zeros((1, width), F32)
        acc_scr[kvi][...] = jnp.zeros((HEAD_DIM, width), F32)

    def values(kvi, k_ref, vt_ref, r0, ci, srow, kc, kind):
        p = jnp.exp2(s_scr[kvi][pl.ds(srow, kc), :] - m_scr[kvi][...])
        l_scr[kvi][...] += jnp.sum(p, axis=0, keepdims=True)
        acc_scr[kvi][...] += jnp.dot(vt_ref[ci, kvi * HEAD_DIM:(kvi + 1) * HEAD_DIM, :], p.astype(BF16),
                                     preferred_element_type=F32)

    def scores_and_values(score_kvi, value_kvi):
        def fn(*chunk):
            scores(score_kvi, *chunk)
            values(value_kvi, *chunk)
        return fn

    start_scores(0)
    for_each_chunk(functools.partial(scores, 0))
    start_scores(1)
    start_values(0)
    for_each_chunk(scores_and_values(1, 0))
    start_values(1)
    for_each_chunk(functools.partial(values, 1))

    outs = []
    for kvi in range(A_KV):
        o = acc_scr[kvi][...] / l_scr[kvi][...]
        outs += [o[:, hh * tq:(hh + 1) * tq] for hh in range(GROUP)]
    o_ref[...] = jnp.concatenate(outs, axis=0).T.astype(BF16)


def _attention(q, seg_arrays, seg_specs, segs, sink, tq, name):
    bsz, rows, _ = q.shape
    has_sink = sink is not None
    in_specs, args = [], []
    if has_sink:
        in_specs.append(pl.BlockSpec(memory_space=pltpu.SMEM))
        args.append(sink)
    in_specs.append(pl.BlockSpec((None, tq, Q_COLS), lambda b, i: (b, i, 0)))
    args.append(q)
    for (k, vt), (k_spec, vt_spec) in zip(seg_arrays, seg_specs):
        in_specs += [k_spec, vt_spec]
        args += [k, vt]
    width = GROUP * tq
    nk_total = sum(nchunks * kc for nchunks, kc, _ in segs)
    per_head = [(nk_total, width), (1, width), (1, width), (HEAD_DIM, width)]
    return pl.pallas_call(
        functools.partial(_attn_kernel, segs=tuple(segs), has_sink=has_sink, tq=tq),
        grid=(bsz, rows // tq),
        in_specs=in_specs,
        out_specs=pl.BlockSpec((None, tq, Q_COLS), lambda b, i: (b, i, 0)),
        out_shape=jax.ShapeDtypeStruct((bsz, rows, Q_COLS), BF16),
        scratch_shapes=[pltpu.VMEM(shape, F32) for shape in per_head for _ in range(A_KV)],
        compiler_params=_params("arbitrary", "arbitrary"),
        name=name,
    )(*args)


def _whole_seg(k, vt):
    nk = k.shape[1]
    nchunks, kc = vt.shape[1], vt.shape[3]
    specs = (pl.BlockSpec((None, nk, KV_COLS), lambda b, i: (b, 0, 0)),
             pl.BlockSpec((None, nchunks, KV_COLS, kc), lambda b, i: (b, 0, 0, 0)))
    return specs, (nchunks, kc, None)


def _attn_window(q, k, vt, k_ctx, vt_ctx, sink):
    nb = q.shape[1] // BLOCK
    per_slab = vt.shape[3] // BLOCK

    def block_specs(pos):
        return (pl.BlockSpec((None, BLOCK, KV_COLS), lambda b, i: (b, pos(i), 0)),
                pl.BlockSpec((None, 1, KV_COLS, BLOCK),
                             lambda b, i: (b, pos(i) // per_slab, 0, pos(i) % per_slab)))

    ctx_specs, ctx_seg = _whole_seg(k_ctx, vt_ctx)
    specs = [block_specs(lambda i: i),
             block_specs(lambda i: jnp.maximum(i - 1, 0)),
             block_specs(lambda i: jnp.minimum(i + 1, nb - 1)),
             ctx_specs]
    segs = [(1, BLOCK, None), (1, BLOCK, "prev"), (1, BLOCK, "next"), ctx_seg]
    return _attention(q, [(k, vt)] * 3 + [(k_ctx, vt_ctx)], specs, segs, sink, BLOCK, "attn_window")


def _attn_global(q, kv_list, sink, tq, name):
    specs, segs = zip(*[_whole_seg(k, vt) for k, vt in kv_list])
    return _attention(q, kv_list, specs, segs, sink, tq, name)


def _rglru_kernel(xp_ref, xc_ref, xn_ref, cw_ref, cb_ref, wg_ref, br_ref, bi_ref, lam_ref, h0_ref,
                  h_ref, hfin_ref, carry_scr, a_scr, u_scr, *, reverse):
    j = pl.program_id(1)
    nt = pl.num_programs(1)
    t = nt - 1 - j if reverse else j
    rows_t, width = xc_ref.shape

    @pl.when(j == 0)
    def _():
        carry_scr[...] = h0_ref[...]

    prev = jnp.where(t > 0, xp_ref[...], 0.0)
    nxt = jnp.where(t < nt - 1, xn_ref[...], 0.0)
    ext = jnp.concatenate([prev, xc_ref[...], nxt], axis=0)
    xconv = cb_ref[...]
    for tap in range(CONV_W):
        off = SUBLANES - (CONV_W - 1) // 2 + tap
        xconv = xconv + cw_ref[tap:tap + 1, :] * ext[off:off + rows_t, :]

    lam = lam_ref[...]
    neg_softplus = -(jnp.maximum(-lam, 0.0) + jnp.log1p(jnp.exp(-jnp.abs(lam))))
    sub = lax.broadcasted_iota(jnp.int32, (rows_t, LANES), 0) & (SUBLANES - 1)

    for c in range(width // LANES):
        cs = slice(c * LANES, (c + 1) * LANES)
        xcc = xconv[:, cs]
        g = jnp.dot(xcc.astype(BF16), wg_ref[c], preferred_element_type=F32)
        r = jax.nn.sigmoid(g[:, :LANES] + br_ref[:, cs])
        ig = jax.nn.sigmoid(g[:, LANES:] + bi_ref[:, cs])
        log_a = (LRU_C * r) * neg_softplus[:, cs]
        a = jnp.exp(log_a)
        u = jnp.sqrt(-jnp.tanh(log_a) * (a * a + 1.0)) * (ig * xcc)
        for s in (1, 2, 4):
            if reverse:
                ok = sub < SUBLANES - s
                a_sh = jnp.where(ok, pltpu.roll(a, rows_t - s, 0), 1.0)
                u_sh = jnp.where(ok, pltpu.roll(u, rows_t - s, 0), 0.0)
            else:
                ok = sub >= s
                a_sh = jnp.where(ok, pltpu.roll(a, s, 0), 1.0)
                u_sh = jnp.where(ok, pltpu.roll(u, s, 0), 0.0)
            u = u + a * u_sh
            a = a * a_sh
        a_scr[:, cs] = a
        u_scr[:, cs] = u

    ngroups = rows_t // SUBLANES
    h_last = carry_scr[...]
    for gi in range(ngroups):
        r0 = (ngroups - 1 - gi if reverse else gi) * SUBLANES
        hg = u_scr[r0:r0 + SUBLANES, :] + a_scr[r0:r0 + SUBLANES, :] * h_last
        h_ref[r0:r0 + SUBLANES, :] = hg
        h_last = hg[0:1, :] if reverse else hg[SUBLANES - 1:SUBLANES, :]
    carry_scr[...] = h_last

    @pl.when(j == nt - 1)
    def _():
        hfin_ref[...] = h_last


def _rglru(xr, h0, cw, cb, wg, br, bi, lam, reverse, tile):
    bsz, rows, width = xr.shape
    nt = rows // tile
    hb = tile // SUBLANES

    def tpos(j):
        return nt - 1 - j if reverse else j

    vec = _resident((1, width), lambda b, j: (0, 0))
    in_specs = [
        pl.BlockSpec((None, SUBLANES, width), lambda b, j: (b, jnp.maximum(tpos(j) * hb - 1, 0), 0)),
        pl.BlockSpec((None, tile, width), lambda b, j: (b, tpos(j), 0)),
        pl.BlockSpec((None, SUBLANES, width),
                     lambda b, j: (b, jnp.minimum((tpos(j) + 1) * hb, rows // SUBLANES - 1), 0)),
        _resident((CONV_W, width), lambda b, j: (0, 0)), vec,
        _resident(wg.shape, lambda b, j: (0, 0, 0)), vec, vec, vec,
        pl.BlockSpec((None, 1, width), lambda b, j: (b, 0, 0)),
    ]
    return pl.pallas_call(
        functools.partial(_rglru_kernel, reverse=reverse),
        grid=(bsz, nt),
        in_specs=in_specs,
        out_specs=[pl.BlockSpec((None, tile, width), lambda b, j: (b, tpos(j), 0)),
                   pl.BlockSpec((None, 1, width), lambda b, j: (b, 0, 0))],
        out_shape=[jax.ShapeDtypeStruct((bsz, rows, width), F32),
                   jax.ShapeDtypeStruct((bsz, 1, width), F32)],
        scratch_shapes=[pltpu.VMEM((1, width), F32), pltpu.VMEM((tile, width), F32),
                        pltpu.VMEM((tile, width), F32)],
        compiler_params=_params("arbitrary", "arbitrary"),
        name="rglru_bwd" if reverse else "rglru_fwd",
    )(xr, xr, xr, cw, cb, wg, br, bi, lam, h0)


def _merge_kernel(x_ref, sc_ref, sh_ref, gate_ref, ya_ref, yb_ref, hf_ref, hb_ref,
                  wyg_ref, wa_ref, wb_ref, wc_ref, wo_ref, lng_ref, lnb_ref, o_ref):
    x = x_ref[...]
    d = x.shape[1]
    h = _modulate(x, sc_ref, sh_ref)
    zz = jnp.dot(h, wyg_ref[...], preferred_element_type=F32)
    yc = ((hf_ref[...] + hb_ref[...]) * jax.nn.gelu(zz[:, 0:d])).astype(BF16)
    mix = (jax.nn.sigmoid(zz[:, d:2 * d]) * jnp.dot(ya_ref[...], wa_ref[...], preferred_element_type=F32)
           + jax.nn.sigmoid(zz[:, 2 * d:3 * d]) * jnp.dot(yb_ref[...], wb_ref[...], preferred_element_type=F32)
           + jax.nn.sigmoid(zz[:, 3 * d:4 * d]) * jnp.dot(yc, wc_ref[...], preferred_element_type=F32))
    o = jnp.dot(mix.astype(BF16), wo_ref[...], preferred_element_type=F32)
    o_ref[...] = _layernorm(ALPHA * x + gate_ref[...] * o, lng_ref[...], lnb_ref[...])


def _merge(x, sc, sh, gate, ya, yb, hf, hb, wyg, wa, wb, wc, wo, lng, lnb, tm):
    bsz, rows, d = x.shape
    mod_spec = pl.BlockSpec((None, 1, d), lambda b, i: (b, 0, 0))
    vec = _resident((1, d), lambda b, i: (0, 0))

    def tile(cols):
        return pl.BlockSpec((None, tm, cols), lambda b, i: (b, i, 0))

    def weight(w):
        return _resident(w.shape, lambda b, i: (0, 0))

    return pl.pallas_call(
        _merge_kernel,
        grid=(bsz, rows // tm),
        in_specs=[tile(d), mod_spec, mod_spec, mod_spec, tile(Q_COLS), tile(Q_COLS), tile(d), tile(d),
                  weight(wyg), weight(wa), weight(wb), weight(wc), weight(wo), vec, vec],
        out_specs=tile(d),
        out_shape=jax.ShapeDtypeStruct((bsz, rows, d), F32),
        compiler_params=_params("arbitrary", "arbitrary"),
        name="merge_ln",
    )(x, sc, sh, gate, ya, yb, hf, hb, wyg, wa, wb, wc, wo, lng, lnb)


def _mlp_kernel(x_ref, sc_ref, sh_ref, gate_ref, w1_ref, w2_ref, lng_ref, lnb_ref, o_ref):
    x = x_ref[...]
    h = _modulate(x, sc_ref, sh_ref)
    u = jnp.dot(h, w1_ref[...], preferred_element_type=F32)
    u = jnp.square(jnp.maximum(u, 0.0)).astype(BF16)
    o = jnp.dot(u, w2_ref[...], preferred_element_type=F32)
    o_ref[...] = _layernorm(ALPHA * x + gate_ref[...] * o, lng_ref[...], lnb_ref[...])


def _mlp(x, sc, sh, gate, w1, w2, lng, lnb, tm):
    bsz, rows, d = x.shape
    mod_spec = pl.BlockSpec((None, 1, d), lambda b, i: (b, 0, 0))
    vec = _resident((1, d), lambda b, i: (0, 0))
    tile = pl.BlockSpec((None, tm, d), lambda b, i: (b, i, 0))
    return pl.pallas_call(
        _mlp_kernel,
        grid=(bsz, rows // tm),
        in_specs=[tile, mod_spec, mod_spec, mod_spec,
                  _resident(w1.shape, lambda b, i: (0, 0)), _resident(w2.shape, lambda b, i: (0, 0)), vec, vec],
        out_specs=tile,
        out_shape=jax.ShapeDtypeStruct((bsz, rows, d), F32),
        compiler_params=_params("arbitrary", "arbitrary"),
        name="mlp_ln",
    )(x, sc, sh, gate, w1, w2, lng, lnb)


def _rope_tables(n):
    pos = jnp.arange(n)
    row = (pos // GRID_W).astype(F32)
    colp = (pos % GRID_W).astype(F32)
    nf = HEAD_DIM // 4
    inv = ROPE_BASE ** (-jnp.arange(nf, dtype=F32) / nf)
    ang_r = row[:, None] * inv
    ang_c = colp[:, None] * inv
    cos_t = jnp.concatenate([jnp.cos(ang_r), jnp.cos(ang_r), jnp.cos(ang_c), jnp.cos(ang_c)], axis=-1)
    sin_t = jnp.concatenate([-jnp.sin(ang_r), jnp.sin(ang_r), -jnp.sin(ang_c), jnp.sin(ang_c)], axis=-1)
    return jnp.tile(cos_t, (1, 2)), jnp.tile(sin_t, (1, 2))


def _gate_weights(w_r, w_i):
    def pair(w):
        w = w.reshape(LRU_BLOCKS // 2, 2, LRU_BLOCK_DIM, LRU_BLOCK_DIM)
        z = jnp.zeros_like(w[:, 0])
        top = jnp.concatenate([w[:, 0], z], axis=-1)
        bot = jnp.concatenate([z, w[:, 1]], axis=-1)
        return jnp.concatenate([top, bot], axis=-2)
    return jnp.concatenate([pair(w_r), pair(w_i)], axis=-1).astype(BF16)


def kernel(x, c, ctx, c_ctx, w_ada, b_ada, w_in, a_sink, b_q_gain, b_k_gain, c_conv_w, c_conv_b,
           c_wr, c_br, c_wi, c_bi, c_lam, w_br_a, w_br_b, w_br_c, w_out, ln1_g, ln1_b,
           w_ff1, w_ff2, ln2_g, ln2_b):
    bsz, n, d = x.shape
    m = ctx.shape[1]
    depth = w_ada.shape[0]

    cc = jnp.zeros((SUBLANES, d), F32).at[:bsz].set(c).at[bsz].set(c_ctx)
    mod = _ada(cc, w_ada, b_ada)
    rope_tabs = _rope_tables(n)
    blk = jnp.arange(LANES) // HEAD_DIM
    ones = (blk[:, None] == blk[None, :]).astype(BF16)
    h_zero = jnp.zeros((bsz, 1, d), F32)

    for l in range(depth):
        with_ctx = l < depth - 1
        mod_lat = mod[l, :bsz].reshape(bsz, 1, 6 * d)
        mod_ctx = jnp.broadcast_to(mod[l, bsz].reshape(1, 1, 6 * d), (bsz, 1, 6 * d))
        sh1, sc1, g1, sh2, sc2, g2 = [mod_lat[:, :, k * d:(k + 1) * d] for k in range(6)]
        csh1, csc1, cg1, csh2, csc2, cg2 = [mod_ctx[:, :, k * d:(k + 1) * d] for k in range(6)]

        w_attn = w_in[l, :, :ATTN_COLS].astype(BF16)
        w_rnn = w_in[l, :, ATTN_COLS:ATTN_COLS + LRU_WIDTH].astype(BF16)
        w_yg = w_in[l, :, ATTN_COLS + LRU_WIDTH:].astype(BF16)
        gq = jnp.tile(b_q_gain[l], 2).reshape(1, LANES)
        gk = jnp.tile(b_k_gain[l], 2).reshape(1, LANES)

        qa, ka, va, qb, kb, vb = _inproj_attn(x, sc1, sh1, w_attn, gq, gk, ones, rope_tabs, 512)
        qa_c, ka_c, va_c, qb_c, kb_c, vb_c = _inproj_attn(ctx, csc1, csh1, w_attn, gq, gk, ones, None, m)

        ya = _attn_window(qa, ka, va, ka_c, va_c, a_sink[l])
        yb = _attn_global(qb, [(kb, vb), (kb_c, vb_c)], None, BLOCK, "attn_global")

        xr = _inproj_rnn(x, sc1, sh1, w_rnn, 512)
        xr_c = _inproj_rnn(ctx, csc1, csh1, w_rnn, m)
        cb = c_conv_b[l].reshape(1, d)
        hs, hs_c = [], []
        for di, reverse in enumerate((False, True)):
            wg = _gate_weights(c_wr[l, di], c_wi[l, di])
            vecs = [v[l, di].reshape(1, d) for v in (c_br, c_bi, c_lam)]
            h_c, h_fin = _rglru(xr_c, h_zero, c_conv_w[l], cb, wg, *vecs, reverse, m)
            h_l, _ = _rglru(xr, h_fin, c_conv_w[l], cb, wg, *vecs, reverse, 256)
            hs.append(h_l)
            hs_c.append(h_c)

        merge_w = [w_yg] + [w[l].astype(BF16) for w in (w_br_a, w_br_b, w_br_c, w_out)]
        ln1 = [ln1_g[l].reshape(1, d), ln1_b[l].reshape(1, d)]
        ln2 = [ln2_g[l].reshape(1, d), ln2_b[l].reshape(1, d)]
        w1 = w_ff1[l].astype(BF16)
        w2 = w_ff2[l].astype(BF16)

        x = _merge(x, sc1, sh1, g1, ya, yb, hs[0], hs[1], *merge_w, *ln1, 256)
        x = _mlp(x, sc2, sh2, g2, w1, w2, *ln2, 256)
        if with_ctx:
            ya_c = _attn_global(qa_c, [(ka_c, va_c)], a_sink[l], BLOCK, "attn_ctx_a")
            yb_c = _attn_global(qb_c, [(kb_c, vb_c)], None, BLOCK, "attn_ctx_b")
            ctx = _merge(ctx, csc1, csh1, cg1, ya_c, yb_c, hs_c[0], hs_c[1], *merge_w, *ln1, m)
            ctx = _mlp(ctx, csc2, csh2, cg2, w1, w2, *ln2, m)
    return x
```

```python
import functools

import jax
import jax.numpy as jnp
from jax import lax
from jax.experimental import pallas as pl
from jax.experimental.pallas import tpu as pltpu

D_MODEL = 1024
DEPTH = 2
GRID_W = 64
HEAD_DIM = 64
ROPE_BASE = 10000.0
BLOCK = 128
A_HEADS = 8
A_KV = 2
B_HEADS = 8
B_KV = 2
LRU_WIDTH = D_MODEL
LRU_BLOCKS = 16
LRU_BLOCK_DIM = LRU_WIDTH // LRU_BLOCKS
CONV_W = 4
LRU_C = 8.0
D_FF = 4 * D_MODEL
ALPHA = (2 * DEPTH) ** 0.25
LN_EPS = 1e-5
RMS_EPS = 1e-6
NEG_INF = -1e30
LOG2_E = 1.4426950408889634
Q_COLS = A_HEADS * HEAD_DIM
KV_COLS = A_KV * HEAD_DIM
ATTN_COLS = 2 * (Q_COLS + 2 * KV_COLS)
GROUP = A_HEADS // A_KV

LANES = 128
SUBLANES = 8
V7X_VMEM_BYTES = 64 * 1024 * 1024
VMEM_LIMIT_BYTES = V7X_VMEM_BYTES - 8 * 1024 * 1024

SCAN_TILE = 256
SEG_LEN = SCAN_TILE // SUBLANES
SQRT_FLOOR = 1e-30

F32 = jnp.float32
BF16 = jnp.bfloat16


def _params(*semantics):
    return pltpu.CompilerParams(dimension_semantics=semantics, vmem_limit_bytes=VMEM_LIMIT_BYTES)


def _resident(block_shape, index_map):
    return pl.BlockSpec(block_shape, index_map, pipeline_mode=pl.Buffered(1))


def _modulate(x, sc_ref, sh_ref):
    return (x * (1.0 + sc_ref[...]) + sh_ref[...]).astype(BF16)


def _layernorm(y, g, b):
    mu = jnp.mean(y, axis=-1, keepdims=True)
    yc = y - mu
    var = jnp.mean(yc * yc, axis=-1, keepdims=True)
    return yc * lax.rsqrt(var + LN_EPS) * g + b


def _ada_kernel(c_ref, w_ref, b_ref, o_ref):
    c = c_ref[...]
    s = c * jax.nn.sigmoid(c)
    o_ref[...] = jnp.dot(s, w_ref[...], preferred_element_type=F32,
                         precision=lax.Precision.HIGHEST) + b_ref[...]


def _ada(cc, w_ada, b_ada):
    depth, d, cols = w_ada.shape
    tn = cols // 4
    return pl.pallas_call(
        _ada_kernel,
        grid=(depth, cols // tn),
        in_specs=[pl.BlockSpec((SUBLANES, d), lambda l, j: (0, 0)),
                  pl.BlockSpec((None, d, tn), lambda l, j: (l, 0, j)),
                  pl.BlockSpec((None, 1, tn), lambda l, j: (l, 0, j))],
        out_specs=pl.BlockSpec((None, SUBLANES, tn), lambda l, j: (l, 0, j)),
        out_shape=jax.ShapeDtypeStruct((depth, SUBLANES, cols), F32),
        compiler_params=_params("arbitrary", "arbitrary"),
        name="ada_mod",
    )(cc, w_ada, b_ada.reshape(depth, 1, cols))


def _inproj_attn_kernel(*refs, rope):
    if rope:
        (x_ref, sc_ref, sh_ref, w_ref, gq_ref, gk_ref, ones_ref, cos_ref, sin_ref,
         qa_ref, ka_ref, va_ref, qb_ref, kb_ref, vb_ref) = refs
    else:
        (x_ref, sc_ref, sh_ref, w_ref, gq_ref, gk_ref, ones_ref,
         qa_ref, ka_ref, va_ref, qb_ref, kb_ref, vb_ref) = refs
    h = _modulate(x_ref[...], sc_ref, sh_ref)
    z = jnp.dot(h, w_ref[...], preferred_element_type=F32)
    tm = z.shape[0]
    lane = lax.broadcasted_iota(jnp.int32, (tm, LANES), 1)
    first16 = (lane & 31) < 16
    ones = ones_ref[...]
    scale = HEAD_DIM ** -0.5 * LOG2_E

    def rot(zc):
        if not rope:
            return zc
        partner = jnp.where(first16, pltpu.roll(zc, LANES - 16, 1), pltpu.roll(zc, 16, 1))
        return zc * cos_ref[...] + partner * sin_ref[...]

    def rms(zc, g_ref):
        sq = zc * zc
        hi = sq.astype(BF16)
        lo = (sq - hi.astype(F32)).astype(BF16)
        ss = (jnp.dot(hi, ones, preferred_element_type=F32)
              + jnp.dot(lo, ones, preferred_element_type=F32))
        return zc * lax.rsqrt(ss * (1.0 / HEAD_DIM) + RMS_EPS) * g_ref[...]

    def col(start):
        return z[:, start:start + LANES]

    for c in range(Q_COLS // LANES):
        qa_ref[:, c * LANES:(c + 1) * LANES] = (rot(col(c * LANES)) * scale).astype(BF16)
    base = Q_COLS
    ka_ref[...] = rot(col(base)).astype(BF16)
    va_ref[...] = col(base + KV_COLS).T.astype(BF16)
    base = Q_COLS + 2 * KV_COLS
    for c in range(Q_COLS // LANES):
        qb_ref[:, c * LANES:(c + 1) * LANES] = (
            rot(rms(col(base + c * LANES), gq_ref)) * scale).astype(BF16)
    base += Q_COLS
    kb_ref[...] = rot(rms(col(base), gk_ref)).astype(BF16)
    vb_ref[...] = col(base + KV_COLS).T.astype(BF16)


def _inproj_attn(x, sc, sh, w, gq, gk, ones, rope_tabs, tm):
    bsz, rows, d = x.shape
    rope = rope_tabs is not None
    mod_spec = pl.BlockSpec((None, 1, d), lambda b, i: (b, 0, 0))
    vec_spec = _resident((1, LANES), lambda b, i: (0, 0))
    in_specs = [pl.BlockSpec((None, tm, d), lambda b, i: (b, i, 0)), mod_spec, mod_spec,
                _resident((d, ATTN_COLS), lambda b, i: (0, 0)), vec_spec, vec_spec,
                _resident((LANES, LANES), lambda b, i: (0, 0))]
    args = [x, sc, sh, w, gq, gk, ones]
    if rope:
        tab_spec = pl.BlockSpec((tm, LANES), lambda b, i: (i, 0))
        in_specs += [tab_spec, tab_spec]
        args += list(rope_tabs)

    def out(cols):
        return (pl.BlockSpec((None, tm, cols), lambda b, i: (b, i, 0)),
                jax.ShapeDtypeStruct((bsz, rows, cols), BF16))

    out_t = (pl.BlockSpec((None, None, KV_COLS, tm), lambda b, i: (b, i, 0, 0)),
             jax.ShapeDtypeStruct((bsz, rows // tm, KV_COLS, tm), BF16))
    outs = [out(Q_COLS), out(KV_COLS), out_t, out(Q_COLS), out(KV_COLS), out_t]
    return pl.pallas_call(
        functools.partial(_inproj_attn_kernel, rope=rope),
        grid=(bsz, rows // tm),
        in_specs=in_specs,
        out_specs=[o[0] for o in outs],
        out_shape=[o[1] for o in outs],
        compiler_params=_params("arbitrary", "arbitrary"),
        name="inproj_attn_rope" if rope else "inproj_attn_ctx",
    )(*args)


def _inproj_rnn_kernel(xp_ref, xc_ref, xn_ref, sc_ref, sh_ref, w_ref, perm_ref, cw_ref, cb_ref, o_ref):
    i = pl.program_id(1)
    nt = pl.num_programs(1)
    rows_t = xc_ref.shape[0]
    w = w_ref[...]
    h = _modulate(xc_ref[...], sc_ref, sh_ref)
    hp = jnp.dot(perm_ref[...], h, preferred_element_type=F32).astype(BF16)
    z = jnp.dot(hp, w, preferred_element_type=F32)
    zp = jnp.dot(_modulate(xp_ref[...], sc_ref, sh_ref), w, preferred_element_type=F32)
    zn = jnp.dot(_modulate(xn_ref[...], sc_ref, sh_ref), w, preferred_element_type=F32)
    zp = jnp.where(i > 0, zp, 0.0)
    zn = jnp.where(i < nt - 1, zn, 0.0)
    sub = lax.broadcasted_iota(jnp.int32, (SUBLANES, z.shape[1]), 0)
    first, second, last = z[0:SUBLANES], z[SUBLANES:2 * SUBLANES], z[rows_t - SUBLANES:]
    before = jnp.where(sub == 0, zp[SUBLANES - 1:SUBLANES], pltpu.roll(last, 1, 0))
    after1 = jnp.where(sub == SUBLANES - 1, zn[0:1], pltpu.roll(first, SUBLANES - 1, 0))
    after2 = jnp.where(sub == SUBLANES - 1, zn[1:2], pltpu.roll(second, SUBLANES - 1, 0))
    taps = [jnp.concatenate([before, z[:rows_t - SUBLANES]], axis=0),
            z,
            jnp.concatenate([z[SUBLANES:], after1], axis=0),
            jnp.concatenate([z[2 * SUBLANES:], after1, after2], axis=0)]
    out = cb_ref[...]
    for tap in range(CONV_W):
        out = out + cw_ref[tap:tap + 1, :] * taps[tap]
    o_ref[...] = out


def _inproj_rnn(x, sc, sh, w, perm, cw, cb):
    bsz, rows, d = x.shape
    cols = w.shape[1]
    nt = rows // SCAN_TILE
    hb = SCAN_TILE // SUBLANES
    mod_spec = pl.BlockSpec((None, 1, d), lambda b, i: (b, 0, 0))
    vec = _resident((1, cols), lambda b, i: (0, 0))
    return pl.pallas_call(
        _inproj_rnn_kernel,
        grid=(bsz, nt),
        in_specs=[pl.BlockSpec((None, SUBLANES, d), lambda b, i: (b, jnp.maximum(i * hb - 1, 0), 0)),
                  pl.BlockSpec((None, SCAN_TILE, d), lambda b, i: (b, i, 0)),
                  pl.BlockSpec((None, SUBLANES, d),
                               lambda b, i: (b, jnp.minimum((i + 1) * hb, rows // SUBLANES - 1), 0)),
                  mod_spec, mod_spec, _resident((d, cols), lambda b, i: (0, 0)),
                  _resident((SCAN_TILE, SCAN_TILE), lambda b, i: (0, 0)),
                  _resident((CONV_W, cols), lambda b, i: (0, 0)), vec],
        out_specs=pl.BlockSpec((None, SCAN_TILE, cols), lambda b, i: (b, i, 0)),
        out_shape=jax.ShapeDtypeStruct((bsz, rows, cols), F32),
        compiler_params=_params("arbitrary", "arbitrary"),
        name="inproj_rnn_conv",
    )(x, x, x, sc, sh, w, perm, cw, cb)


def _attn_kernel(*refs, segs, has_sink, tq, lookahead):
    refs = list(refs)
    sink_ref = refs.pop(0) if has_sink else None
    q_ref = refs.pop(0)
    qn_ref = refs.pop(0) if lookahead else None
    kv_refs = [(refs.pop(0), refs.pop(0)) for _ in segs]
    o_ref = refs.pop(0)
    s_scr, m_scr, l_scr, acc_scr = [refs[g * A_KV:(g + 1) * A_KV] for g in range(4)]
    i = pl.program_id(1)
    nb = pl.num_programs(1)
    width = GROUP * tq

    def seg_mask(kind):
        if kind is None:
            return None
        j = lax.broadcasted_iota(jnp.int32, (BLOCK, width), 0)
        r = lax.broadcasted_iota(jnp.int32, (BLOCK, width), 1) & (tq - 1)
        far = jnp.int32(4 * BLOCK)
        if kind == "prev":
            return (j - r) >= jnp.where(i > 0, 0, far)
        return (r - j) >= jnp.where(i < nb - 1, 0, far)

    def sink_row(kvi):
        return jnp.concatenate(
            [jnp.full((1, tq), sink_ref[GROUP * kvi + hh] * LOG2_E, F32) for hh in range(GROUP)], axis=1)

    def query_operand(ref, kvi):
        qt = ref[...].T
        heads = jnp.concatenate(
            [qt[(GROUP * kvi + hh) * HEAD_DIM:(GROUP * kvi + hh + 1) * HEAD_DIM, :] for hh in range(GROUP)],
            axis=1)
        parts = [jnp.zeros((HEAD_DIM, width), BF16)] * A_KV
        parts[kvi] = heads
        return jnp.concatenate(parts, axis=0)

    def for_each_chunk(*fns):
        base = 0
        for (nchunks, kc, kind), (k_ref, vt_ref) in zip(segs, kv_refs):
            if nchunks == 1:
                for fn in fns:
                    fn(k_ref, vt_ref, 0, 0, base, kc, kind)
            else:
                def body(ci, carry, k_ref=k_ref, vt_ref=vt_ref, kc=kc, base=base):
                    r0 = pl.multiple_of(ci * kc, kc)
                    for fn in fns:
                        fn(k_ref, vt_ref, r0, ci, base + r0, kc, None)
                    return carry
                lax.fori_loop(0, nchunks, body, 0, unroll=4)
            base += nchunks * kc

    def start_scores(kvi):
        m_scr[kvi][...] = sink_row(kvi) if has_sink else jnp.full((1, width), NEG_INF, F32)

    def scores(kvi, qz, k_ref, vt_ref, r0, ci, srow, kc, kind):
        s = jnp.dot(k_ref[pl.ds(r0, kc), :], qz, preferred_element_type=F32)
        mask = seg_mask(kind)
        if mask is not None:
            s = jnp.where(mask, s, NEG_INF)
        s_scr[kvi][pl.ds(srow, kc), :] = s
        m_scr[kvi][...] = jnp.maximum(m_scr[kvi][...], jnp.max(s, axis=0, keepdims=True))

    def start_values(kvi):
        if has_sink:
            l_scr[kvi][...] = jnp.exp2(sink_row(kvi) - m_scr[kvi][...])
        else:
            l_scr[kvi][...] = jnp.zeros((1, width), F32)
        acc_scr[kvi][...] = jnp.zeros((HEAD_DIM, width), F32)

    def values(kvi, k_ref, vt_ref, r0, ci, srow, kc, kind):
        p = jnp.exp2(s_scr[kvi][pl.ds(srow, kc), :] - m_scr[kvi][...])
        l_scr[kvi][...] += jnp.sum(p, axis=0, keepdims=True)
        acc_scr[kvi][...] += jnp.dot(vt_ref[ci, kvi * HEAD_DIM:(kvi + 1) * HEAD_DIM, :], p.astype(BF16),
                                     preferred_element_type=F32)

    def first_scores():
        start_scores(0)
        for_each_chunk(functools.partial(scores, 0, query_operand(q_ref, 0)))

    if lookahead:
        pl.when(i == 0)(first_scores)
    else:
        first_scores()
    start_scores(1)
    start_values(0)
    for_each_chunk(functools.partial(scores, 1, query_operand(q_ref, 1)), functools.partial(values, 0))
    start_values(1)
    if lookahead:
        start_scores(0)
        for_each_chunk(functools.partial(values, 1), functools.partial(scores, 0, query_operand(qn_ref, 0)))
    else:
        for_each_chunk(functools.partial(values, 1))

    outs = []
    for kvi in range(A_KV):
        o = acc_scr[kvi][...] / l_scr[kvi][...]
        outs += [o[:, hh * tq:(hh + 1) * tq] for hh in range(GROUP)]
    o_ref[...] = jnp.concatenate(outs, axis=0).T.astype(BF16)


def _attention(q, seg_arrays, seg_specs, segs, sink, tq, name, lookahead=False):
    bsz, rows, _ = q.shape
    nb = rows // tq
    has_sink = sink is not None
    in_specs, args = [], []
    if has_sink:
        in_specs.append(pl.BlockSpec(memory_space=pltpu.SMEM))
        args.append(sink)
    in_specs.append(pl.BlockSpec((None, tq, Q_COLS), lambda b, i: (b, i, 0)))
    args.append(q)
    if lookahead:
        in_specs.append(pl.BlockSpec((None, tq, Q_COLS), lambda b, i: (b, jnp.minimum(i + 1, nb - 1), 0)))
        args.append(q)
    for (k, vt), (k_spec, vt_spec) in zip(seg_arrays, seg_specs):
        in_specs += [k_spec, vt_spec]
        args += [k, vt]
    width = GROUP * tq
    nk_total = sum(nchunks * kc for nchunks, kc, _ in segs)
    per_head = [(nk_total, width), (1, width), (1, width), (HEAD_DIM, width)]
    return pl.pallas_call(
        functools.partial(_attn_kernel, segs=tuple(segs), has_sink=has_sink, tq=tq, lookahead=lookahead),
        grid=(bsz, nb),
        in_specs=in_specs,
        out_specs=pl.BlockSpec((None, tq, Q_COLS), lambda b, i: (b, i, 0)),
        out_shape=jax.ShapeDtypeStruct((bsz, rows, Q_COLS), BF16),
        scratch_shapes=[pltpu.VMEM(shape, F32) for shape in per_head for _ in range(A_KV)],
        compiler_params=_params("arbitrary", "arbitrary"),
        name=name,
    )(*args)


def _whole_seg(k, vt):
    nk = k.shape[1]
    nchunks, kc = vt.shape[1], vt.shape[3]
    specs = (pl.BlockSpec((None, nk, KV_COLS), lambda b, i: (b, 0, 0)),
             pl.BlockSpec((None, nchunks, KV_COLS, kc), lambda b, i: (b, 0, 0, 0)))
    return specs, (nchunks, kc, None)


def _attn_window(q, k, vt, k_ctx, vt_ctx, sink):
    nb = q.shape[1] // BLOCK
    per_slab = vt.shape[3] // BLOCK

    def block_specs(pos):
        return (pl.BlockSpec((None, BLOCK, KV_COLS), lambda b, i: (b, pos(i), 0)),
                pl.BlockSpec((None, 1, KV_COLS, BLOCK),
                             lambda b, i: (b, pos(i) // per_slab, 0, pos(i) % per_slab)))

    ctx_specs, ctx_seg = _whole_seg(k_ctx, vt_ctx)
    specs = [block_specs(lambda i: i),
             block_specs(lambda i: jnp.maximum(i - 1, 0)),
             block_specs(lambda i: jnp.minimum(i + 1, nb - 1)),
             ctx_specs]
    segs = [(1, BLOCK, None), (1, BLOCK, "prev"), (1, BLOCK, "next"), ctx_seg]
    return _attention(q, [(k, vt)] * 3 + [(k_ctx, vt_ctx)], specs, segs, sink, BLOCK, "attn_window")


def _attn_global(q, kv_list, sink, tq, name, lookahead=False):
    specs, segs = zip(*[_whole_seg(k, vt) for k, vt in kv_list])
    return _attention(q, kv_list, specs, segs, sink, tq, name, lookahead)


def _rglru_kernel(x_ref, wg_ref, br_ref, bi_ref, lam_ref, h0_ref,
                  h_ref, hfin_ref, carry_scr, decay_scr, *, reverse):
    j = pl.program_id(1)
    nt = pl.num_programs(1)
    rows_t, width = x_ref.shape
    ngroups = rows_t // SUBLANES

    @pl.when(j == 0)
    def _():
        carry_scr[...] = h0_ref[...]

    lam = lam_ref[...]
    neg_softplus = -(jnp.maximum(-lam, 0.0) + jnp.log1p(jnp.exp(-jnp.abs(lam))))
    sub = lax.broadcasted_iota(jnp.int32, (SUBLANES, LANES), 0)
    order = range(ngroups - 1, -1, -1) if reverse else range(ngroups)

    def rows(g):
        return slice(g * SUBLANES, (g + 1) * SUBLANES)

    for c in range(width // LANES):
        cs = slice(c * LANES, (c + 1) * LANES)
        xcc = x_ref[:, cs]
        g = jnp.dot(xcc.astype(BF16), wg_ref[c], preferred_element_type=F32)
        r = jax.nn.sigmoid(g[:, :LANES] + br_ref[:, cs])
        ig = jax.nn.sigmoid(g[:, LANES:] + bi_ref[:, cs])
        log_a = (LRU_C * r) * neg_softplus[:, cs]
        a = jnp.exp(log_a)
        w = -jnp.tanh(log_a) * (a * a + 1.0)
        u = (w * lax.rsqrt(jnp.maximum(w, SQRT_FLOOR))) * (ig * xcc)

        h_loc = decay = None
        for gi in order:
            a_g, u_g = a[rows(gi)], u[rows(gi)]
            h_loc = u_g if h_loc is None else a_g * h_loc + u_g
            decay = a_g if decay is None else a_g * decay
            h_ref[rows(gi), cs] = h_loc
            decay_scr[rows(gi), cs] = decay

        end, dec = h_loc, decay
        for s in (1, 2, 4):
            if reverse:
                ok = sub < SUBLANES - s
                shift = SUBLANES - s
            else:
                ok = sub >= s
                shift = s
            end = end + dec * jnp.where(ok, pltpu.roll(end, shift, 0), 0.0)
            dec = dec * jnp.where(ok, pltpu.roll(dec, shift, 0), 1.0)
        carry = carry_scr[:, cs]
        seg_end = end + dec * carry
        if reverse:
            start = jnp.where(sub == SUBLANES - 1, carry, pltpu.roll(seg_end, SUBLANES - 1, 0))
            carry_scr[:, cs] = seg_end[0:1]
        else:
            start = jnp.where(sub == 0, carry, pltpu.roll(seg_end, 1, 0))
            carry_scr[:, cs] = seg_end[SUBLANES - 1:SUBLANES]
        for gi in order:
            h_ref[rows(gi), cs] = h_ref[rows(gi), cs] + decay_scr[rows(gi), cs] * start

    @pl.when(j == nt - 1)
    def _():
        hfin_ref[...] = carry_scr[...]


def _rglru(xs, h0, wg, br, bi, lam, reverse):
    bsz, rows, width = xs.shape
    nt = rows // SCAN_TILE

    def tpos(j):
        return nt - 1 - j if reverse else j

    vec = _resident((1, width), lambda b, j: (0, 0))
    tile = pl.BlockSpec((None, SCAN_TILE, width), lambda b, j: (b, tpos(j), 0))
    state = pl.BlockSpec((None, 1, width), lambda b, j: (b, 0, 0))
    return pl.pallas_call(
        functools.partial(_rglru_kernel, reverse=reverse),
        grid=(bsz, nt),
        in_specs=[tile, _resident(wg.shape, lambda b, j: (0, 0, 0)), vec, vec, vec, state],
        out_specs=[tile, state],
        out_shape=[jax.ShapeDtypeStruct((bsz, rows, width), F32),
                   jax.ShapeDtypeStruct((bsz, 1, width), F32)],
        scratch_shapes=[pltpu.VMEM((1, width), F32), pltpu.VMEM((SCAN_TILE, width), F32)],
        compiler_params=_params("arbitrary", "arbitrary"),
        name="rglru_bwd" if reverse else "rglru_fwd",
    )(xs, wg, br, bi, lam, h0)


def _merge_kernel(x_ref, sc_ref, sh_ref, gate_ref, ya_ref, yb_ref, hf_ref, hb_ref, perm_ref, perm_t_ref,
                  wy_ref, wg_ref, wa_ref, wb_ref, wc_ref, wo_ref, lng_ref, lnb_ref, o_ref):
    x = x_ref[...]
    d = x.shape[1]
    h = _modulate(x, sc_ref, sh_ref)
    hp = jnp.dot(perm_ref[...], h, preferred_element_type=F32).astype(BF16)
    y_rnn = jnp.dot(hp, wy_ref[...], preferred_element_type=F32)
    yc = ((hf_ref[...] + hb_ref[...]) * jax.nn.gelu(y_rnn)).astype(BF16)
    yc = jnp.dot(perm_t_ref[...], yc, preferred_element_type=F32).astype(BF16)
    zz = jnp.dot(h, wg_ref[...], preferred_element_type=F32)
    mix = (jax.nn.sigmoid(zz[:, 0:d]) * jnp.dot(ya_ref[...], wa_ref[...], preferred_element_type=F32)
           + jax.nn.sigmoid(zz[:, d:2 * d]) * jnp.dot(yb_ref[...], wb_ref[...], preferred_element_type=F32)
           + jax.nn.sigmoid(zz[:, 2 * d:3 * d]) * jnp.dot(yc, wc_ref[...], preferred_element_type=F32))
    o = jnp.dot(mix.astype(BF16), wo_ref[...], preferred_element_type=F32)
    o_ref[...] = _layernorm(ALPHA * x + gate_ref[...] * o, lng_ref[...], lnb_ref[...])


def _merge(x, sc, sh, gate, ya, yb, hf, hb, perm, perm_t, wy, wg, wa, wb, wc, wo, lng, lnb):
    bsz, rows, d = x.shape
    mod_spec = pl.BlockSpec((None, 1, d), lambda b, i: (b, 0, 0))
    vec = _resident((1, d), lambda b, i: (0, 0))

    def tile(cols):
        return pl.BlockSpec((None, SCAN_TILE, cols), lambda b, i: (b, i, 0))

    def weight(w):
        return _resident(w.shape, lambda b, i: (0, 0))

    weights = [perm, perm_t, wy, wg, wa, wb, wc, wo]
    return pl.pallas_call(
        _merge_kernel,
        grid=(bsz, rows // SCAN_TILE),
        in_specs=[tile(d), mod_spec, mod_spec, mod_spec, tile(Q_COLS), tile(Q_COLS), tile(d), tile(d)]
                 + [weight(w) for w in weights] + [vec, vec],
        out_specs=tile(d),
        out_shape=jax.ShapeDtypeStruct((bsz, rows, d), F32),
        compiler_params=_params("arbitrary", "arbitrary"),
        name="merge_ln",
    )(x, sc, sh, gate, ya, yb, hf, hb, *weights, lng, lnb)


def _mlp_kernel(x_ref, sc_ref, sh_ref, gate_ref, w1_ref, w2_ref, lng_ref, lnb_ref, o_ref):
    x = x_ref[...]
    h = _modulate(x, sc_ref, sh_ref)
    u = jnp.dot(h, w1_ref[...], preferred_element_type=F32)
    u = jnp.square(jnp.maximum(u, 0.0)).astype(BF16)
    o = jnp.dot(u, w2_ref[...], preferred_element_type=F32)
    o_ref[...] = _layernorm(ALPHA * x + gate_ref[...] * o, lng_ref[...], lnb_ref[...])


def _mlp(x, sc, sh, gate, w1, w2, lng, lnb, tm):
    bsz, rows, d = x.shape
    mod_spec = pl.BlockSpec((None, 1, d), lambda b, i: (b, 0, 0))
    vec = _resident((1, d), lambda b, i: (0, 0))
    tile = pl.BlockSpec((None, tm, d), lambda b, i: (b, i, 0))
    return pl.pallas_call(
        _mlp_kernel,
        grid=(bsz, rows // tm),
        in_specs=[tile, mod_spec, mod_spec, mod_spec,
                  _resident(w1.shape, lambda b, i: (0, 0)), _resident(w2.shape, lambda b, i: (0, 0)), vec, vec],
        out_specs=tile,
        out_shape=jax.ShapeDtypeStruct((bsz, rows, d), F32),
        compiler_params=_params("arbitrary", "arbitrary"),
        name="mlp_ln",
    )(x, sc, sh, gate, w1, w2, lng, lnb)


def _rope_tables(n):
    pos = jnp.arange(n)
    row = (pos // GRID_W).astype(F32)
    colp = (pos % GRID_W).astype(F32)
    nf = HEAD_DIM // 4
    inv = ROPE_BASE ** (-jnp.arange(nf, dtype=F32) / nf)
    ang_r = row[:, None] * inv
    ang_c = colp[:, None] * inv
    cos_t = jnp.concatenate([jnp.cos(ang_r), jnp.cos(ang_r), jnp.cos(ang_c), jnp.cos(ang_c)], axis=-1)
    sin_t = jnp.concatenate([-jnp.sin(ang_r), jnp.sin(ang_r), -jnp.sin(ang_c), jnp.sin(ang_c)], axis=-1)
    return jnp.tile(cos_t, (1, 2)), jnp.tile(sin_t, (1, 2))


def _gate_weights(w_r, w_i):
    def pair(w):
        w = w.reshape(LRU_BLOCKS // 2, 2, LRU_BLOCK_DIM, LRU_BLOCK_DIM)
        z = jnp.zeros_like(w[:, 0])
        top = jnp.concatenate([w[:, 0], z], axis=-1)
        bot = jnp.concatenate([z, w[:, 1]], axis=-1)
        return jnp.concatenate([top, bot], axis=-2)
    return jnp.concatenate([pair(w_r), pair(w_i)], axis=-1).astype(BF16)


def kernel(x, c, ctx, c_ctx, w_ada, b_ada, w_in, a_sink, b_q_gain, b_k_gain, c_conv_w, c_conv_b,
           c_wr, c_br, c_wi, c_bi, c_lam, w_br_a, w_br_b, w_br_c, w_out, ln1_g, ln1_b,
           w_ff1, w_ff2, ln2_g, ln2_b):
    bsz, n, d = x.shape
    m = ctx.shape[1]
    depth = w_ada.shape[0]

    cc = jnp.zeros((SUBLANES, d), F32).at[:bsz].set(c).at[bsz].set(c_ctx)
    mod = _ada(cc, w_ada, b_ada)
    rope_tabs = _rope_tables(n)
    blk = jnp.arange(LANES) // HEAD_DIM
    ones = (blk[:, None] == blk[None, :]).astype(BF16)
    h_zero = jnp.zeros((bsz, 1, d), F32)
    slot = jnp.arange(SCAN_TILE)
    token_of_slot = SEG_LEN * (slot % SUBLANES) + slot // SUBLANES
    perm = (token_of_slot[:, None] == slot[None, :]).astype(BF16)
    perm_t = perm.T

    for l in range(depth):
        with_ctx = l < depth - 1
        mod_lat = mod[l, :bsz].reshape(bsz, 1, 6 * d)
        mod_ctx = jnp.broadcast_to(mod[l, bsz].reshape(1, 1, 6 * d), (bsz, 1, 6 * d))
        sh1, sc1, g1, sh2, sc2, g2 = [mod_lat[:, :, k * d:(k + 1) * d] for k in range(6)]
        csh1, csc1, cg1, csh2, csc2, cg2 = [mod_ctx[:, :, k * d:(k + 1) * d] for k in range(6)]

        w_attn = w_in[l, :, :ATTN_COLS].astype(BF16)
        w_rnn = w_in[l, :, ATTN_COLS:ATTN_COLS + LRU_WIDTH].astype(BF16)
        w_y = w_in[l, :, ATTN_COLS + LRU_WIDTH:ATTN_COLS + 2 * LRU_WIDTH].astype(BF16)
        w_g = w_in[l, :, ATTN_COLS + 2 * LRU_WIDTH:].astype(BF16)
        gq = jnp.tile(b_q_gain[l], 2).reshape(1, LANES)
        gk = jnp.tile(b_k_gain[l], 2).reshape(1, LANES)

        qa, ka, va, qb, kb, vb = _inproj_attn(x, sc1, sh1, w_attn, gq, gk, ones, rope_tabs, 512)
        qa_c, ka_c, va_c, qb_c, kb_c, vb_c = _inproj_attn(ctx, csc1, csh1, w_attn, gq, gk, ones, None, m)

        ya = _attn_window(qa, ka, va, ka_c, va_c, a_sink[l])
        yb = _attn_global(qb, [(kb, vb), (kb_c, vb_c)], None, BLOCK, "attn_global", lookahead=True)

        cb = c_conv_b[l].reshape(1, d)
        xs = _inproj_rnn(x, sc1, sh1, w_rnn, perm, c_conv_w[l], cb)
        xs_c = _inproj_rnn(ctx, csc1, csh1, w_rnn, perm, c_conv_w[l], cb)
        hs, hs_c = [], []
        for di, reverse in enumerate((False, True)):
            wg = _gate_weights(c_wr[l, di], c_wi[l, di])
            vecs = [v[l, di].reshape(1, d) for v in (c_br, c_bi, c_lam)]
            h_c, h_fin = _rglru(xs_c, h_zero, wg, *vecs, reverse)
            h_l, _ = _rglru(xs, h_fin, wg, *vecs, reverse)
            hs.append(h_l)
            hs_c.append(h_c)

        merge_w = [perm, perm_t, w_y, w_g] + [w[l].astype(BF16) for w in (w_br_a, w_br_b, w_br_c, w_out)]
        ln1 = [ln1_g[l].reshape(1, d), ln1_b[l].reshape(1, d)]
        ln2 = [ln2_g[l].reshape(1, d), ln2_b[l].reshape(1, d)]
        w1 = w_ff1[l].astype(BF16)
        w2 = w_ff2[l].astype(BF16)

        x = _merge(x, sc1, sh1, g1, ya, yb, hs[0], hs[1], *merge_w, *ln1)
        x = _mlp(x, sc2, sh2, g2, w1, w2, *ln2, 256)
        if with_ctx:
            ya_c = _attn_global(qa_c, [(ka_c, va_c)], a_sink[l], BLOCK, "attn_ctx_a")
            yb_c = _attn_global(qb_c, [(kb_c, vb_c)], None, BLOCK, "attn_ctx_b")
            ctx = _merge(ctx, csc1, csh1, cg1, ya_c, yb_c, hs_c[0], hs_c[1], *merge_w, *ln1)
            ctx = _mlp(ctx, csc2, csh2, cg2, w1, w2, *ln2, m)
    return x
```

```python
import functools

import jax
import jax.numpy as jnp
from jax import lax
from jax.experimental import pallas as pl
from jax.experimental.pallas import tpu as pltpu

D_MODEL = 1024
DEPTH = 2
GRID_W = 64
HEAD_DIM = 64
ROPE_BASE = 10000.0
BLOCK = 128
A_HEADS = 8
A_KV = 2
B_HEADS = 8
B_KV = 2
LRU_WIDTH = D_MODEL
LRU_BLOCKS = 16
LRU_BLOCK_DIM = LRU_WIDTH // LRU_BLOCKS
CONV_W = 4
LRU_C = 8.0
D_FF = 4 * D_MODEL
ALPHA = (2 * DEPTH) ** 0.25
LN_EPS = 1e-5
RMS_EPS = 1e-6
NEG_INF = -1e30
LOG2_E = 1.4426950408889634
Q_COLS = A_HEADS * HEAD_DIM
KV_COLS = A_KV * HEAD_DIM
ATTN_COLS = 2 * (Q_COLS + 2 * KV_COLS)
GROUP = A_HEADS // A_KV

LANES = 128
SUBLANES = 8
V7X_VMEM_BYTES = 64 * 1024 * 1024
VMEM_LIMIT_BYTES = V7X_VMEM_BYTES - 8 * 1024 * 1024

SCAN_TILE = 256
SEG_LEN = SCAN_TILE // SUBLANES
STATIC_CHUNKS = 4
WINDOW_SUBTILES = 4
SQRT_FLOOR = 1e-30

F32 = jnp.float32
BF16 = jnp.bfloat16


def _params(*semantics):
    return pltpu.CompilerParams(dimension_semantics=semantics, vmem_limit_bytes=VMEM_LIMIT_BYTES)


def _resident(block_shape, index_map):
    return pl.BlockSpec(block_shape, index_map, pipeline_mode=pl.Buffered(1))


def _modulate(x, sc_ref, sh_ref):
    return (x * (1.0 + sc_ref[...]) + sh_ref[...]).astype(BF16)


def _layernorm(y, g, b):
    mu = jnp.mean(y, axis=-1, keepdims=True)
    yc = y - mu
    var = jnp.mean(yc * yc, axis=-1, keepdims=True)
    return yc * lax.rsqrt(var + LN_EPS) * g + b


def _ada_kernel(c_ref, w_ref, b_ref, o_ref):
    c = c_ref[...]
    s = c * jax.nn.sigmoid(c)
    o_ref[...] = jnp.dot(s, w_ref[...], preferred_element_type=F32,
                         precision=lax.Precision.HIGHEST) + b_ref[...]


def _ada(cc, w_ada, b_ada):
    depth, d, cols = w_ada.shape
    tn = cols // 4
    return pl.pallas_call(
        _ada_kernel,
        grid=(depth, cols // tn),
        in_specs=[pl.BlockSpec((SUBLANES, d), lambda l, j: (0, 0)),
                  pl.BlockSpec((None, d, tn), lambda l, j: (l, 0, j)),
                  pl.BlockSpec((None, 1, tn), lambda l, j: (l, 0, j))],
        out_specs=pl.BlockSpec((None, SUBLANES, tn), lambda l, j: (l, 0, j)),
        out_shape=jax.ShapeDtypeStruct((depth, SUBLANES, cols), F32),
        compiler_params=_params("arbitrary", "arbitrary"),
        name="ada_mod",
    )(cc, w_ada, b_ada.reshape(depth, 1, cols))


def _inproj_attn_kernel(*refs, rope):
    if rope:
        (x_ref, sc_ref, sh_ref, w_ref, gq_ref, gk_ref, ones_ref, cos_ref, sin_ref,
         qa_ref, ka_ref, va_ref, qb_ref, kb_ref, vb_ref) = refs
    else:
        (x_ref, sc_ref, sh_ref, w_ref, gq_ref, gk_ref, ones_ref,
         qa_ref, ka_ref, va_ref, qb_ref, kb_ref, vb_ref) = refs
    h = _modulate(x_ref[...], sc_ref, sh_ref)
    z = jnp.dot(h, w_ref[...], preferred_element_type=F32)
    tm = z.shape[0]
    lane = lax.broadcasted_iota(jnp.int32, (tm, LANES), 1)
    first16 = (lane & 31) < 16
    ones = ones_ref[...]
    scale = HEAD_DIM ** -0.5 * LOG2_E

    def rot(zc):
        if not rope:
            return zc
        partner = jnp.where(first16, pltpu.roll(zc, LANES - 16, 1), pltpu.roll(zc, 16, 1))
        return zc * cos_ref[...] + partner * sin_ref[...]

    def rms(zc, g_ref):
        sq = zc * zc
        hi = sq.astype(BF16)
        lo = (sq - hi.astype(F32)).astype(BF16)
        ss = (jnp.dot(hi, ones, preferred_element_type=F32)
              + jnp.dot(lo, ones, preferred_element_type=F32))
        return zc * lax.rsqrt(ss * (1.0 / HEAD_DIM) + RMS_EPS) * g_ref[...]

    def col(start):
        return z[:, start:start + LANES]

    for c in range(Q_COLS // LANES):
        qa_ref[:, c * LANES:(c + 1) * LANES] = (rot(col(c * LANES)) * scale).astype(BF16)
    base = Q_COLS
    ka_ref[...] = rot(col(base)).astype(BF16)
    va_t = col(base + KV_COLS).T.astype(BF16)
    for c in range(va_ref.shape[0]):
        va_ref[c] = va_t[:, c * BLOCK:(c + 1) * BLOCK]
    base = Q_COLS + 2 * KV_COLS
    for c in range(Q_COLS // LANES):
        qb_ref[:, c * LANES:(c + 1) * LANES] = (
            rot(rms(col(base + c * LANES), gq_ref)) * scale).astype(BF16)
    base += Q_COLS
    kb_ref[...] = rot(rms(col(base), gk_ref)).astype(BF16)
    vb_ref[0] = col(base + KV_COLS).T.astype(BF16)


def _inproj_attn(x, sc, sh, w, gq, gk, ones, rope_tabs, tm):
    bsz, rows, d = x.shape
    rope = rope_tabs is not None
    mod_spec = pl.BlockSpec((None, 1, d), lambda b, i: (b, 0, 0))
    vec_spec = _resident((1, LANES), lambda b, i: (0, 0))
    in_specs = [pl.BlockSpec((None, tm, d), lambda b, i: (b, i, 0)), mod_spec, mod_spec,
                _resident((d, ATTN_COLS), lambda b, i: (0, 0)), vec_spec, vec_spec,
                _resident((LANES, LANES), lambda b, i: (0, 0))]
    args = [x, sc, sh, w, gq, gk, ones]
    if rope:
        tab_spec = pl.BlockSpec((tm, LANES), lambda b, i: (i, 0))
        in_specs += [tab_spec, tab_spec]
        args += list(rope_tabs)

    def out(cols):
        return (pl.BlockSpec((None, tm, cols), lambda b, i: (b, i, 0)),
                jax.ShapeDtypeStruct((bsz, rows, cols), BF16))

    def out_t(slab):
        return (pl.BlockSpec((None, tm // slab, KV_COLS, slab), lambda b, i: (b, i, 0, 0)),
                jax.ShapeDtypeStruct((bsz, rows // slab, KV_COLS, slab), BF16))

    outs = [out(Q_COLS), out(KV_COLS), out_t(BLOCK), out(Q_COLS), out(KV_COLS), out_t(tm)]
    return pl.pallas_call(
        functools.partial(_inproj_attn_kernel, rope=rope),
        grid=(bsz, rows // tm),
        in_specs=in_specs,
        out_specs=[o[0] for o in outs],
        out_shape=[o[1] for o in outs],
        compiler_params=_params("arbitrary", "arbitrary"),
        name="inproj_attn_rope" if rope else "inproj_attn_ctx",
    )(*args)


def _inproj_rnn_kernel(xp_ref, xc_ref, xn_ref, sc_ref, sh_ref, w_ref, perm_ref, cw_ref, cb_ref, o_ref):
    i = pl.program_id(1)
    nt = pl.num_programs(1)
    rows_t = xc_ref.shape[0]
    h = _modulate(xc_ref[...], sc_ref, sh_ref)
    hp = jnp.dot(perm_ref[...], h, preferred_element_type=F32).astype(BF16)
    halo = _modulate(jnp.concatenate([xp_ref[...], xn_ref[...]], axis=0), sc_ref, sh_ref)
    lhs = jnp.concatenate([hp, halo], axis=0)
    zall = jnp.dot(lhs, w_ref[...], preferred_element_type=F32)
    z = zall[:rows_t]
    zp = jnp.where(i > 0, zall[rows_t:rows_t + SUBLANES], 0.0)
    zn = jnp.where(i < nt - 1, zall[rows_t + SUBLANES:], 0.0)
    sub = lax.broadcasted_iota(jnp.int32, (SUBLANES, z.shape[1]), 0)
    first, second, last = z[0:SUBLANES], z[SUBLANES:2 * SUBLANES], z[rows_t - SUBLANES:]
    before = jnp.where(sub == 0, zp[SUBLANES - 1:SUBLANES], pltpu.roll(last, 1, 0))
    after1 = jnp.where(sub == SUBLANES - 1, zn[0:1], pltpu.roll(first, SUBLANES - 1, 0))
    after2 = jnp.where(sub == SUBLANES - 1, zn[1:2], pltpu.roll(second, SUBLANES - 1, 0))
    taps = [jnp.concatenate([before, z[:rows_t - SUBLANES]], axis=0),
            z,
            jnp.concatenate([z[SUBLANES:], after1], axis=0),
            jnp.concatenate([z[2 * SUBLANES:], after1, after2], axis=0)]
    out = cb_ref[...]
    for tap in range(CONV_W):
        out = out + cw_ref[tap:tap + 1, :] * taps[tap]
    o_ref[...] = out


def _inproj_rnn(x, sc, sh, w, perm, cw, cb):
    bsz, rows, d = x.shape
    cols = w.shape[1]
    nt = rows // SCAN_TILE
    hb = SCAN_TILE // SUBLANES
    mod_spec = pl.BlockSpec((None, 1, d), lambda b, i: (b, 0, 0))
    vec = _resident((1, cols), lambda b, i: (0, 0))
    return pl.pallas_call(
        _inproj_rnn_kernel,
        grid=(bsz, nt),
        in_specs=[pl.BlockSpec((None, SUBLANES, d), lambda b, i: (b, jnp.maximum(i * hb - 1, 0), 0)),
                  pl.BlockSpec((None, SCAN_TILE, d), lambda b, i: (b, i, 0)),
                  pl.BlockSpec((None, SUBLANES, d),
                               lambda b, i: (b, jnp.minimum((i + 1) * hb, rows // SUBLANES - 1), 0)),
                  mod_spec, mod_spec, _resident((d, cols), lambda b, i: (0, 0)),
                  _resident((SCAN_TILE, SCAN_TILE), lambda b, i: (0, 0)),
                  _resident((CONV_W, cols), lambda b, i: (0, 0)), vec],
        out_specs=pl.BlockSpec((None, SCAN_TILE, cols), lambda b, i: (b, i, 0)),
        out_shape=jax.ShapeDtypeStruct((bsz, rows, cols), F32),
        compiler_params=_params("arbitrary", "arbitrary"),
        name="inproj_rnn_conv",
    )(x, x, x, sc, sh, w, perm, cw, cb)


def _attn_kernel(*refs, segs, n_src, has_sink, tq, nsub, lookahead):
    refs = list(refs)
    sink_ref = refs.pop(0) if has_sink else None
    q_ref = refs.pop(0)
    qn_ref = refs.pop(0) if lookahead else None
    kv_refs = [(refs.pop(0), refs.pop(0)) for _ in range(n_src)]
    o_ref = refs.pop(0)
    i = pl.program_id(1)
    nblk = pl.num_programs(1) * nsub
    width = GROUP * tq

    def sink_row(kvi):
        return jnp.concatenate(
            [jnp.full((1, tq), sink_ref[GROUP * kvi + hh] * LOG2_E, F32) for hh in range(GROUP)], axis=1)

    def query_operand(q, kvi):
        qt = q.T
        heads = jnp.concatenate(
            [qt[(GROUP * kvi + hh) * HEAD_DIM:(GROUP * kvi + hh + 1) * HEAD_DIM, :] for hh in range(GROUP)],
            axis=1)
        parts = [jnp.zeros((HEAD_DIM, width), BF16)] * A_KV
        parts[kvi] = heads
        return jnp.concatenate(parts, axis=0)

    def run_tile(sub):
        mine = refs[sub * 4 * A_KV:(sub + 1) * 4 * A_KV]
        s_scr, m_scr, l_scr, acc_scr = [mine[g * A_KV:(g + 1) * A_KV] for g in range(4)]
        pos = i * nsub + sub
        rows = slice(sub * tq, (sub + 1) * tq)

        def seg_mask(where):
            if where not in ("prev", "next"):
                return None
            j = lax.broadcasted_iota(jnp.int32, (BLOCK, width), 0)
            r = lax.broadcasted_iota(jnp.int32, (BLOCK, width), 1) & (tq - 1)
            far = jnp.int32(4 * BLOCK)
            if where == "prev":
                return (j - r) >= jnp.where(pos > 0, 0, far)
            return (r - j) >= jnp.where(pos < nblk - 1, 0, far)

        def block_of(where):
            return {"cur": pos, "prev": jnp.maximum(pos - 1, 0), "next": jnp.minimum(pos + 1, nblk - 1)}[where]

        def for_each_chunk(*fns):
            base = 0
            for src, where, nchunks, kc in segs:
                k_ref, vt_ref = kv_refs[src]
                if where is None:
                    row0, slab0 = 0, 0
                else:
                    slab0 = block_of(where)
                    row0 = pl.multiple_of(slab0 * BLOCK, BLOCK)
                if nchunks <= STATIC_CHUNKS:
                    for ci in range(nchunks):
                        for fn in fns:
                            fn(k_ref, vt_ref, row0 + ci * kc, slab0 + ci, base + ci * kc, kc, where)
                else:
                    def body(ci, carry, k_ref=k_ref, vt_ref=vt_ref, kc=kc, base=base):
                        r0 = pl.multiple_of(ci * kc, kc)
                        for fn in fns:
                            fn(k_ref, vt_ref, r0, ci, base + r0, kc, None)
                        return carry
                    lax.fori_loop(0, nchunks, body, 0, unroll=4)
                base += nchunks * kc

        def start_scores(kvi):
            m_scr[kvi][...] = sink_row(kvi) if has_sink else jnp.full((1, width), NEG_INF, F32)

        def scores(kvi, qz, k_ref, vt_ref, r0, slab, srow, kc, where):
            s = jnp.dot(k_ref[pl.ds(r0, kc), :], qz, preferred_element_type=F32)
            mask = seg_mask(where)
            if mask is not None:
                s = jnp.where(mask, s, NEG_INF)
            s_scr[kvi][pl.ds(srow, kc), :] = s
            m_scr[kvi][...] = jnp.maximum(m_scr[kvi][...], jnp.max(s, axis=0, keepdims=True))

        def start_values(kvi):
            if has_sink:
                l_scr[kvi][...] = jnp.exp2(sink_row(kvi) - m_scr[kvi][...])
            else:
                l_scr[kvi][...] = jnp.zeros((1, width), F32)
            acc_scr[kvi][...] = jnp.zeros((HEAD_DIM, width), F32)

        def values(kvi, k_ref, vt_ref, r0, slab, srow, kc, where):
            p = jnp.exp2(s_scr[kvi][pl.ds(srow, kc), :] - m_scr[kvi][...])
            l_scr[kvi][...] += jnp.sum(p, axis=0, keepdims=True)
            acc_scr[kvi][...] += jnp.dot(vt_ref[slab, kvi * HEAD_DIM:(kvi + 1) * HEAD_DIM, :], p.astype(BF16),
                                         preferred_element_type=F32)

        def first_scores():
            start_scores(0)
            for_each_chunk(functools.partial(scores, 0, query_operand(q_ref[rows, :], 0)))

        if lookahead:
            pl.when(i == 0)(first_scores)
        else:
            first_scores()
        start_scores(1)
        start_values(0)
        for_each_chunk(functools.partial(scores, 1, query_operand(q_ref[rows, :], 1)),
                       functools.partial(values, 0))
        start_values(1)
        if lookahead:
            start_scores(0)
            for_each_chunk(functools.partial(values, 1),
                           functools.partial(scores, 0, query_operand(qn_ref[...], 0)))
        else:
            for_each_chunk(functools.partial(values, 1))

        outs = []
        for kvi in range(A_KV):
            o = acc_scr[kvi][...] / l_scr[kvi][...]
            outs += [o[:, hh * tq:(hh + 1) * tq] for hh in range(GROUP)]
        o_ref[rows, :] = jnp.concatenate(outs, axis=0).T.astype(BF16)

    for sub in range(nsub):
        run_tile(sub)


def _attention(q, kv_arrays, segs, sink, tq, nsub, name, lookahead=False):
    assert not (lookahead and nsub != 1)
    bsz, rows, _ = q.shape
    nb = rows // (tq * nsub)
    has_sink = sink is not None
    in_specs, args = [], []
    if has_sink:
        in_specs.append(pl.BlockSpec(memory_space=pltpu.SMEM))
        args.append(sink)
    in_specs.append(pl.BlockSpec((None, tq * nsub, Q_COLS), lambda b, i: (b, i, 0)))
    args.append(q)
    if lookahead:
        in_specs.append(pl.BlockSpec((None, tq, Q_COLS), lambda b, i: (b, jnp.minimum(i + 1, nb - 1), 0)))
        args.append(q)
    for k, vt in kv_arrays:
        in_specs += [pl.BlockSpec((None,) + k.shape[1:], lambda b, i: (b, 0, 0)),
                     pl.BlockSpec((None,) + vt.shape[1:], lambda b, i: (b, 0, 0, 0))]
        args += [k, vt]
    width = GROUP * tq
    nk_total = sum(nchunks * kc for _, _, nchunks, kc in segs)
    per_head = [(nk_total, width), (1, width), (1, width), (HEAD_DIM, width)]
    return pl.pallas_call(
        functools.partial(_attn_kernel, segs=tuple(segs), n_src=len(kv_arrays), has_sink=has_sink, tq=tq,
                          nsub=nsub, lookahead=lookahead),
        grid=(bsz, nb),
        in_specs=in_specs,
        out_specs=pl.BlockSpec((None, tq * nsub, Q_COLS), lambda b, i: (b, i, 0)),
        out_shape=jax.ShapeDtypeStruct((bsz, rows, Q_COLS), BF16),
        scratch_shapes=[pltpu.VMEM(shape, F32) for _ in range(nsub) for shape in per_head for _ in range(A_KV)],
        compiler_params=_params("arbitrary", "arbitrary"),
        name=name,
    )(*args)


def _whole_seg(src, vt):
    return (src, None, vt.shape[1], vt.shape[3])


def _attn_window(q, k, vt, k_ctx, vt_ctx, sink):
    assert vt.shape[3] == BLOCK
    segs = [(0, "cur", 1, BLOCK), (0, "prev", 1, BLOCK), (0, "next", 1, BLOCK), _whole_seg(1, vt_ctx)]
    return _attention(q, [(k, vt), (k_ctx, vt_ctx)], segs, sink, BLOCK, WINDOW_SUBTILES, "attn_window")


def _attn_global(q, kv_list, sink, tq, name, lookahead=False):
    segs = [_whole_seg(src, vt) for src, (_, vt) in enumerate(kv_list)]
    return _attention(q, kv_list, segs, sink, tq, 1, name, lookahead)


def _rglru_kernel(x_ref, wg_ref, br_ref, bi_ref, lam_ref, h0_ref,
                  h_ref, hfin_ref, carry_scr, decay_scr, *, reverse):
    j = pl.program_id(1)
    nt = pl.num_programs(1)
    rows_t, width = x_ref.shape
    ngroups = rows_t // SUBLANES

    @pl.when(j == 0)
    def _():
        carry_scr[...] = h0_ref[...]

    lam = lam_ref[...]
    neg_softplus = -(jnp.maximum(-lam, 0.0) + jnp.log1p(jnp.exp(-jnp.abs(lam))))
    sub = lax.broadcasted_iota(jnp.int32, (SUBLANES, LANES), 0)
    order = range(ngroups - 1, -1, -1) if reverse else range(ngroups)

    def rows(g):
        return slice(g * SUBLANES, (g + 1) * SUBLANES)

    for c in range(width // LANES):
        cs = slice(c * LANES, (c + 1) * LANES)
        xcc = x_ref[:, cs]
        g = jnp.dot(xcc.astype(BF16), wg_ref[c], preferred_element_type=F32)
        r = jax.nn.sigmoid(g[:, :LANES] + br_ref[:, cs])
        ig = jax.nn.sigmoid(g[:, LANES:] + bi_ref[:, cs])
        log_a = (LRU_C * r) * neg_softplus[:, cs]
        a = jnp.exp(log_a)
        w = -jnp.tanh(log_a) * (a * a + 1.0)
        u = (w * lax.rsqrt(jnp.maximum(w, SQRT_FLOOR))) * (ig * xcc)

        h_loc = decay = None
        for gi in order:
            a_g, u_g = a[rows(gi)], u[rows(gi)]
            h_loc = u_g if h_loc is None else a_g * h_loc + u_g
            decay = a_g if decay is None else a_g * decay
            h_ref[rows(gi), cs] = h_loc
            decay_scr[rows(gi), cs] = decay

        end, dec = h_loc, decay
        for s in (1, 2, 4):
            if reverse:
                ok = sub < SUBLANES - s
                shift = SUBLANES - s
            else:
                ok = sub >= s
                shift = s
            end = end + dec * jnp.where(ok, pltpu.roll(end, shift, 0), 0.0)
            dec = dec * jnp.where(ok, pltpu.roll(dec, shift, 0), 1.0)
        carry = carry_scr[:, cs]
        seg_end = end + dec * carry
        if reverse:
            start = jnp.where(sub == SUBLANES - 1, carry, pltpu.roll(seg_end, SUBLANES - 1, 0))
            carry_scr[:, cs] = seg_end[0:1]
        else:
            start = jnp.where(sub == 0, carry, pltpu.roll(seg_end, 1, 0))
            carry_scr[:, cs] = seg_end[SUBLANES - 1:SUBLANES]
        for gi in order:
            h_ref[rows(gi), cs] = h_ref[rows(gi), cs] + decay_scr[rows(gi), cs] * start

    @pl.when(j == nt - 1)
    def _():
        hfin_ref[...] = carry_scr[...]


def _rglru(xs, h0, wg, br, bi, lam, reverse):
    bsz, rows, width = xs.shape
    nt = rows // SCAN_TILE

    def tpos(j):
        return nt - 1 - j if reverse else j

    vec = _resident((1, width), lambda b, j: (0, 0))
    tile = pl.BlockSpec((None, SCAN_TILE, width), lambda b, j: (b, tpos(j), 0))
    state = pl.BlockSpec((None, 1, width), lambda b, j: (b, 0, 0))
    return pl.pallas_call(
        functools.partial(_rglru_kernel, reverse=reverse),
        grid=(bsz, nt),
        in_specs=[tile, _resident(wg.shape, lambda b, j: (0, 0, 0)), vec, vec, vec, state],
        out_specs=[tile, state],
        out_shape=[jax.ShapeDtypeStruct((bsz, rows, width), F32),
                   jax.ShapeDtypeStruct((bsz, 1, width), F32)],
        scratch_shapes=[pltpu.VMEM((1, width), F32), pltpu.VMEM((SCAN_TILE, width), F32)],
        compiler_params=_params("arbitrary", "arbitrary"),
        name="rglru_bwd" if reverse else "rglru_fwd",
    )(xs, wg, br, bi, lam, h0)


def _merge_kernel(x_ref, sc_ref, sh_ref, gate_ref, ya_ref, yb_ref, hf_ref, hb_ref, perm_ref, perm_t_ref,
                  wy_ref, wg_ref, wa_ref, wb_ref, wc_ref, wo_ref, lng_ref, lnb_ref, o_ref):
    x = x_ref[...]
    d = x.shape[1]
    h = _modulate(x, sc_ref, sh_ref)
    def per_scan_tile(p_ref, rows):
        return jnp.concatenate(
            [jnp.dot(p_ref[...], rows[k:k + SCAN_TILE], preferred_element_type=F32).astype(BF16)
             for k in range(0, rows.shape[0], SCAN_TILE)], axis=0)

    y_rnn = jnp.dot(per_scan_tile(perm_ref, h), wy_ref[...], preferred_element_type=F32)
    yc = per_scan_tile(perm_t_ref, ((hf_ref[...] + hb_ref[...]) * jax.nn.gelu(y_rnn)).astype(BF16))
    zz = jnp.dot(h, wg_ref[...], preferred_element_type=F32)
    mix = (jax.nn.sigmoid(zz[:, 0:d]) * jnp.dot(ya_ref[...], wa_ref[...], preferred_element_type=F32)
           + jax.nn.sigmoid(zz[:, d:2 * d]) * jnp.dot(yb_ref[...], wb_ref[...], preferred_element_type=F32)
           + jax.nn.sigmoid(zz[:, 2 * d:3 * d]) * jnp.dot(yc, wc_ref[...], preferred_element_type=F32))
    o = jnp.dot(mix.astype(BF16), wo_ref[...], preferred_element_type=F32)
    o_ref[...] = _layernorm(ALPHA * x + gate_ref[...] * o, lng_ref[...], lnb_ref[...])


def _merge(x, sc, sh, gate, ya, yb, hf, hb, perm, perm_t, wy, wg, wa, wb, wc, wo, lng, lnb, tm):
    bsz, rows, d = x.shape
    mod_spec = pl.BlockSpec((None, 1, d), lambda b, i: (b, 0, 0))
    vec = _resident((1, d), lambda b, i: (0, 0))

    def tile(cols):
        return pl.BlockSpec((None, tm, cols), lambda b, i: (b, i, 0))

    def weight(w):
        return _resident(w.shape, lambda b, i: (0, 0))

    weights = [perm, perm_t, wy, wg, wa, wb, wc, wo]
    return pl.pallas_call(
        _merge_kernel,
        grid=(bsz, rows // tm),
        in_specs=[tile(d), mod_spec, mod_spec, mod_spec, tile(Q_COLS), tile(Q_COLS), tile(d), tile(d)]
                 + [weight(w) for w in weights] + [vec, vec],
        out_specs=tile(d),
        out_shape=jax.ShapeDtypeStruct((bsz, rows, d), F32),
        compiler_params=_params("arbitrary", "arbitrary"),
        name="merge_ln",
    )(x, sc, sh, gate, ya, yb, hf, hb, *weights, lng, lnb)


def _mlp_kernel(x_ref, sc_ref, sh_ref, gate_ref, w1_ref, w2_ref, lng_ref, lnb_ref, o_ref):
    x = x_ref[...]
    h = _modulate(x, sc_ref, sh_ref)
    u = jnp.dot(h, w1_ref[...], preferred_element_type=F32)
    u = jnp.square(jnp.maximum(u, 0.0)).astype(BF16)
    o = jnp.dot(u, w2_ref[...], preferred_element_type=F32)
    o_ref[...] = _layernorm(ALPHA * x + gate_ref[...] * o, lng_ref[...], lnb_ref[...])


def _mlp(x, sc, sh, gate, w1, w2, lng, lnb, tm):
    bsz, rows, d = x.shape
    mod_spec = pl.BlockSpec((None, 1, d), lambda b, i: (b, 0, 0))
    vec = _resident((1, d), lambda b, i: (0, 0))
    tile = pl.BlockSpec((None, tm, d), lambda b, i: (b, i, 0))
    return pl.pallas_call(
        _mlp_kernel,
        grid=(bsz, rows // tm),
        in_specs=[tile, mod_spec, mod_spec, mod_spec,
                  _resident(w1.shape, lambda b, i: (0, 0)), _resident(w2.shape, lambda b, i: (0, 0)), vec, vec],
        out_specs=tile,
        out_shape=jax.ShapeDtypeStruct((bsz, rows, d), F32),
        compiler_params=_params("arbitrary", "arbitrary"),
        name="mlp_ln",
    )(x, sc, sh, gate, w1, w2, lng, lnb)


def _rope_tables(n):
    pos = jnp.arange(n)
    row = (pos // GRID_W).astype(F32)
    colp = (pos % GRID_W).astype(F32)
    nf = HEAD_DIM // 4
    inv = ROPE_BASE ** (-jnp.arange(nf, dtype=F32) / nf)
    ang_r = row[:, None] * inv
    ang_c = colp[:, None] * inv
    cos_t = jnp.concatenate([jnp.cos(ang_r), jnp.cos(ang_r), jnp.cos(ang_c), jnp.cos(ang_c)], axis=-1)
    sin_t = jnp.concatenate([-jnp.sin(ang_r), jnp.sin(ang_r), -jnp.sin(ang_c), jnp.sin(ang_c)], axis=-1)
    return jnp.tile(cos_t, (1, 2)), jnp.tile(sin_t, (1, 2))


def _gate_weights(w_r, w_i):
    def pair(w):
        w = w.reshape(LRU_BLOCKS // 2, 2, LRU_BLOCK_DIM, LRU_BLOCK_DIM)
        z = jnp.zeros_like(w[:, 0])
        top = jnp.concatenate([w[:, 0], z], axis=-1)
        bot = jnp.concatenate([z, w[:, 1]], axis=-1)
        return jnp.concatenate([top, bot], axis=-2)
    return jnp.concatenate([pair(w_r), pair(w_i)], axis=-1).astype(BF16)


def kernel(x, c, ctx, c_ctx, w_ada, b_ada, w_in, a_sink, b_q_gain, b_k_gain, c_conv_w, c_conv_b,
           c_wr, c_br, c_wi, c_bi, c_lam, w_br_a, w_br_b, w_br_c, w_out, ln1_g, ln1_b,
           w_ff1, w_ff2, ln2_g, ln2_b):
    bsz, n, d = x.shape
    m = ctx.shape[1]
    depth = w_ada.shape[0]

    cc = jnp.zeros((SUBLANES, d), F32).at[:bsz].set(c).at[bsz].set(c_ctx)
    mod = _ada(cc, w_ada, b_ada)
    rope_tabs = _rope_tables(n)
    blk = jnp.arange(LANES) // HEAD_DIM
    ones = (blk[:, None] == blk[None, :]).astype(BF16)
    h_zero = jnp.zeros((bsz, 1, d), F32)
    slot = jnp.arange(SCAN_TILE)
    token_of_slot = SEG_LEN * (slot % SUBLANES) + slot // SUBLANES
    perm = (token_of_slot[:, None] == slot[None, :]).astype(BF16)
    perm_t = perm.T

    for l in range(depth):
        with_ctx = l < depth - 1
        mod_lat = mod[l, :bsz].reshape(bsz, 1, 6 * d)
        mod_ctx = jnp.broadcast_to(mod[l, bsz].reshape(1, 1, 6 * d), (bsz, 1, 6 * d))
        sh1, sc1, g1, sh2, sc2, g2 = [mod_lat[:, :, k * d:(k + 1) * d] for k in range(6)]
        csh1, csc1, cg1, csh2, csc2, cg2 = [mod_ctx[:, :, k * d:(k + 1) * d] for k in range(6)]

        w_attn = w_in[l, :, :ATTN_COLS].astype(BF16)
        w_rnn = w_in[l, :, ATTN_COLS:ATTN_COLS + LRU_WIDTH].astype(BF16)
        w_y = w_in[l, :, ATTN_COLS + LRU_WIDTH:ATTN_COLS + 2 * LRU_WIDTH].astype(BF16)
        w_g = w_in[l, :, ATTN_COLS + 2 * LRU_WIDTH:].astype(BF16)
        gq = jnp.tile(b_q_gain[l], 2).reshape(1, LANES)
        gk = jnp.tile(b_k_gain[l], 2).reshape(1, LANES)

        qa, ka, va, qb, kb, vb = _inproj_attn(x, sc1, sh1, w_attn, gq, gk, ones, rope_tabs, 512)
        qa_c, ka_c, va_c, qb_c, kb_c, vb_c = _inproj_attn(ctx, csc1, csh1, w_attn, gq, gk, ones, None, m)

        ya = _attn_window(qa, ka, va, ka_c, va_c, a_sink[l])
        yb = _attn_global(qb, [(kb, vb), (kb_c, vb_c)], None, BLOCK, "attn_global", lookahead=True)

        cb = c_conv_b[l].reshape(1, d)
        xs = _inproj_rnn(x, sc1, sh1, w_rnn, perm, c_conv_w[l], cb)
        xs_c = _inproj_rnn(ctx, csc1, csh1, w_rnn, perm, c_conv_w[l], cb)
        hs, hs_c = [], []
        for di, reverse in enumerate((False, True)):
            wg = _gate_weights(c_wr[l, di], c_wi[l, di])
            vecs = [v[l, di].reshape(1, d) for v in (c_br, c_bi, c_lam)]
            h_c, h_fin = _rglru(xs_c, h_zero, wg, *vecs, reverse)
            h_l, _ = _rglru(xs, h_fin, wg, *vecs, reverse)
            hs.append(h_l)
            hs_c.append(h_c)

        merge_w = [perm, perm_t, w_y, w_g] + [w[l].astype(BF16) for w in (w_br_a, w_br_b, w_br_c, w_out)]
        ln1 = [ln1_g[l].reshape(1, d), ln1_b[l].reshape(1, d)]
        ln2 = [ln2_g[l].reshape(1, d), ln2_b[l].reshape(1, d)]
        w1 = w_ff1[l].astype(BF16)
        w2 = w_ff2[l].astype(BF16)

        x = _merge(x, sc1, sh1, g1, ya, yb, hs[0], hs[1], *merge_w, *ln1, 2 * SCAN_TILE)
        x = _mlp(x, sc2, sh2, g2, w1, w2, *ln2, 512)
        if with_ctx:
            ya_c = _attn_global(qa_c, [(ka_c, va_c)], a_sink[l], BLOCK, "attn_ctx_a")
            yb_c = _attn_global(qb_c, [(kb_c, vb_c)], None, BLOCK, "attn_ctx_b")
            ctx = _merge(ctx, csc1, csh1, cg1, ya_c, yb_c, hs_c[0], hs_c[1], *merge_w, *ln1, m)
            ctx = _mlp(ctx, csc2, csh2, cg2, w1, w2, *ln2, m)
    return x
```

```python
import functools

import jax
import jax.numpy as jnp
from jax import lax
from jax.experimental import pallas as pl
from jax.experimental.pallas import tpu as pltpu

D_MODEL = 1024
DEPTH = 2
GRID_W = 64
HEAD_DIM = 64
ROPE_BASE = 10000.0
BLOCK = 128
A_HEADS = 8
A_KV = 2
B_HEADS = 8
B_KV = 2
LRU_WIDTH = D_MODEL
LRU_BLOCKS = 16
LRU_BLOCK_DIM = LRU_WIDTH // LRU_BLOCKS
CONV_W = 4
LRU_C = 8.0
D_FF = 4 * D_MODEL
ALPHA = (2 * DEPTH) ** 0.25
LN_EPS = 1e-5
RMS_EPS = 1e-6
NEG_INF = -1e30
LOG2_E = 1.4426950408889634
Q_COLS = A_HEADS * HEAD_DIM
KV_COLS = A_KV * HEAD_DIM
ATTN_COLS = 2 * (Q_COLS + 2 * KV_COLS)
GROUP = A_HEADS // A_KV

LANES = 128
SUBLANES = 8
V7X_VMEM_BYTES = 64 * 1024 * 1024
VMEM_LIMIT_BYTES = V7X_VMEM_BYTES - 8 * 1024 * 1024

SCAN_TILE = 256
SEG_LEN = SCAN_TILE // SUBLANES
STATIC_CHUNKS = 4
WINDOW_SUBTILES = 8
SQRT_FLOOR = 1e-30

F32 = jnp.float32
BF16 = jnp.bfloat16


def _params(*semantics):
    return pltpu.CompilerParams(dimension_semantics=semantics, vmem_limit_bytes=VMEM_LIMIT_BYTES)


def _resident(block_shape, index_map):
    return pl.BlockSpec(block_shape, index_map, pipeline_mode=pl.Buffered(1))


def _modulate(x, sc_ref, sh_ref):
    return (x * (1.0 + sc_ref[...]) + sh_ref[...]).astype(BF16)


def _layernorm(y, g, b):
    mu = jnp.mean(y, axis=-1, keepdims=True)
    yc = y - mu
    var = jnp.mean(yc * yc, axis=-1, keepdims=True)
    return yc * lax.rsqrt(var + LN_EPS) * g + b


def _ada_kernel(c_ref, w_ref, b_ref, o_ref):
    c = c_ref[...]
    s = c * jax.nn.sigmoid(c)
    o_ref[...] = jnp.dot(s, w_ref[...], preferred_element_type=F32,
                         precision=lax.Precision.HIGHEST) + b_ref[...]


def _ada(cc, w_ada, b_ada):
    depth, d, cols = w_ada.shape
    tn = cols // 4
    return pl.pallas_call(
        _ada_kernel,
        grid=(depth, cols // tn),
        in_specs=[pl.BlockSpec((SUBLANES, d), lambda l, j: (0, 0)),
                  pl.BlockSpec((None, d, tn), lambda l, j: (l, 0, j)),
                  pl.BlockSpec((None, 1, tn), lambda l, j: (l, 0, j))],
        out_specs=pl.BlockSpec((None, SUBLANES, tn), lambda l, j: (l, 0, j)),
        out_shape=jax.ShapeDtypeStruct((depth, SUBLANES, cols), F32),
        compiler_params=_params("arbitrary", "arbitrary"),
        name="ada_mod",
    )(cc, w_ada, b_ada.reshape(depth, 1, cols))


def _inproj_attn_kernel(*refs, rope):
    if rope:
        (x_ref, sc_ref, sh_ref, w_ref, gq_ref, gk_ref, ones_ref, cos_ref, sin_ref,
         qa_ref, ka_ref, va_ref, qb_ref, kb_ref, vb_ref) = refs
    else:
        (x_ref, sc_ref, sh_ref, w_ref, gq_ref, gk_ref, ones_ref,
         qa_ref, ka_ref, va_ref, qb_ref, kb_ref, vb_ref) = refs
    h = _modulate(x_ref[...], sc_ref, sh_ref)
    z = jnp.dot(h, w_ref[...], preferred_element_type=F32)
    tm = z.shape[0]
    lane = lax.broadcasted_iota(jnp.int32, (tm, LANES), 1)
    first16 = (lane & 31) < 16
    ones = ones_ref[...]
    scale = HEAD_DIM ** -0.5 * LOG2_E

    def rot(zc):
        if not rope:
            return zc
        partner = jnp.where(first16, pltpu.roll(zc, LANES - 16, 1), pltpu.roll(zc, 16, 1))
        return zc * cos_ref[...] + partner * sin_ref[...]

    def rms(zc, g_ref):
        sq = zc * zc
        hi = sq.astype(BF16)
        lo = (sq - hi.astype(F32)).astype(BF16)
        ss = (jnp.dot(hi, ones, preferred_element_type=F32)
              + jnp.dot(lo, ones, preferred_element_type=F32))
        return zc * lax.rsqrt(ss * (1.0 / HEAD_DIM) + RMS_EPS) * g_ref[...]

    def col(start):
        return z[:, start:start + LANES]

    for c in range(Q_COLS // LANES):
        qa_ref[:, c * LANES:(c + 1) * LANES] = (rot(col(c * LANES)) * scale).astype(BF16)
    base = Q_COLS
    ka_ref[...] = rot(col(base)).astype(BF16)
    va_t = col(base + KV_COLS).T.astype(BF16)
    for c in range(va_ref.shape[0]):
        va_ref[c] = va_t[:, c * BLOCK:(c + 1) * BLOCK]
    base = Q_COLS + 2 * KV_COLS
    for c in range(Q_COLS // LANES):
        qb_ref[:, c * LANES:(c + 1) * LANES] = (
            rot(rms(col(base + c * LANES), gq_ref)) * scale).astype(BF16)
    base += Q_COLS
    kb_ref[...] = rot(rms(col(base), gk_ref)).astype(BF16)
    vb_ref[0] = col(base + KV_COLS).T.astype(BF16)


def _inproj_attn(x, sc, sh, w, gq, gk, ones, rope_tabs, tm):
    bsz, rows, d = x.shape
    rope = rope_tabs is not None
    mod_spec = pl.BlockSpec((None, 1, d), lambda b, i: (b, 0, 0))
    vec_spec = _resident((1, LANES), lambda b, i: (0, 0))
    in_specs = [pl.BlockSpec((None, tm, d), lambda b, i: (b, i, 0)), mod_spec, mod_spec,
                _resident((d, ATTN_COLS), lambda b, i: (0, 0)), vec_spec, vec_spec,
                _resident((LANES, LANES), lambda b, i: (0, 0))]
    args = [x, sc, sh, w, gq, gk, ones]
    if rope:
        tab_spec = pl.BlockSpec((tm, LANES), lambda b, i: (i, 0))
        in_specs += [tab_spec, tab_spec]
        args += list(rope_tabs)

    def out(cols):
        return (pl.BlockSpec((None, tm, cols), lambda b, i: (b, i, 0)),
                jax.ShapeDtypeStruct((bsz, rows, cols), BF16))

    def out_t(slab):
        return (pl.BlockSpec((None, tm // slab, KV_COLS, slab), lambda b, i: (b, i, 0, 0)),
                jax.ShapeDtypeStruct((bsz, rows // slab, KV_COLS, slab), BF16))

    outs = [out(Q_COLS), out(KV_COLS), out_t(BLOCK), out(Q_COLS), out(KV_COLS), out_t(tm)]
    return pl.pallas_call(
        functools.partial(_inproj_attn_kernel, rope=rope),
        grid=(bsz, rows // tm),
        in_specs=in_specs,
        out_specs=[o[0] for o in outs],
        out_shape=[o[1] for o in outs],
        compiler_params=_params("arbitrary", "arbitrary"),
        name="inproj_attn_rope" if rope else "inproj_attn_ctx",
    )(*args)


def _inproj_rnn_kernel(xp_ref, xc_ref, xn_ref, sc_ref, sh_ref, w_ref, perm_ref, cw_ref, cb_ref, o_ref):
    i = pl.program_id(1)
    nt = pl.num_programs(1)
    rows_t = xc_ref.shape[0]
    h = _modulate(xc_ref[...], sc_ref, sh_ref)
    hp = jnp.dot(perm_ref[...], h, preferred_element_type=F32).astype(BF16)
    halo = _modulate(jnp.concatenate([xp_ref[...], xn_ref[...]], axis=0), sc_ref, sh_ref)
    lhs = jnp.concatenate([hp, halo], axis=0)
    zall = jnp.dot(lhs, w_ref[...], preferred_element_type=F32)
    z = zall[:rows_t]
    zp = jnp.where(i > 0, zall[rows_t:rows_t + SUBLANES], 0.0)
    zn = jnp.where(i < nt - 1, zall[rows_t + SUBLANES:], 0.0)
    sub = lax.broadcasted_iota(jnp.int32, (SUBLANES, z.shape[1]), 0)
    first, second, last = z[0:SUBLANES], z[SUBLANES:2 * SUBLANES], z[rows_t - SUBLANES:]
    before = jnp.where(sub == 0, zp[SUBLANES - 1:SUBLANES], pltpu.roll(last, 1, 0))
    after1 = jnp.where(sub == SUBLANES - 1, zn[0:1], pltpu.roll(first, SUBLANES - 1, 0))
    after2 = jnp.where(sub == SUBLANES - 1, zn[1:2], pltpu.roll(second, SUBLANES - 1, 0))
    taps = [jnp.concatenate([before, z[:rows_t - SUBLANES]], axis=0),
            z,
            jnp.concatenate([z[SUBLANES:], after1], axis=0),
            jnp.concatenate([z[2 * SUBLANES:], after1, after2], axis=0)]
    out = cb_ref[...]
    for tap in range(CONV_W):
        out = out + cw_ref[tap:tap + 1, :] * taps[tap]
    o_ref[...] = out


def _inproj_rnn(x, sc, sh, w, perm, cw, cb):
    bsz, rows, d = x.shape
    cols = w.shape[1]
    nt = rows // SCAN_TILE
    hb = SCAN_TILE // SUBLANES
    mod_spec = pl.BlockSpec((None, 1, d), lambda b, i: (b, 0, 0))
    vec = _resident((1, cols), lambda b, i: (0, 0))
    return pl.pallas_call(
        _inproj_rnn_kernel,
        grid=(bsz, nt),
        in_specs=[pl.BlockSpec((None, SUBLANES, d), lambda b, i: (b, jnp.maximum(i * hb - 1, 0), 0)),
                  pl.BlockSpec((None, SCAN_TILE, d), lambda b, i: (b, i, 0)),
                  pl.BlockSpec((None, SUBLANES, d),
                               lambda b, i: (b, jnp.minimum((i + 1) * hb, rows // SUBLANES - 1), 0)),
                  mod_spec, mod_spec, _resident((d, cols), lambda b, i: (0, 0)),
                  _resident((SCAN_TILE, SCAN_TILE), lambda b, i: (0, 0)),
                  _resident((CONV_W, cols), lambda b, i: (0, 0)), vec],
        out_specs=pl.BlockSpec((None, SCAN_TILE, cols), lambda b, i: (b, i, 0)),
        out_shape=jax.ShapeDtypeStruct((bsz, rows, cols), F32),
        compiler_params=_params("arbitrary", "arbitrary"),
        name="inproj_rnn_conv",
    )(x, x, x, sc, sh, w, perm, cw, cb)


def _attn_kernel(*refs, segs, n_src, has_sink, tq, nsub, lookahead):
    refs = list(refs)
    sink_ref = refs.pop(0) if has_sink else None
    q_ref = refs.pop(0)
    qn_ref = refs.pop(0) if lookahead else None
    kv_refs = [(refs.pop(0), refs.pop(0)) for _ in range(n_src)]
    o_ref = refs.pop(0)
    i = pl.program_id(1)
    nblk = pl.num_programs(1) * nsub
    width = GROUP * tq

    def sink_row(kvi):
        return jnp.concatenate(
            [jnp.full((1, tq), sink_ref[GROUP * kvi + hh] * LOG2_E, F32) for hh in range(GROUP)], axis=1)

    def query_operand(q, kvi):
        qt = q.T
        heads = jnp.concatenate(
            [qt[(GROUP * kvi + hh) * HEAD_DIM:(GROUP * kvi + hh + 1) * HEAD_DIM, :] for hh in range(GROUP)],
            axis=1)
        parts = [jnp.zeros((HEAD_DIM, width), BF16)] * A_KV
        parts[kvi] = heads
        return jnp.concatenate(parts, axis=0)

    def run_tile(sub):
        mine = refs[sub * 4 * A_KV:(sub + 1) * 4 * A_KV]
        s_scr, m_scr, l_scr, acc_scr = [mine[g * A_KV:(g + 1) * A_KV] for g in range(4)]
        pos = i * nsub + sub
        rows = slice(sub * tq, (sub + 1) * tq)

        def seg_mask(where):
            if where not in ("prev", "next"):
                return None
            j = lax.broadcasted_iota(jnp.int32, (BLOCK, width), 0)
            r = lax.broadcasted_iota(jnp.int32, (BLOCK, width), 1) & (tq - 1)
            far = jnp.int32(4 * BLOCK)
            if where == "prev":
                return (j - r) >= jnp.where(pos > 0, 0, far)
            return (r - j) >= jnp.where(pos < nblk - 1, 0, far)

        def block_of(where):
            return {"cur": pos, "prev": jnp.maximum(pos - 1, 0), "next": jnp.minimum(pos + 1, nblk - 1)}[where]

        def for_each_chunk(*fns):
            base = 0
            for src, where, nchunks, kc in segs:
                k_ref, vt_ref = kv_refs[src]
                if where is None:
                    row0, slab0 = 0, 0
                else:
                    slab0 = block_of(where)
                    row0 = pl.multiple_of(slab0 * BLOCK, BLOCK)
                if nchunks <= STATIC_CHUNKS:
                    for ci in range(nchunks):
                        for fn in fns:
                            fn(k_ref, vt_ref, row0 + ci * kc, slab0 + ci, base + ci * kc, kc, where)
                else:
                    def body(ci, carry, k_ref=k_ref, vt_ref=vt_ref, kc=kc, base=base):
                        r0 = pl.multiple_of(ci * kc, kc)
                        for fn in fns:
                            fn(k_ref, vt_ref, r0, ci, base + r0, kc, None)
                        return carry
                    lax.fori_loop(0, nchunks, body, 0, unroll=4)
                base += nchunks * kc

        def start_scores(kvi):
            m_scr[kvi][...] = sink_row(kvi) if has_sink else jnp.full((1, width), NEG_INF, F32)

        def scores(kvi, qz, k_ref, vt_ref, r0, slab, srow, kc, where):
            s = jnp.dot(k_ref[pl.ds(r0, kc), :], qz, preferred_element_type=F32)
            mask = seg_mask(where)
            if mask is not None:
                s = jnp.where(mask, s, NEG_INF)
            s_scr[kvi][pl.ds(srow, kc), :] = s
            m_scr[kvi][...] = jnp.maximum(m_scr[kvi][...], jnp.max(s, axis=0, keepdims=True))

        def start_values(kvi):
            if has_sink:
                l_scr[kvi][...] = jnp.exp2(sink_row(kvi) - m_scr[kvi][...])
            else:
                l_scr[kvi][...] = jnp.zeros((1, width), F32)
            acc_scr[kvi][...] = jnp.zeros((HEAD_DIM, width), F32)

        def values(kvi, k_ref, vt_ref, r0, slab, srow, kc, where):
            p = jnp.exp2(s_scr[kvi][pl.ds(srow, kc), :] - m_scr[kvi][...])
            l_scr[kvi][...] += jnp.sum(p, axis=0, keepdims=True)
            acc_scr[kvi][...] += jnp.dot(vt_ref[slab, kvi * HEAD_DIM:(kvi + 1) * HEAD_DIM, :], p.astype(BF16),
                                         preferred_element_type=F32)

        def first_scores():
            start_scores(0)
            for_each_chunk(functools.partial(scores, 0, query_operand(q_ref[rows, :], 0)))

        def phase_a():
            if lookahead:
                pl.when(i == 0)(first_scores)
            else:
                first_scores()

        def phase_b():
            start_scores(1)
            start_values(0)
            for_each_chunk(functools.partial(scores, 1, query_operand(q_ref[rows, :], 1)),
                           functools.partial(values, 0))

        def phase_c():
            start_values(1)
            if lookahead:
                start_scores(0)
                for_each_chunk(functools.partial(values, 1),
                               functools.partial(scores, 0, query_operand(qn_ref[...], 0)))
            else:
                for_each_chunk(functools.partial(values, 1))

        def finish():
            outs = []
            for kvi in range(A_KV):
                o = acc_scr[kvi][...] / l_scr[kvi][...]
                outs += [o[:, hh * tq:(hh + 1) * tq] for hh in range(GROUP)]
            o_ref[rows, :] = jnp.concatenate(outs, axis=0).T.astype(BF16)

        return phase_a, phase_b, phase_c, finish

    for phases in zip(*[run_tile(sub) for sub in range(nsub)]):
        for phase in phases:
            phase()


def _attention(q, kv_arrays, segs, sink, tq, nsub, name, lookahead=False):
    assert not (lookahead and nsub != 1)
    bsz, rows, _ = q.shape
    nb = rows // (tq * nsub)
    has_sink = sink is not None
    in_specs, args = [], []
    if has_sink:
        in_specs.append(pl.BlockSpec(memory_space=pltpu.SMEM))
        args.append(sink)
    in_specs.append(pl.BlockSpec((None, tq * nsub, Q_COLS), lambda b, i: (b, i, 0)))
    args.append(q)
    if lookahead:
        in_specs.append(pl.BlockSpec((None, tq, Q_COLS), lambda b, i: (b, jnp.minimum(i + 1, nb - 1), 0)))
        args.append(q)
    for k, vt in kv_arrays:
        in_specs += [pl.BlockSpec((None,) + k.shape[1:], lambda b, i: (b, 0, 0)),
                     pl.BlockSpec((None,) + vt.shape[1:], lambda b, i: (b, 0, 0, 0))]
        args += [k, vt]
    width = GROUP * tq
    nk_total = sum(nchunks * kc for _, _, nchunks, kc in segs)
    per_head = [(nk_total, width), (1, width), (1, width), (HEAD_DIM, width)]
    return pl.pallas_call(
        functools.partial(_attn_kernel, segs=tuple(segs), n_src=len(kv_arrays), has_sink=has_sink, tq=tq,
                          nsub=nsub, lookahead=lookahead),
        grid=(bsz, nb),
        in_specs=in_specs,
        out_specs=pl.BlockSpec((None, tq * nsub, Q_COLS), lambda b, i: (b, i, 0)),
        out_shape=jax.ShapeDtypeStruct((bsz, rows, Q_COLS), BF16),
        scratch_shapes=[pltpu.VMEM(shape, F32) for _ in range(nsub) for shape in per_head for _ in range(A_KV)],
        compiler_params=_params("arbitrary", "arbitrary"),
        name=name,
    )(*args)


def _whole_seg(src, vt):
    return (src, None, vt.shape[1], vt.shape[3])


def _attn_window(q, k, vt, k_ctx, vt_ctx, sink):
    assert vt.shape[3] == BLOCK
    segs = [(0, "cur", 1, BLOCK), (0, "prev", 1, BLOCK), (0, "next", 1, BLOCK), _whole_seg(1, vt_ctx)]
    return _attention(q, [(k, vt), (k_ctx, vt_ctx)], segs, sink, BLOCK, WINDOW_SUBTILES, "attn_window")


def _attn_global(q, kv_list, sink, tq, name, lookahead=False):
    segs = [_whole_seg(src, vt) for src, (_, vt) in enumerate(kv_list)]
    return _attention(q, kv_list, segs, sink, tq, 1, name, lookahead)


def _rglru_kernel(x_ref, wg_ref, br_ref, bi_ref, lam_ref, h0_ref,
                  h_ref, hfin_ref, carry_scr, decay_scr, *, reverse):
    j = pl.program_id(1)
    nt = pl.num_programs(1)
    rows_t, width = x_ref.shape
    ngroups = rows_t // SUBLANES

    @pl.when(j == 0)
    def _():
        carry_scr[...] = h0_ref[...]

    lam = lam_ref[...]
    neg_softplus = -(jnp.maximum(-lam, 0.0) + jnp.log1p(jnp.exp(-jnp.abs(lam))))
    sub = lax.broadcasted_iota(jnp.int32, (SUBLANES, LANES), 0)
    order = range(ngroups - 1, -1, -1) if reverse else range(ngroups)

    def rows(g):
        return slice(g * SUBLANES, (g + 1) * SUBLANES)

    for c in range(width // LANES):
        cs = slice(c * LANES, (c + 1) * LANES)
        xcc = x_ref[:, cs]
        g = jnp.dot(xcc.astype(BF16), wg_ref[c], preferred_element_type=F32)
        r = jax.nn.sigmoid(g[:, :LANES] + br_ref[:, cs])
        ig = jax.nn.sigmoid(g[:, LANES:] + bi_ref[:, cs])
        log_a = (LRU_C * r) * neg_softplus[:, cs]
        a = jnp.exp(log_a)
        w = -jnp.tanh(log_a) * (a * a + 1.0)
        u = (w * lax.rsqrt(jnp.maximum(w, SQRT_FLOOR))) * (ig * xcc)

        h_loc = decay = None
        for gi in order:
            a_g, u_g = a[rows(gi)], u[rows(gi)]
            h_loc = u_g if h_loc is None else a_g * h_loc + u_g
            decay = a_g if decay is None else a_g * decay
            h_ref[rows(gi), cs] = h_loc
            decay_scr[rows(gi), cs] = decay

        end, dec = h_loc, decay
        for s in (1, 2, 4):
            if reverse:
                ok = sub < SUBLANES - s
                shift = SUBLANES - s
            else:
                ok = sub >= s
                shift = s
            end = end + dec * jnp.where(ok, pltpu.roll(end, shift, 0), 0.0)
            dec = dec * jnp.where(ok, pltpu.roll(dec, shift, 0), 1.0)
        carry = carry_scr[:, cs]
        seg_end = end + dec * carry
        if reverse:
            start = jnp.where(sub == SUBLANES - 1, carry, pltpu.roll(seg_end, SUBLANES - 1, 0))
            carry_scr[:, cs] = seg_end[0:1]
        else:
            start = jnp.where(sub == 0, carry, pltpu.roll(seg_end, 1, 0))
            carry_scr[:, cs] = seg_end[SUBLANES - 1:SUBLANES]
        for gi in order:
            h_ref[rows(gi), cs] = h_ref[rows(gi), cs] + decay_scr[rows(gi), cs] * start

    @pl.when(j == nt - 1)
    def _():
        hfin_ref[...] = carry_scr[...]


def _rglru(xs, h0, wg, br, bi, lam, reverse):
    bsz, rows, width = xs.shape
    nt = rows // SCAN_TILE

    def tpos(j):
        return nt - 1 - j if reverse else j

    vec = _resident((1, width), lambda b, j: (0, 0))
    tile = pl.BlockSpec((None, SCAN_TILE, width), lambda b, j: (b, tpos(j), 0))
    state = pl.BlockSpec((None, 1, width), lambda b, j: (b, 0, 0))
    return pl.pallas_call(
        functools.partial(_rglru_kernel, reverse=reverse),
        grid=(bsz, nt),
        in_specs=[tile, _resident(wg.shape, lambda b, j: (0, 0, 0)), vec, vec, vec, state],
        out_specs=[tile, state],
        out_shape=[jax.ShapeDtypeStruct((bsz, rows, width), F32),
                   jax.ShapeDtypeStruct((bsz, 1, width), F32)],
        scratch_shapes=[pltpu.VMEM((1, width), F32), pltpu.VMEM((SCAN_TILE, width), F32)],
        compiler_params=_params("arbitrary", "arbitrary"),
        name="rglru_bwd" if reverse else "rglru_fwd",
    )(xs, wg, br, bi, lam, h0)


def _merge_kernel(x_ref, sc_ref, sh_ref, gate_ref, ya_ref, yb_ref, hf_ref, hb_ref, perm_ref, perm_t_ref,
                  wy_ref, wg_ref, wa_ref, wb_ref, wc_ref, wo_ref, lng_ref, lnb_ref, o_ref):
    x = x_ref[...]
    d = x.shape[1]
    h = _modulate(x, sc_ref, sh_ref)
    def per_scan_tile(p_ref, rows):
        return jnp.concatenate(
            [jnp.dot(p_ref[...], rows[k:k + SCAN_TILE], preferred_element_type=F32).astype(BF16)
             for k in range(0, rows.shape[0], SCAN_TILE)], axis=0)

    y_rnn = jnp.dot(per_scan_tile(perm_ref, h), wy_ref[...], preferred_element_type=F32)
    yc = per_scan_tile(perm_t_ref, ((hf_ref[...] + hb_ref[...]) * jax.nn.gelu(y_rnn)).astype(BF16))
    zz = jnp.dot(h, wg_ref[...], preferred_element_type=F32)
    mix = (jax.nn.sigmoid(zz[:, 0:d]) * jnp.dot(ya_ref[...], wa_ref[...], preferred_element_type=F32)
           + jax.nn.sigmoid(zz[:, d:2 * d]) * jnp.dot(yb_ref[...], wb_ref[...], preferred_element_type=F32)
           + jax.nn.sigmoid(zz[:, 2 * d:3 * d]) * jnp.dot(yc, wc_ref[...], preferred_element_type=F32))
    o = jnp.dot(mix.astype(BF16), wo_ref[...], preferred_element_type=F32)
    o_ref[...] = _layernorm(ALPHA * x + gate_ref[...] * o, lng_ref[...], lnb_ref[...])


def _merge(x, sc, sh, gate, ya, yb, hf, hb, perm, perm_t, wy, wg, wa, wb, wc, wo, lng, lnb, tm):
    bsz, rows, d = x.shape
    mod_spec = pl.BlockSpec((None, 1, d), lambda b, i: (b, 0, 0))
    vec = _resident((1, d), lambda b, i: (0, 0))

    def tile(cols):
        return pl.BlockSpec((None, tm, cols), lambda b, i: (b, i, 0))

    def weight(w):
        return _resident(w.shape, lambda b, i: (0, 0))

    weights = [perm, perm_t, wy, wg, wa, wb, wc, wo]
    return pl.pallas_call(
        _merge_kernel,
        grid=(bsz, rows // tm),
        in_specs=[tile(d), mod_spec, mod_spec, mod_spec, tile(Q_COLS), tile(Q_COLS), tile(d), tile(d)]
                 + [weight(w) for w in weights] + [vec, vec],
        out_specs=tile(d),
        out_shape=jax.ShapeDtypeStruct((bsz, rows, d), F32),
        compiler_params=_params("arbitrary", "arbitrary"),
        name="merge_ln",
    )(x, sc, sh, gate, ya, yb, hf, hb, *weights, lng, lnb)


def _mlp_kernel(x_ref, sc_ref, sh_ref, gate_ref, w1_ref, w2_ref, lng_ref, lnb_ref, o_ref):
    x = x_ref[...]
    h = _modulate(x, sc_ref, sh_ref)
    u = jnp.dot(h, w1_ref[...], preferred_element_type=F32)
    u = jnp.square(jnp.maximum(u, 0.0)).astype(BF16)
    o = jnp.dot(u, w2_ref[...], preferred_element_type=F32)
    o_ref[...] = _layernorm(ALPHA * x + gate_ref[...] * o, lng_ref[...], lnb_ref[...])


def _mlp(x, sc, sh, gate, w1, w2, lng, lnb, tm):
    bsz, rows, d = x.shape
    mod_spec = pl.BlockSpec((None, 1, d), lambda b, i: (b, 0, 0))
    vec = _resident((1, d), lambda b, i: (0, 0))
    tile = pl.BlockSpec((None, tm, d), lambda b, i: (b, i, 0))
    return pl.pallas_call(
        _mlp_kernel,
        grid=(bsz, rows // tm),
        in_specs=[tile, mod_spec, mod_spec, mod_spec,
                  _resident(w1.shape, lambda b, i: (0, 0)), _resident(w2.shape, lambda b, i: (0, 0)), vec, vec],
        out_specs=tile,
        out_shape=jax.ShapeDtypeStruct((bsz, rows, d), F32),
        compiler_params=_params("arbitrary", "arbitrary"),
        name="mlp_ln",
    )(x, sc, sh, gate, w1, w2, lng, lnb)


def _rope_tables(n):
    pos = jnp.arange(n)
    row = (pos // GRID_W).astype(F32)
    colp = (pos % GRID_W).astype(F32)
    nf = HEAD_DIM // 4
    inv = ROPE_BASE ** (-jnp.arange(nf, dtype=F32) / nf)
    ang_r = row[:, None] * inv
    ang_c = colp[:, None] * inv
    cos_t = jnp.concatenate([jnp.cos(ang_r), jnp.cos(ang_r), jnp.cos(ang_c), jnp.cos(ang_c)], axis=-1)
    sin_t = jnp.concatenate([-jnp.sin(ang_r), jnp.sin(ang_r), -jnp.sin(ang_c), jnp.sin(ang_c)], axis=-1)
    return jnp.tile(cos_t, (1, 2)), jnp.tile(sin_t, (1, 2))


def _gate_weights(w_r, w_i):
    def pair(w):
        w = w.reshape(LRU_BLOCKS // 2, 2, LRU_BLOCK_DIM, LRU_BLOCK_DIM)
        z = jnp.zeros_like(w[:, 0])
        top = jnp.concatenate([w[:, 0], z], axis=-1)
        bot = jnp.concatenate([z, w[:, 1]], axis=-1)
        return jnp.concatenate([top, bot], axis=-2)
    return jnp.concatenate([pair(w_r), pair(w_i)], axis=-1).astype(BF16)


def kernel(x, c, ctx, c_ctx, w_ada, b_ada, w_in, a_sink, b_q_gain, b_k_gain, c_conv_w, c_conv_b,
           c_wr, c_br, c_wi, c_bi, c_lam, w_br_a, w_br_b, w_br_c, w_out, ln1_g, ln1_b,
           w_ff1, w_ff2, ln2_g, ln2_b):
    bsz, n, d = x.shape
    m = ctx.shape[1]
    depth = w_ada.shape[0]

    cc = jnp.zeros((SUBLANES, d), F32).at[:bsz].set(c).at[bsz].set(c_ctx)
    mod = _ada(cc, w_ada, b_ada)
    rope_tabs = _rope_tables(n)
    blk = jnp.arange(LANES) // HEAD_DIM
    ones = (blk[:, None] == blk[None, :]).astype(BF16)
    h_zero = jnp.zeros((bsz, 1, d), F32)
    slot = jnp.arange(SCAN_TILE)
    token_of_slot = SEG_LEN * (slot % SUBLANES) + slot // SUBLANES
    perm = (token_of_slot[:, None] == slot[None, :]).astype(BF16)
    perm_t = perm.T

    for l in range(depth):
        with_ctx = l < depth - 1
        mod_lat = mod[l, :bsz].reshape(bsz, 1, 6 * d)
        mod_ctx = jnp.broadcast_to(mod[l, bsz].reshape(1, 1, 6 * d), (bsz, 1, 6 * d))
        sh1, sc1, g1, sh2, sc2, g2 = [mod_lat[:, :, k * d:(k + 1) * d] for k in range(6)]
        csh1, csc1, cg1, csh2, csc2, cg2 = [mod_ctx[:, :, k * d:(k + 1) * d] for k in range(6)]

        w_attn = w_in[l, :, :ATTN_COLS].astype(BF16)
        w_rnn = w_in[l, :, ATTN_COLS:ATTN_COLS + LRU_WIDTH].astype(BF16)
        w_y = w_in[l, :, ATTN_COLS + LRU_WIDTH:ATTN_COLS + 2 * LRU_WIDTH].astype(BF16)
        w_g = w_in[l, :, ATTN_COLS + 2 * LRU_WIDTH:].astype(BF16)
        gq = jnp.tile(b_q_gain[l], 2).reshape(1, LANES)
        gk = jnp.tile(b_k_gain[l], 2).reshape(1, LANES)

        qa, ka, va, qb, kb, vb = _inproj_attn(x, sc1, sh1, w_attn, gq, gk, ones, rope_tabs, 512)
        qa_c, ka_c, va_c, qb_c, kb_c, vb_c = _inproj_attn(ctx, csc1, csh1, w_attn, gq, gk, ones, None, m)

        ya = _attn_window(qa, ka, va, ka_c, va_c, a_sink[l])
        yb = _attn_global(qb, [(kb, vb), (kb_c, vb_c)], None, BLOCK, "attn_global", lookahead=True)

        cb = c_conv_b[l].reshape(1, d)
        xs = _inproj_rnn(x, sc1, sh1, w_rnn, perm, c_conv_w[l], cb)
        xs_c = _inproj_rnn(ctx, csc1, csh1, w_rnn, perm, c_conv_w[l], cb)
        hs, hs_c = [], []
        for di, reverse in enumerate((False, True)):
            wg = _gate_weights(c_wr[l, di], c_wi[l, di])
            vecs = [v[l, di].reshape(1, d) for v in (c_br, c_bi, c_lam)]
            h_c, h_fin = _rglru(xs_c, h_zero, wg, *vecs, reverse)
            h_l, _ = _rglru(xs, h_fin, wg, *vecs, reverse)
            hs.append(h_l)
            hs_c.append(h_c)

        merge_w = [perm, perm_t, w_y, w_g] + [w[l].astype(BF16) for w in (w_br_a, w_br_b, w_br_c, w_out)]
        ln1 = [ln1_g[l].reshape(1, d), ln1_b[l].reshape(1, d)]
        ln2 = [ln2_g[l].reshape(1, d), ln2_b[l].reshape(1, d)]
        w1 = w_ff1[l].astype(BF16)
        w2 = w_ff2[l].astype(BF16)

        x = _merge(x, sc1, sh1, g1, ya, yb, hs[0], hs[1], *merge_w, *ln1, 2 * SCAN_TILE)
        x = _mlp(x, sc2, sh2, g2, w1, w2, *ln2, 512)
        if with_ctx:
            ya_c = _attn_global(qa_c, [(ka_c, va_c)], a_sink[l], BLOCK, "attn_ctx_a")
            yb_c = _attn_global(qb_c, [(kb_c, vb_c)], None, BLOCK, "attn_ctx_b")
            ctx = _merge(ctx, csc1, csh1, cg1, ya_c, yb_c, hs_c[0], hs_c[1], *merge_w, *ln1, m)
            ctx = _mlp(ctx, csc2, csh2, cg2, w1, w2, *ln2, m)
    return x
```

```python
import functools

import jax
import jax.numpy as jnp
from jax import lax
from jax.experimental import pallas as pl
from jax.experimental.pallas import tpu as pltpu

D_MODEL = 1024
DEPTH = 2
GRID_W = 64
HEAD_DIM = 64
ROPE_BASE = 10000.0
BLOCK = 128
A_HEADS = 8
A_KV = 2
B_HEADS = 8
B_KV = 2
LRU_WIDTH = D_MODEL
LRU_BLOCKS = 16
LRU_BLOCK_DIM = LRU_WIDTH // LRU_BLOCKS
CONV_W = 4
LRU_C = 8.0
D_FF = 4 * D_MODEL
ALPHA = (2 * DEPTH) ** 0.25
LN_EPS = 1e-5
RMS_EPS = 1e-6
NEG_INF = -1e30
LOG2_E = 1.4426950408889634
Q_COLS = A_HEADS * HEAD_DIM
KV_COLS = A_KV * HEAD_DIM
ATTN_COLS = 2 * (Q_COLS + 2 * KV_COLS)
GROUP = A_HEADS // A_KV

LANES = 128
SUBLANES = 8
V7X_VMEM_BYTES = 64 * 1024 * 1024
VMEM_LIMIT_BYTES = V7X_VMEM_BYTES - 8 * 1024 * 1024

SCAN_TILE = 256
SEG_LEN = SCAN_TILE // SUBLANES
LOGIT_RANGE = 100.0
STATIC_CHUNKS = 4
WINDOW_SUBTILES = 8
SQRT_FLOOR = 1e-30

F32 = jnp.float32
BF16 = jnp.bfloat16


def _params(*semantics):
    return pltpu.CompilerParams(dimension_semantics=semantics, vmem_limit_bytes=VMEM_LIMIT_BYTES)


def _resident(block_shape, index_map):
    return pl.BlockSpec(block_shape, index_map, pipeline_mode=pl.Buffered(1))


def _modulate(x, sc_ref, sh_ref):
    return (x * (1.0 + sc_ref[...]) + sh_ref[...]).astype(BF16)


def _layernorm(y, g, b):
    mu = jnp.mean(y, axis=-1, keepdims=True)
    yc = y - mu
    var = jnp.mean(yc * yc, axis=-1, keepdims=True)
    return yc * lax.rsqrt(var + LN_EPS) * g + b


def _ada_kernel(c_ref, w_ref, b_ref, o_ref):
    c = c_ref[...]
    s = c * jax.nn.sigmoid(c)
    o_ref[...] = jnp.dot(s, w_ref[...], preferred_element_type=F32,
                         precision=lax.Precision.HIGHEST) + b_ref[...]


def _ada(cc, w_ada, b_ada):
    depth, d, cols = w_ada.shape
    tn = cols // 4
    return pl.pallas_call(
        _ada_kernel,
        grid=(depth, cols // tn),
        in_specs=[pl.BlockSpec((SUBLANES, d), lambda l, j: (0, 0)),
                  pl.BlockSpec((None, d, tn), lambda l, j: (l, 0, j)),
                  pl.BlockSpec((None, 1, tn), lambda l, j: (l, 0, j))],
        out_specs=pl.BlockSpec((None, SUBLANES, tn), lambda l, j: (l, 0, j)),
        out_shape=jax.ShapeDtypeStruct((depth, SUBLANES, cols), F32),
        compiler_params=_params("arbitrary", "arbitrary"),
        name="ada_mod",
    )(cc, w_ada, b_ada.reshape(depth, 1, cols))


def _inproj_attn_kernel(*refs, rope):
    if rope:
        (x_ref, sc_ref, sh_ref, w_ref, gq_ref, gk_ref, ones_ref, cos_ref, sin_ref,
         qa_ref, ka_ref, va_ref, qb_ref, kb_ref, vb_ref) = refs
    else:
        (x_ref, sc_ref, sh_ref, w_ref, gq_ref, gk_ref, ones_ref,
         qa_ref, ka_ref, va_ref, qb_ref, kb_ref, vb_ref) = refs
    h = _modulate(x_ref[...], sc_ref, sh_ref)
    z = jnp.dot(h, w_ref[...], preferred_element_type=F32)
    tm = z.shape[0]
    lane = lax.broadcasted_iota(jnp.int32, (tm, LANES), 1)
    first16 = (lane & 31) < 16
    ones = ones_ref[...]
    scale = HEAD_DIM ** -0.5 * LOG2_E

    def rot(zc):
        if not rope:
            return zc
        partner = jnp.where(first16, pltpu.roll(zc, LANES - 16, 1), pltpu.roll(zc, 16, 1))
        return zc * cos_ref[...] + partner * sin_ref[...]

    def rms(zc, g_ref):
        sq = zc * zc
        hi = sq.astype(BF16)
        lo = (sq - hi.astype(F32)).astype(BF16)
        ss = (jnp.dot(hi, ones, preferred_element_type=F32)
              + jnp.dot(lo, ones, preferred_element_type=F32))
        return zc * lax.rsqrt(ss * (1.0 / HEAD_DIM) + RMS_EPS) * g_ref[...]

    def col(start):
        return z[:, start:start + LANES]

    for c in range(Q_COLS // LANES):
        qa_ref[:, c * LANES:(c + 1) * LANES] = (rot(col(c * LANES)) * scale).astype(BF16)
    base = Q_COLS
    ka_ref[...] = rot(col(base)).astype(BF16)
    va_t = col(base + KV_COLS).T.astype(BF16)
    for c in range(va_ref.shape[0]):
        va_ref[c] = va_t[:, c * BLOCK:(c + 1) * BLOCK]
    base = Q_COLS + 2 * KV_COLS
    for c in range(Q_COLS // LANES):
        qb_ref[:, c * LANES:(c + 1) * LANES] = (
            rot(rms(col(base + c * LANES), gq_ref)) * scale).astype(BF16)
    base += Q_COLS
    kb_ref[...] = rot(rms(col(base), gk_ref)).astype(BF16)
    vb_ref[0] = col(base + KV_COLS).T.astype(BF16)


def _inproj_attn(x, sc, sh, w, gq, gk, ones, rope_tabs, tm):
    bsz, rows, d = x.shape
    rope = rope_tabs is not None
    mod_spec = pl.BlockSpec((None, 1, d), lambda b, i: (b, 0, 0))
    vec_spec = _resident((1, LANES), lambda b, i: (0, 0))
    in_specs = [pl.BlockSpec((None, tm, d), lambda b, i: (b, i, 0)), mod_spec, mod_spec,
                _resident((d, ATTN_COLS), lambda b, i: (0, 0)), vec_spec, vec_spec,
                _resident((LANES, LANES), lambda b, i: (0, 0))]
    args = [x, sc, sh, w, gq, gk, ones]
    if rope:
        tab_spec = pl.BlockSpec((tm, LANES), lambda b, i: (i, 0))
        in_specs += [tab_spec, tab_spec]
        args += list(rope_tabs)

    def out(cols):
        return (pl.BlockSpec((None, tm, cols), lambda b, i: (b, i, 0)),
                jax.ShapeDtypeStruct((bsz, rows, cols), BF16))

    def out_t(slab):
        return (pl.BlockSpec((None, tm // slab, KV_COLS, slab), lambda b, i: (b, i, 0, 0)),
                jax.ShapeDtypeStruct((bsz, rows // slab, KV_COLS, slab), BF16))

    outs = [out(Q_COLS), out(KV_COLS), out_t(BLOCK), out(Q_COLS), out(KV_COLS), out_t(tm)]
    return pl.pallas_call(
        functools.partial(_inproj_attn_kernel, rope=rope),
        grid=(bsz, rows // tm),
        in_specs=in_specs,
        out_specs=[o[0] for o in outs],
        out_shape=[o[1] for o in outs],
        compiler_params=_params("arbitrary", "arbitrary"),
        name="inproj_attn_rope" if rope else "inproj_attn_ctx",
    )(*args)


def _inproj_rnn_kernel(xp_ref, xc_ref, xn_ref, sc_ref, sh_ref, w_ref, perm_ref, cw_ref, cb_ref, o_ref):
    i = pl.program_id(1)
    nt = pl.num_programs(1)
    rows_t = xc_ref.shape[0]
    h = _modulate(xc_ref[...], sc_ref, sh_ref)
    hp = jnp.dot(perm_ref[...], h, preferred_element_type=F32).astype(BF16)
    halo = _modulate(jnp.concatenate([xp_ref[...], xn_ref[...]], axis=0), sc_ref, sh_ref)
    lhs = jnp.concatenate([hp, halo], axis=0)
    zall = jnp.dot(lhs, w_ref[...], preferred_element_type=F32)
    z = zall[:rows_t]
    zp = jnp.where(i > 0, zall[rows_t:rows_t + SUBLANES], 0.0)
    zn = jnp.where(i < nt - 1, zall[rows_t + SUBLANES:], 0.0)
    sub = lax.broadcasted_iota(jnp.int32, (SUBLANES, z.shape[1]), 0)
    first, second, last = z[0:SUBLANES], z[SUBLANES:2 * SUBLANES], z[rows_t - SUBLANES:]
    before = jnp.where(sub == 0, zp[SUBLANES - 1:SUBLANES], pltpu.roll(last, 1, 0))
    after1 = jnp.where(sub == SUBLANES - 1, zn[0:1], pltpu.roll(first, SUBLANES - 1, 0))
    after2 = jnp.where(sub == SUBLANES - 1, zn[1:2], pltpu.roll(second, SUBLANES - 1, 0))
    taps = [jnp.concatenate([before, z[:rows_t - SUBLANES]], axis=0),
            z,
            jnp.concatenate([z[SUBLANES:], after1], axis=0),
            jnp.concatenate([z[2 * SUBLANES:], after1, after2], axis=0)]
    out = cb_ref[...]
    for tap in range(CONV_W):
        out = out + cw_ref[tap:tap + 1, :] * taps[tap]
    o_ref[...] = out


def _inproj_rnn(x, sc, sh, w, perm, cw, cb):
    bsz, rows, d = x.shape
    cols = w.shape[1]
    nt = rows // SCAN_TILE
    hb = SCAN_TILE // SUBLANES
    mod_spec = pl.BlockSpec((None, 1, d), lambda b, i: (b, 0, 0))
    vec = _resident((1, cols), lambda b, i: (0, 0))
    return pl.pallas_call(
        _inproj_rnn_kernel,
        grid=(bsz, nt),
        in_specs=[pl.BlockSpec((None, SUBLANES, d), lambda b, i: (b, jnp.maximum(i * hb - 1, 0), 0)),
                  pl.BlockSpec((None, SCAN_TILE, d), lambda b, i: (b, i, 0)),
                  pl.BlockSpec((None, SUBLANES, d),
                               lambda b, i: (b, jnp.minimum((i + 1) * hb, rows // SUBLANES - 1), 0)),
                  mod_spec, mod_spec, _resident((d, cols), lambda b, i: (0, 0)),
                  _resident((SCAN_TILE, SCAN_TILE), lambda b, i: (0, 0)),
                  _resident((CONV_W, cols), lambda b, i: (0, 0)), vec],
        out_specs=pl.BlockSpec((None, SCAN_TILE, cols), lambda b, i: (b, i, 0)),
        out_shape=jax.ShapeDtypeStruct((bsz, rows, cols), F32),
        compiler_params=_params("arbitrary", "arbitrary"),
        name="inproj_rnn_conv",
    )(x, x, x, sc, sh, w, perm, cw, cb)


def _attn_kernel(*refs, segs, n_src, has_sink, tq, nsub, lookahead, scan):
    refs = list(refs)
    sink_ref = refs.pop(0) if has_sink else None
    q_ref = refs.pop(0)
    qn_ref = refs.pop(0) if lookahead else None
    kv_refs = [(refs.pop(0), refs.pop(0)) for _ in range(n_src)]
    if scan:
        xs_ref, wg_ref, br_ref, bi_ref, lam_ref, h0_ref = [refs.pop(0) for _ in range(6)]
    o_ref = refs.pop(0)
    if scan:
        h_ref = refs.pop(0)
        u_scr, a_scr, nsp_scr, carry_scr = [refs.pop() for _ in range(4)]
    i = pl.program_id(1)
    nblk = pl.num_programs(1) * nsub
    width = GROUP * tq

    def sink_row(kvi):
        return jnp.concatenate(
            [jnp.full((1, tq), sink_ref[GROUP * kvi + hh] * LOG2_E, F32) for hh in range(GROUP)], axis=1)

    def query_operand(q, kvi):
        qt = q.T
        heads = jnp.concatenate(
            [qt[(GROUP * kvi + hh) * HEAD_DIM:(GROUP * kvi + hh + 1) * HEAD_DIM, :] for hh in range(GROUP)],
            axis=1)
        parts = [jnp.zeros((HEAD_DIM, width), BF16)] * A_KV
        parts[kvi] = heads
        return jnp.concatenate(parts, axis=0)

    def run_tile(sub):
        mine = refs[sub * 4 * A_KV:(sub + 1) * 4 * A_KV]
        s_scr, m_scr, l_scr, acc_scr = [mine[g * A_KV:(g + 1) * A_KV] for g in range(4)]
        pos = i * nsub + sub
        rows = slice(sub * tq, (sub + 1) * tq)

        def seg_mask(where):
            if where not in ("prev", "next"):
                return None
            j = lax.broadcasted_iota(jnp.int32, (BLOCK, width), 0)
            r = lax.broadcasted_iota(jnp.int32, (BLOCK, width), 1) & (tq - 1)
            far = jnp.int32(4 * BLOCK)
            if where == "prev":
                return (j - r) >= jnp.where(pos > 0, 0, far)
            return (r - j) >= jnp.where(pos < nblk - 1, 0, far)

        def block_of(where):
            return {"cur": pos, "prev": jnp.maximum(pos - 1, 0), "next": jnp.minimum(pos + 1, nblk - 1)}[where]

        def for_each_chunk(*fns, along=None):
            base = 0
            for src, where, nchunks, kc in segs:
                k_ref, vt_ref = kv_refs[src]
                if where is None:
                    row0, slab0 = 0, 0
                else:
                    slab0 = block_of(where)
                    row0 = pl.multiple_of(slab0 * BLOCK, BLOCK)
                if nchunks <= STATIC_CHUNKS:
                    for ci in range(nchunks):
                        for fn in fns:
                            fn(k_ref, vt_ref, row0 + ci * kc, slab0 + ci, base + ci * kc, kc, where)
                else:
                    def body(ci, carry, k_ref=k_ref, vt_ref=vt_ref, kc=kc, base=base):
                        r0 = pl.multiple_of(ci * kc, kc)
                        for fn in fns:
                            fn(k_ref, vt_ref, r0, ci, base + r0, kc, None)
                        if along is not None:
                            along(ci)
                        return carry
                    lax.fori_loop(0, nchunks, body, 0, unroll=4)
                base += nchunks * kc

        def start_scores(kvi):
            m_scr[kvi][...] = sink_row(kvi) if has_sink else jnp.full((1, width), NEG_INF, F32)

        def scores(kvi, qz, k_ref, vt_ref, r0, slab, srow, kc, where):
            s = jnp.dot(k_ref[pl.ds(r0, kc), :], qz, preferred_element_type=F32)
            mask = seg_mask(where)
            if mask is not None:
                s = jnp.where(mask, s, NEG_INF)
            s_scr[kvi][pl.ds(srow, kc), :] = s
            m_scr[kvi][...] = jnp.maximum(m_scr[kvi][...], jnp.max(s, axis=0, keepdims=True))

        def start_values(kvi):
            if has_sink:
                l_scr[kvi][...] = jnp.exp2(sink_row(kvi) - m_scr[kvi][...])
            else:
                l_scr[kvi][...] = jnp.zeros((1, width), F32)
            acc_scr[kvi][...] = jnp.zeros((HEAD_DIM, width), F32)

        def values(kvi, k_ref, vt_ref, r0, slab, srow, kc, where):
            p = jnp.exp2(s_scr[kvi][pl.ds(srow, kc), :] - m_scr[kvi][...])
            l_scr[kvi][...] += jnp.sum(p, axis=0, keepdims=True)
            acc_scr[kvi][...] += jnp.dot(vt_ref[slab, kvi * HEAD_DIM:(kvi + 1) * HEAD_DIM, :], p.astype(BF16),
                                         preferred_element_type=F32)

        def first_scores():
            start_scores(0)
            for_each_chunk(functools.partial(scores, 0, query_operand(q_ref[rows, :], 0)))

        def phase_a():
            if lookahead:
                pl.when(i == 0)(first_scores)
            else:
                first_scores()

        half = SCAN_TILE // 2

        def scan_first_half(c):
            _rglru_gates(xs_ref, wg_ref, br_ref, bi_ref, nsp_scr, a_scr, u_scr, c, slice(0, half))

        def scan_second_half(c):
            _rglru_gates(xs_ref, wg_ref, br_ref, bi_ref, nsp_scr, a_scr, u_scr, c, slice(half, SCAN_TILE))
            _rglru_scan(a_scr, u_scr, h_ref, carry_scr, c, scan == "bwd")

        def phase_b():
            start_scores(1)
            start_values(0)
            for_each_chunk(functools.partial(scores, 1, query_operand(q_ref[rows, :], 1)),
                           functools.partial(values, 0), along=scan_first_half if scan else None)

        def phase_c():
            start_values(1)
            if lookahead:
                start_scores(0)
                for_each_chunk(functools.partial(values, 1),
                               functools.partial(scores, 0, query_operand(qn_ref[...], 0)),
                               along=scan_second_half if scan else None)
            else:
                for_each_chunk(functools.partial(values, 1))

        def finish():
            outs = []
            for kvi in range(A_KV):
                o = acc_scr[kvi][...] / l_scr[kvi][...]
                outs += [o[:, hh * tq:(hh + 1) * tq] for hh in range(GROUP)]
            o_ref[rows, :] = jnp.concatenate(outs, axis=0).T.astype(BF16)

        return phase_a, phase_b, phase_c, finish

    if scan:
        nsp_scr[...] = _neg_softplus_neg(lam_ref[...])

        @pl.when(i == 0)
        def _():
            carry_scr[...] = h0_ref[...]

    for phases in zip(*[run_tile(sub) for sub in range(nsub)]):
        for phase in phases:
            phase()


def _attn_bounded_kernel(*refs, segs, n_src, tq, scan):
    refs = list(refs)
    q_ref = refs.pop(0)
    kv_refs = [(refs.pop(0), refs.pop(0)) for _ in range(n_src)]
    if scan:
        xs_ref, wg_ref, br_ref, bi_ref, lam_ref, h0_ref = [refs.pop(0) for _ in range(6)]
    o_ref = refs.pop(0)
    if scan:
        h_ref = refs.pop(0)
        carry_scr, nsp_scr = refs.pop(), refs.pop()
        gate_scr = [(refs.pop(), refs.pop()) for _ in range(2)]
    l_scr, acc_scr = refs[:A_KV], refs[A_KV:2 * A_KV]
    s_scr = [refs[2 * A_KV:3 * A_KV], refs[3 * A_KV:4 * A_KV]]
    i = pl.program_id(1)
    width = GROUP * tq

    if scan:
        nsp_scr[...] = _neg_softplus_neg(lam_ref[...])

        @pl.when(i == 0)
        def _():
            carry_scr[...] = h0_ref[...]

    qt = q_ref[...].T
    qz = []
    for kvi in range(A_KV):
        heads = jnp.concatenate(
            [qt[(GROUP * kvi + hh) * HEAD_DIM:(GROUP * kvi + hh + 1) * HEAD_DIM, :] for hh in range(GROUP)],
            axis=1)
        parts = [jnp.zeros((HEAD_DIM, width), BF16)] * A_KV
        parts[kvi] = heads
        qz.append(jnp.concatenate(parts, axis=0))
        l_scr[kvi][...] = jnp.zeros((1, width), F32)
        acc_scr[kvi][...] = jnp.zeros((HEAD_DIM, width), F32)

    def scores(k_ref, r0, kc, buf):
        k = k_ref[pl.ds(r0, kc), :]
        for kvi in range(A_KV):
            buf[kvi][0:kc, :] = jnp.dot(k, qz[kvi], preferred_element_type=F32)

    def values(vt_ref, slab, kc, buf):
        for kvi in range(A_KV):
            p = jnp.exp2(buf[kvi][0:kc, :])
            l_scr[kvi][...] += jnp.sum(p, axis=0, keepdims=True)
            acc_scr[kvi][...] += jnp.dot(vt_ref[slab, kvi * HEAD_DIM:(kvi + 1) * HEAD_DIM, :], p.astype(BF16),
                                         preferred_element_type=F32)

    def gates(c, buf):
        _rglru_gates(xs_ref, wg_ref, br_ref, bi_ref, nsp_scr, buf[0], buf[1], c, slice(None), to_chunk=True)

    def scan_chunk(c, buf):
        _rglru_scan(buf[0], buf[1], h_ref, carry_scr, c, scan == "bwd", from_chunk=True)

    (src0, _, nchunks, kc), rest = segs[0], segs[1:]
    k_ref, vt_ref = kv_refs[src0]
    assert nchunks % 2 == 0
    last = nchunks - 1
    scores(k_ref, 0, kc, s_scr[0])
    if scan:
        gates(0, gate_scr[0])

    def body(j, carry):
        even, odd = 2 * j, 2 * j + 1
        nxt = jnp.minimum(even + 2, last)
        scores(k_ref, pl.multiple_of(odd * kc, kc), kc, s_scr[1])
        values(vt_ref, even, kc, s_scr[0])
        if scan:
            gates(odd, gate_scr[1])
            scan_chunk(even, gate_scr[0])
        scores(k_ref, pl.multiple_of(nxt * kc, kc), kc, s_scr[0])
        values(vt_ref, odd, kc, s_scr[1])
        if scan:
            gates(nxt, gate_scr[0])
            scan_chunk(odd, gate_scr[1])
        return carry

    lax.fori_loop(0, nchunks // 2, body, 0, unroll=True)

    for src, where, nch, kcs in rest:
        assert where is None and nch <= STATIC_CHUNKS
        kr, vr = kv_refs[src]
        for ci in range(nch):
            scores(kr, ci * kcs, kcs, s_scr[ci % 2])
            values(vr, ci, kcs, s_scr[ci % 2])

    outs = []
    for kvi in range(A_KV):
        o = acc_scr[kvi][...] / l_scr[kvi][...]
        outs += [o[:, hh * tq:(hh + 1) * tq] for hh in range(GROUP)]
    o_ref[...] = jnp.concatenate(outs, axis=0).T.astype(BF16)


def _attention(q, kv_arrays, segs, sink, tq, nsub, name, lookahead=False, part=(0, 1), scan=None,
               bounded=False):
    assert not (lookahead and nsub != 1) and not (bounded and (lookahead or nsub != 1 or sink is not None))
    bsz, rows, _ = q.shape
    nb = rows // (tq * nsub * part[1])
    first = part[0] * nb
    has_sink = sink is not None
    in_specs, args = [], []
    if has_sink:
        in_specs.append(pl.BlockSpec(memory_space=pltpu.SMEM))
        args.append(sink)
    in_specs.append(pl.BlockSpec((None, tq * nsub, Q_COLS), lambda b, i: (b, first + i, 0)))
    args.append(q)
    if lookahead:
        in_specs.append(pl.BlockSpec((None, tq, Q_COLS), lambda b, i: (b, first + jnp.minimum(i + 1, nb - 1), 0)))
        args.append(q)
    for k, vt in kv_arrays:
        in_specs += [pl.BlockSpec((None,) + k.shape[1:], lambda b, i: (b, 0, 0)),
                     pl.BlockSpec((None,) + vt.shape[1:], lambda b, i: (b, 0, 0, 0))]
        args += [k, vt]
    width = GROUP * tq
    nk_total = sum(nchunks * kc for _, _, nchunks, kc in segs)
    per_head = [(nk_total, width), (1, width), (1, width), (HEAD_DIM, width)]
    if bounded:
        kc0 = segs[0][3]
        per_head = [(1, width), (HEAD_DIM, width), (kc0, width), (kc0, width)]
    out_specs = [pl.BlockSpec((None, tq * nsub, Q_COLS), lambda b, i: (b, i, 0))]
    out_shape = [jax.ShapeDtypeStruct((bsz, rows // part[1], Q_COLS), BF16)]
    scratch = [pltpu.VMEM(shape, F32) for _ in range(nsub) for shape in per_head for _ in range(A_KV)]
    direction = None
    if scan is not None:
        direction, xs, h0, wg, br, bi, lam = scan
        srows, swidth = xs.shape[1:]
        assert srows // SCAN_TILE == nb and swidth // LANES == segs[0][2]
        tile = pl.BlockSpec((None, SCAN_TILE, swidth),
                            (lambda b, i: (b, nb - 1 - i, 0)) if direction == "bwd" else (lambda b, i: (b, i, 0)))
        vec = _resident((1, swidth), lambda b, i: (0, 0))
        in_specs += [tile, _resident(wg.shape, lambda b, i: (0, 0, 0)), vec, vec, vec,
                     pl.BlockSpec((None, 1, swidth), lambda b, i: (b, 0, 0))]
        args += [xs, wg, br, bi, lam, h0]
        out_specs.append(tile)
        out_shape.append(jax.ShapeDtypeStruct(xs.shape, F32))
        if bounded:
            scratch += [pltpu.VMEM((SCAN_TILE, LANES), F32) for _ in range(4)]
            scratch += [pltpu.VMEM((1, swidth), F32), pltpu.VMEM((1, swidth), F32)]
        else:
            scratch += [pltpu.VMEM((1, swidth), F32), pltpu.VMEM((1, swidth), F32),
                        pltpu.VMEM((SCAN_TILE, swidth), F32), pltpu.VMEM((SCAN_TILE, swidth), F32)]
    if bounded:
        body = functools.partial(_attn_bounded_kernel, segs=tuple(segs), n_src=len(kv_arrays), tq=tq, scan=direction)
    else:
        body = functools.partial(_attn_kernel, segs=tuple(segs), n_src=len(kv_arrays), has_sink=has_sink, tq=tq,
                                 nsub=nsub, lookahead=lookahead, scan=direction)
    outs = pl.pallas_call(
        body,
        grid=(bsz, nb),
        in_specs=in_specs,
        out_specs=out_specs,
        out_shape=out_shape,
        scratch_shapes=scratch,
        compiler_params=_params("arbitrary", "arbitrary"),
        name=name,
    )(*args)
    return outs if scan is not None else outs[0]


def _whole_seg(src, vt):
    return (src, None, vt.shape[1], vt.shape[3])


def _attn_window(q, k, vt, k_ctx, vt_ctx, sink):
    assert vt.shape[3] == BLOCK
    segs = [(0, "cur", 1, BLOCK), (0, "prev", 1, BLOCK), (0, "next", 1, BLOCK), _whole_seg(1, vt_ctx)]
    return _attention(q, [(k, vt), (k_ctx, vt_ctx)], segs, sink, BLOCK, WINDOW_SUBTILES, "attn_window")


def _attn_global(q, kv_list, sink, tq, name, lookahead=False, part=(0, 1), scan=None, bounded=False):
    segs = [_whole_seg(src, vt) for src, (_, vt) in enumerate(kv_list)]
    return _attention(q, kv_list, segs, sink, tq, 1, name, lookahead, part, scan, bounded)


def _lane_chunk(c):
    if isinstance(c, int):
        return slice(c * LANES, (c + 1) * LANES)
    return pl.ds(pl.multiple_of(c * LANES, LANES), LANES)


def _neg_softplus_neg(lam):
    return -(jnp.maximum(-lam, 0.0) + jnp.log1p(jnp.exp(-jnp.abs(lam))))


def _rglru_gates(x_ref, wg_ref, br_ref, bi_ref, nsp_ref, a_scr, u_scr, c, rows, to_chunk=False):
    cs = _lane_chunk(c)
    dst = slice(None) if to_chunk else cs
    xcc = x_ref[rows, cs]
    g = jnp.dot(xcc.astype(BF16), wg_ref[c], preferred_element_type=F32)
    r = jax.nn.sigmoid(g[:, :LANES] + br_ref[:, cs])
    ig = jax.nn.sigmoid(g[:, LANES:] + bi_ref[:, cs])
    log_a = (LRU_C * r) * nsp_ref[:, cs]
    a = jnp.exp(log_a)
    w = -jnp.tanh(log_a) * (a * a + 1.0)
    a_scr[rows, dst] = a
    u_scr[rows, dst] = (w * lax.rsqrt(jnp.maximum(w, SQRT_FLOOR))) * (ig * xcc)


def _rglru_scan(a_scr, u_scr, h_ref, carry_scr, c, reverse, from_chunk=False):
    cs = _lane_chunk(c)
    src = slice(None) if from_chunk else cs
    ngroups = a_scr.shape[0] // SUBLANES
    sub = lax.broadcasted_iota(jnp.int32, (SUBLANES, LANES), 0)
    order = range(ngroups - 1, -1, -1) if reverse else range(ngroups)

    def rows(g):
        return slice(g * SUBLANES, (g + 1) * SUBLANES)

    h_loc = decay = None
    for gi in order:
        a_g, u_g = a_scr[rows(gi), src], u_scr[rows(gi), src]
        h_loc = u_g if h_loc is None else a_g * h_loc + u_g
        decay = a_g if decay is None else a_g * decay
        h_ref[rows(gi), cs] = h_loc
        a_scr[rows(gi), src] = decay

    end, dec = h_loc, decay
    for s in (1, 2, 4):
        if reverse:
            ok = sub < SUBLANES - s
            shift = SUBLANES - s
        else:
            ok = sub >= s
            shift = s
        end = end + dec * jnp.where(ok, pltpu.roll(end, shift, 0), 0.0)
        dec = dec * jnp.where(ok, pltpu.roll(dec, shift, 0), 1.0)
    carry = carry_scr[:, cs]
    seg_end = end + dec * carry
    if reverse:
        start = jnp.where(sub == SUBLANES - 1, carry, pltpu.roll(seg_end, SUBLANES - 1, 0))
        carry_scr[:, cs] = seg_end[0:1]
    else:
        start = jnp.where(sub == 0, carry, pltpu.roll(seg_end, 1, 0))
        carry_scr[:, cs] = seg_end[SUBLANES - 1:SUBLANES]
    for gi in order:
        h_ref[rows(gi), cs] = h_ref[rows(gi), cs] + a_scr[rows(gi), src] * start


def _rglru_kernel(x_ref, wg_ref, br_ref, bi_ref, lam_ref, h0_ref,
                  h_ref, hfin_ref, carry_scr, nsp_scr, a_scr, u_scr, *, reverse):
    j = pl.program_id(1)
    nt = pl.num_programs(1)

    @pl.when(j == 0)
    def _():
        carry_scr[...] = h0_ref[...]

    nsp_scr[...] = _neg_softplus_neg(lam_ref[...])

    def chunk(c, carry):
        _rglru_gates(x_ref, wg_ref, br_ref, bi_ref, nsp_scr, a_scr, u_scr, c, slice(None))
        _rglru_scan(a_scr, u_scr, h_ref, carry_scr, c, reverse)
        return carry

    lax.fori_loop(0, x_ref.shape[1] // LANES, chunk, 0)

    @pl.when(j == nt - 1)
    def _():
        hfin_ref[...] = carry_scr[...]


def _rglru(xs, h0, wg, br, bi, lam, reverse):
    bsz, rows, width = xs.shape
    nt = rows // SCAN_TILE

    def tpos(j):
        return nt - 1 - j if reverse else j

    vec = _resident((1, width), lambda b, j: (0, 0))
    tile = pl.BlockSpec((None, SCAN_TILE, width), lambda b, j: (b, tpos(j), 0))
    state = pl.BlockSpec((None, 1, width), lambda b, j: (b, 0, 0))
    return pl.pallas_call(
        functools.partial(_rglru_kernel, reverse=reverse),
        grid=(bsz, nt),
        in_specs=[tile, _resident(wg.shape, lambda b, j: (0, 0, 0)), vec, vec, vec, state],
        out_specs=[tile, state],
        out_shape=[jax.ShapeDtypeStruct((bsz, rows, width), F32),
                   jax.ShapeDtypeStruct((bsz, 1, width), F32)],
        scratch_shapes=[pltpu.VMEM((1, width), F32), pltpu.VMEM((1, width), F32),
                        pltpu.VMEM((SCAN_TILE, width), F32), pltpu.VMEM((SCAN_TILE, width), F32)],
        compiler_params=_params("arbitrary", "arbitrary"),
        name="rglru_bwd" if reverse else "rglru_fwd",
    )(xs, wg, br, bi, lam, h0)


def _merge_kernel(x_ref, sc_ref, sh_ref, gate_ref, ya_ref, yb_ref, hf_ref, hb_ref, perm_ref, perm_t_ref,
                  wy_ref, wg_ref, wa_ref, wb_ref, wc_ref, wo_ref, lng_ref, lnb_ref, o_ref):
    x = x_ref[...]
    d = x.shape[1]
    h = _modulate(x, sc_ref, sh_ref)
    def per_scan_tile(p_ref, rows):
        return jnp.concatenate(
            [jnp.dot(p_ref[...], rows[k:k + SCAN_TILE], preferred_element_type=F32).astype(BF16)
             for k in range(0, rows.shape[0], SCAN_TILE)], axis=0)

    y_rnn = jnp.dot(per_scan_tile(perm_ref, h), wy_ref[...], preferred_element_type=F32)
    yc = per_scan_tile(perm_t_ref, ((hf_ref[...] + hb_ref[...]) * jax.nn.gelu(y_rnn)).astype(BF16))
    zz = jnp.dot(h, wg_ref[...], preferred_element_type=F32)
    mix = (jax.nn.sigmoid(zz[:, 0:d]) * jnp.dot(ya_ref[...], wa_ref[...], preferred_element_type=F32)
           + jax.nn.sigmoid(zz[:, d:2 * d]) * jnp.dot(yb_ref[...], wb_ref[...], preferred_element_type=F32)
           + jax.nn.sigmoid(zz[:, 2 * d:3 * d]) * jnp.dot(yc, wc_ref[...], preferred_element_type=F32))
    o = jnp.dot(mix.astype(BF16), wo_ref[...], preferred_element_type=F32)
    o_ref[...] = _layernorm(ALPHA * x + gate_ref[...] * o, lng_ref[...], lnb_ref[...])


def _merge(x, sc, sh, gate, ya, yb, hf, hb, perm, perm_t, wy, wg, wa, wb, wc, wo, lng, lnb, tm):
    bsz, rows, d = x.shape
    mod_spec = pl.BlockSpec((None, 1, d), lambda b, i: (b, 0, 0))
    vec = _resident((1, d), lambda b, i: (0, 0))

    def tile(cols):
        return pl.BlockSpec((None, tm, cols), lambda b, i: (b, i, 0))

    def weight(w):
        return _resident(w.shape, lambda b, i: (0, 0))

    weights = [perm, perm_t, wy, wg, wa, wb, wc, wo]
    return pl.pallas_call(
        _merge_kernel,
        grid=(bsz, rows // tm),
        in_specs=[tile(d), mod_spec, mod_spec, mod_spec, tile(Q_COLS), tile(Q_COLS), tile(d), tile(d)]
                 + [weight(w) for w in weights] + [vec, vec],
        out_specs=tile(d),
        out_shape=jax.ShapeDtypeStruct((bsz, rows, d), F32),
        compiler_params=_params("arbitrary", "arbitrary"),
        name="merge_ln",
    )(x, sc, sh, gate, ya, yb, hf, hb, *weights, lng, lnb)


def _mlp_kernel(x_ref, sc_ref, sh_ref, gate_ref, w1_ref, w2_ref, lng_ref, lnb_ref, o_ref):
    x = x_ref[...]
    h = _modulate(x, sc_ref, sh_ref)
    u = jnp.dot(h, w1_ref[...], preferred_element_type=F32)
    u = jnp.square(jnp.maximum(u, 0.0)).astype(BF16)
    o = jnp.dot(u, w2_ref[...], preferred_element_type=F32)
    o_ref[...] = _layernorm(ALPHA * x + gate_ref[...] * o, lng_ref[...], lnb_ref[...])


def _mlp(x, sc, sh, gate, w1, w2, lng, lnb, tm):
    bsz, rows, d = x.shape
    mod_spec = pl.BlockSpec((None, 1, d), lambda b, i: (b, 0, 0))
    vec = _resident((1, d), lambda b, i: (0, 0))
    tile = pl.BlockSpec((None, tm, d), lambda b, i: (b, i, 0))
    return pl.pallas_call(
        _mlp_kernel,
        grid=(bsz, rows // tm),
        in_specs=[tile, mod_spec, mod_spec, mod_spec,
                  _resident(w1.shape, lambda b, i: (0, 0)), _resident(w2.shape, lambda b, i: (0, 0)), vec, vec],
        out_specs=tile,
        out_shape=jax.ShapeDtypeStruct((bsz, rows, d), F32),
        compiler_params=_params("arbitrary", "arbitrary"),
        name="mlp_ln",
    )(x, sc, sh, gate, w1, w2, lng, lnb)


def _rope_tables(n):
    pos = jnp.arange(n)
    row = (pos // GRID_W).astype(F32)
    colp = (pos % GRID_W).astype(F32)
    nf = HEAD_DIM // 4
    inv = ROPE_BASE ** (-jnp.arange(nf, dtype=F32) / nf)
    ang_r = row[:, None] * inv
    ang_c = colp[:, None] * inv
    cos_t = jnp.concatenate([jnp.cos(ang_r), jnp.cos(ang_r), jnp.cos(ang_c), jnp.cos(ang_c)], axis=-1)
    sin_t = jnp.concatenate([-jnp.sin(ang_r), jnp.sin(ang_r), -jnp.sin(ang_c), jnp.sin(ang_c)], axis=-1)
    return jnp.tile(cos_t, (1, 2)), jnp.tile(sin_t, (1, 2))


def _gate_weights(w_r, w_i):
    def pair(w):
        w = w.reshape(LRU_BLOCKS // 2, 2, LRU_BLOCK_DIM, LRU_BLOCK_DIM)
        z = jnp.zeros_like(w[:, 0])
        top = jnp.concatenate([w[:, 0], z], axis=-1)
        bot = jnp.concatenate([z, w[:, 1]], axis=-1)
        return jnp.concatenate([top, bot], axis=-2)
    return jnp.concatenate([pair(w_r), pair(w_i)], axis=-1).astype(BF16)


def kernel(x, c, ctx, c_ctx, w_ada, b_ada, w_in, a_sink, b_q_gain, b_k_gain, c_conv_w, c_conv_b,
           c_wr, c_br, c_wi, c_bi, c_lam, w_br_a, w_br_b, w_br_c, w_out, ln1_g, ln1_b,
           w_ff1, w_ff2, ln2_g, ln2_b):
    bsz, n, d = x.shape
    m = ctx.shape[1]
    depth = w_ada.shape[0]

    cc = jnp.zeros((SUBLANES, d), F32).at[:bsz].set(c).at[bsz].set(c_ctx)
    mod = _ada(cc, w_ada, b_ada)
    rope_tabs = _rope_tables(n)
    blk = jnp.arange(LANES) // HEAD_DIM
    ones = (blk[:, None] == blk[None, :]).astype(BF16)
    h_zero = jnp.zeros((bsz, 1, d), F32)
    slot = jnp.arange(SCAN_TILE)
    token_of_slot = SEG_LEN * (slot % SUBLANES) + slot // SUBLANES
    perm = (token_of_slot[:, None] == slot[None, :]).astype(BF16)
    perm_t = perm.T

    for l in range(depth):
        with_ctx = l < depth - 1
        mod_lat = mod[l, :bsz].reshape(bsz, 1, 6 * d)
        mod_ctx = jnp.broadcast_to(mod[l, bsz].reshape(1, 1, 6 * d), (bsz, 1, 6 * d))
        sh1, sc1, g1, sh2, sc2, g2 = [mod_lat[:, :, k * d:(k + 1) * d] for k in range(6)]
        csh1, csc1, cg1, csh2, csc2, cg2 = [mod_ctx[:, :, k * d:(k + 1) * d] for k in range(6)]

        w_attn = w_in[l, :, :ATTN_COLS].astype(BF16)
        w_rnn = w_in[l, :, ATTN_COLS:ATTN_COLS + LRU_WIDTH].astype(BF16)
        w_y = w_in[l, :, ATTN_COLS + LRU_WIDTH:ATTN_COLS + 2 * LRU_WIDTH].astype(BF16)
        w_g = w_in[l, :, ATTN_COLS + 2 * LRU_WIDTH:].astype(BF16)
        logit_bound = (HEAD_DIM * HEAD_DIM ** -0.5 * LOG2_E) * jnp.max(jnp.abs(b_q_gain[l])) * jnp.max(
            jnp.abs(b_k_gain[l]))
        gq = jnp.tile(b_q_gain[l], 2).reshape(1, LANES)
        gk = jnp.tile(b_k_gain[l], 2).reshape(1, LANES)

        qa, ka, va, qb, kb, vb = _inproj_attn(x, sc1, sh1, w_attn, gq, gk, ones, rope_tabs, 512)
        qa_c, ka_c, va_c, qb_c, kb_c, vb_c = _inproj_attn(ctx, csc1, csh1, w_attn, gq, gk, ones, None, m)

        ya = _attn_window(qa, ka, va, ka_c, va_c, a_sink[l])
        cb = c_conv_b[l].reshape(1, d)
        xs = _inproj_rnn(x, sc1, sh1, w_rnn, perm, c_conv_w[l], cb)
        xs_c = _inproj_rnn(ctx, csc1, csh1, w_rnn, perm, c_conv_w[l], cb)
        hs, hs_c, yb_parts = [], [], []
        for di, direction in enumerate(("fwd", "bwd")):
            wg = _gate_weights(c_wr[l, di], c_wi[l, di])
            vecs = [v[l, di].reshape(1, d) for v in (c_br, c_bi, c_lam)]
            h_c, h_fin = _rglru(xs_c, h_zero, wg, *vecs, direction == "bwd")
            scan = (direction, xs, h_fin, wg, *vecs)
            kv_b = [(kb, vb), (kb_c, vb_c)]
            yb_part, h_l = lax.cond(
                logit_bound <= LOGIT_RANGE,
                lambda: _attn_global(qb, kv_b, None, BLOCK, "attn_bounded_" + direction,
                                     part=(di, 2), scan=scan, bounded=True),
                lambda: _attn_global(qb, kv_b, None, BLOCK, "attn_global_" + direction,
                                     lookahead=True, part=(di, 2), scan=scan))
            yb_parts.append(yb_part)
            hs.append(h_l)
            hs_c.append(h_c)
        yb = jnp.concatenate(yb_parts, axis=1)

        merge_w = [perm, perm_t, w_y, w_g] + [w[l].astype(BF16) for w in (w_br_a, w_br_b, w_br_c, w_out)]
        ln1 = [ln1_g[l].reshape(1, d), ln1_b[l].reshape(1, d)]
        ln2 = [ln2_g[l].reshape(1, d), ln2_b[l].reshape(1, d)]
        w1 = w_ff1[l].astype(BF16)
        w2 = w_ff2[l].astype(BF16)

        x = _merge(x, sc1, sh1, g1, ya, yb, hs[0], hs[1], *merge_w, *ln1, 2 * SCAN_TILE)
        x = _mlp(x, sc2, sh2, g2, w1, w2, *ln2, 512)
        if with_ctx:
            ya_c = _attn_global(qa_c, [(ka_c, va_c)], a_sink[l], BLOCK, "attn_ctx_a")
            yb_c = _attn_global(qb_c, [(kb_c, vb_c)], None, BLOCK, "attn_ctx_b")
            ctx = _merge(ctx, csc1, csh1, cg1, ya_c, yb_c, hs_c[0], hs_c[1], *merge_w, *ln1, m)
            ctx = _mlp(ctx, csc2, csh2, cg2, w1, w2, *ln2, m)
    return x
```

```python
import functools

import jax
import jax.numpy as jnp
from jax import lax
from jax.experimental import pallas as pl
from jax.experimental.pallas import tpu as pltpu

D_MODEL = 1024
DEPTH = 2
GRID_W = 64
HEAD_DIM = 64
ROPE_BASE = 10000.0
BLOCK = 128
A_HEADS = 8
A_KV = 2
B_HEADS = 8
B_KV = 2
LRU_WIDTH = D_MODEL
LRU_BLOCKS = 16
LRU_BLOCK_DIM = LRU_WIDTH // LRU_BLOCKS
CONV_W = 4
LRU_C = 8.0
D_FF = 4 * D_MODEL
ALPHA = (2 * DEPTH) ** 0.25
LN_EPS = 1e-5
RMS_EPS = 1e-6
NEG_INF = -1e30
LOG2_E = 1.4426950408889634
Q_COLS = A_HEADS * HEAD_DIM
KV_COLS = A_KV * HEAD_DIM
ATTN_COLS = 2 * (Q_COLS + 2 * KV_COLS)
GROUP = A_HEADS // A_KV

LANES = 128
SUBLANES = 8
V7X_VMEM_BYTES = 64 * 1024 * 1024
VMEM_LIMIT_BYTES = V7X_VMEM_BYTES - 8 * 1024 * 1024

SCAN_TILE = 256
SEG_LEN = SCAN_TILE // SUBLANES
LOGIT_RANGE = 100.0
STATIC_CHUNKS = 4
WINDOW_SUBTILES = 8
SQRT_FLOOR = 1e-30

F32 = jnp.float32
BF16 = jnp.bfloat16


def _params(*semantics):
    return pltpu.CompilerParams(dimension_semantics=semantics, vmem_limit_bytes=VMEM_LIMIT_BYTES)


def _resident(block_shape, index_map):
    return pl.BlockSpec(block_shape, index_map, pipeline_mode=pl.Buffered(1))


def _modulate(x, sc_ref, sh_ref):
    return (x * (1.0 + sc_ref[...]) + sh_ref[...]).astype(BF16)


def _layernorm(y, g, b):
    mu = jnp.mean(y, axis=-1, keepdims=True)
    yc = y - mu
    var = jnp.mean(yc * yc, axis=-1, keepdims=True)
    return yc * lax.rsqrt(var + LN_EPS) * g + b


def _ada_kernel(c_ref, w_ref, b_ref, o_ref):
    c = c_ref[...]
    s = c * jax.nn.sigmoid(c)
    o_ref[...] = jnp.dot(s, w_ref[...], preferred_element_type=F32,
                         precision=lax.Precision.HIGHEST) + b_ref[...]


def _ada(cc, w_ada, b_ada):
    depth, d, cols = w_ada.shape
    tn = cols // 4
    return pl.pallas_call(
        _ada_kernel,
        grid=(depth, cols // tn),
        in_specs=[pl.BlockSpec((SUBLANES, d), lambda l, j: (0, 0)),
                  pl.BlockSpec((None, d, tn), lambda l, j: (l, 0, j)),
                  pl.BlockSpec((None, 1, tn), lambda l, j: (l, 0, j))],
        out_specs=pl.BlockSpec((None, SUBLANES, tn), lambda l, j: (l, 0, j)),
        out_shape=jax.ShapeDtypeStruct((depth, SUBLANES, cols), F32),
        compiler_params=_params("arbitrary", "arbitrary"),
        name="ada_mod",
    )(cc, w_ada, b_ada.reshape(depth, 1, cols))


def _inproj_kernel(*refs, rope):
    refs = list(refs)
    (xp_ref, x_ref, xn_ref, sc_ref, sh_ref, wa_ref, wr_ref, gq_ref, gk_ref, ones_ref, perm_ref,
     cw_ref, cb_ref) = [refs.pop(0) for _ in range(13)]
    if rope:
        cos_ref, sin_ref = refs.pop(0), refs.pop(0)
    qa_ref, ka_ref, va_ref, qb_ref, kb_ref, vb_ref, xs_ref = refs
    i = pl.program_id(1)
    nt = pl.num_programs(1)
    h = _modulate(x_ref[...], sc_ref, sh_ref)
    tm = h.shape[0]

    z = jnp.dot(h, wa_ref[...], preferred_element_type=F32)
    lane = lax.broadcasted_iota(jnp.int32, (tm, LANES), 1)
    first16 = (lane & 31) < 16
    ones = ones_ref[...]
    scale = HEAD_DIM ** -0.5 * LOG2_E

    def rot(zc):
        if not rope:
            return zc
        partner = jnp.where(first16, pltpu.roll(zc, LANES - 16, 1), pltpu.roll(zc, 16, 1))
        return zc * cos_ref[...] + partner * sin_ref[...]

    def rms(zc, g_ref):
        sq = zc * zc
        hi = sq.astype(BF16)
        lo = (sq - hi.astype(F32)).astype(BF16)
        ss = (jnp.dot(hi, ones, preferred_element_type=F32)
              + jnp.dot(lo, ones, preferred_element_type=F32))
        return zc * lax.rsqrt(ss * (1.0 / HEAD_DIM) + RMS_EPS) * g_ref[...]

    def col(start):
        return z[:, start:start + LANES]

    for c in range(Q_COLS // LANES):
        qa_ref[:, c * LANES:(c + 1) * LANES] = (rot(col(c * LANES)) * scale).astype(BF16)
    base = Q_COLS
    ka_ref[...] = rot(col(base)).astype(BF16)
    va_t = col(base + KV_COLS).T.astype(BF16)
    for c in range(va_ref.shape[0]):
        va_ref[c] = va_t[:, c * BLOCK:(c + 1) * BLOCK]
    base = Q_COLS + 2 * KV_COLS
    for c in range(Q_COLS // LANES):
        qb_ref[:, c * LANES:(c + 1) * LANES] = (
            rot(rms(col(base + c * LANES), gq_ref)) * scale).astype(BF16)
    base += Q_COLS
    kb_ref[...] = rot(rms(col(base), gk_ref)).astype(BF16)
    vb_ref[0] = col(base + KV_COLS).T.astype(BF16)

    ntile = tm // SCAN_TILE
    hp = [jnp.dot(perm_ref[...], h[k * SCAN_TILE:(k + 1) * SCAN_TILE], preferred_element_type=F32).astype(BF16)
          for k in range(ntile)]
    halo = _modulate(jnp.concatenate([xp_ref[...], xn_ref[...]], axis=0), sc_ref, sh_ref)
    zall = jnp.dot(jnp.concatenate(hp + [halo], axis=0), wr_ref[...], preferred_element_type=F32)
    zs = [zall[k * SCAN_TILE:(k + 1) * SCAN_TILE] for k in range(ntile)]
    edge_prev = jnp.where(i > 0, zall[tm:tm + SUBLANES], 0.0)
    edge_next = jnp.where(i < nt - 1, zall[tm + SUBLANES:], 0.0)
    sub = lax.broadcasted_iota(jnp.int32, (SUBLANES, zall.shape[1]), 0)
    last_row = SCAN_TILE - 1
    for k, zk in enumerate(zs):
        tok_before = edge_prev[SUBLANES - 1:SUBLANES] if k == 0 else zs[k - 1][last_row:last_row + 1]
        tok_after1 = edge_next[0:1] if k == ntile - 1 else zs[k + 1][0:1]
        tok_after2 = edge_next[1:2] if k == ntile - 1 else zs[k + 1][SUBLANES:SUBLANES + 1]
        first, second, last = zk[0:SUBLANES], zk[SUBLANES:2 * SUBLANES], zk[SCAN_TILE - SUBLANES:]
        before = jnp.where(sub == 0, tok_before, pltpu.roll(last, 1, 0))
        after1 = jnp.where(sub == SUBLANES - 1, tok_after1, pltpu.roll(first, SUBLANES - 1, 0))
        after2 = jnp.where(sub == SUBLANES - 1, tok_after2, pltpu.roll(second, SUBLANES - 1, 0))
        taps = [jnp.concatenate([before, zk[:SCAN_TILE - SUBLANES]], axis=0),
                zk,
                jnp.concatenate([zk[SUBLANES:], after1], axis=0),
                jnp.concatenate([zk[2 * SUBLANES:], after1, after2], axis=0)]
        out = cb_ref[...]
        for tap in range(CONV_W):
            out = out + cw_ref[tap:tap + 1, :] * taps[tap]
        xs_ref[k * SCAN_TILE:(k + 1) * SCAN_TILE, :] = out


def _inproj(x, sc, sh, w_attn, w_rnn, gq, gk, ones, perm, cw, cb, rope_tabs, tm):
    bsz, rows, d = x.shape
    rope = rope_tabs is not None
    hb = tm // SUBLANES
    mod_spec = pl.BlockSpec((None, 1, d), lambda b, i: (b, 0, 0))
    vec_spec = _resident((1, LANES), lambda b, i: (0, 0))
    rnn_cols = w_rnn.shape[1]
    in_specs = [pl.BlockSpec((None, SUBLANES, d), lambda b, i: (b, jnp.maximum(i * hb - 1, 0), 0)),
                pl.BlockSpec((None, tm, d), lambda b, i: (b, i, 0)),
                pl.BlockSpec((None, SUBLANES, d),
                             lambda b, i: (b, jnp.minimum((i + 1) * hb, rows // SUBLANES - 1), 0)),
                mod_spec, mod_spec,
                _resident((d, ATTN_COLS), lambda b, i: (0, 0)), _resident((d, rnn_cols), lambda b, i: (0, 0)),
                vec_spec, vec_spec, _resident((LANES, LANES), lambda b, i: (0, 0)),
                _resident((SCAN_TILE, SCAN_TILE), lambda b, i: (0, 0)),
                _resident((CONV_W, rnn_cols), lambda b, i: (0, 0)), _resident((1, rnn_cols), lambda b, i: (0, 0))]
    args = [x, x, x, sc, sh, w_attn, w_rnn, gq, gk, ones, perm, cw, cb]
    if rope:
        tab_spec = pl.BlockSpec((tm, LANES), lambda b, i: (i, 0))
        in_specs += [tab_spec, tab_spec]
        args += list(rope_tabs)

    def out(cols, dtype=BF16):
        return (pl.BlockSpec((None, tm, cols), lambda b, i: (b, i, 0)),
                jax.ShapeDtypeStruct((bsz, rows, cols), dtype))

    def out_t(slab):
        return (pl.BlockSpec((None, tm // slab, KV_COLS, slab), lambda b, i: (b, i, 0, 0)),
                jax.ShapeDtypeStruct((bsz, rows // slab, KV_COLS, slab), BF16))

    outs = [out(Q_COLS), out(KV_COLS), out_t(BLOCK), out(Q_COLS), out(KV_COLS), out_t(tm), out(rnn_cols, F32)]
    return pl.pallas_call(
        functools.partial(_inproj_kernel, rope=rope),
        grid=(bsz, rows // tm),
        in_specs=in_specs,
        out_specs=[o[0] for o in outs],
        out_shape=[o[1] for o in outs],
        compiler_params=_params("arbitrary", "arbitrary"),
        name="inproj_rope" if rope else "inproj_ctx",
    )(*args)


def _attn_kernel(*refs, segs, n_src, has_sink, tq, nsub, lookahead, scan):
    refs = list(refs)
    sink_ref = refs.pop(0) if has_sink else None
    q_ref = refs.pop(0)
    qn_ref = refs.pop(0) if lookahead else None
    kv_refs = [(refs.pop(0), refs.pop(0)) for _ in range(n_src)]
    if scan:
        xs_ref, wg_ref, br_ref, bi_ref, lam_ref, h0_ref = [refs.pop(0) for _ in range(6)]
    o_ref = refs.pop(0)
    if scan:
        h_ref = refs.pop(0)
        u_scr, a_scr, nsp_scr, carry_scr = [refs.pop() for _ in range(4)]
    i = pl.program_id(1)
    nblk = pl.num_programs(1) * nsub
    width = GROUP * tq

    def sink_row(kvi):
        return jnp.concatenate(
            [jnp.full((1, tq), sink_ref[GROUP * kvi + hh] * LOG2_E, F32) for hh in range(GROUP)], axis=1)

    def query_operand(q, kvi):
        qt = q.T
        heads = jnp.concatenate(
            [qt[(GROUP * kvi + hh) * HEAD_DIM:(GROUP * kvi + hh + 1) * HEAD_DIM, :] for hh in range(GROUP)],
            axis=1)
        parts = [jnp.zeros((HEAD_DIM, width), BF16)] * A_KV
        parts[kvi] = heads
        return jnp.concatenate(parts, axis=0)

    def run_tile(sub):
        mine = refs[sub * 4 * A_KV:(sub + 1) * 4 * A_KV]
        s_scr, m_scr, l_scr, acc_scr = [mine[g * A_KV:(g + 1) * A_KV] for g in range(4)]
        pos = i * nsub + sub
        rows = slice(sub * tq, (sub + 1) * tq)

        def seg_mask(where):
            if where not in ("prev", "next"):
                return None
            j = lax.broadcasted_iota(jnp.int32, (BLOCK, width), 0)
            r = lax.broadcasted_iota(jnp.int32, (BLOCK, width), 1) & (tq - 1)
            far = jnp.int32(4 * BLOCK)
            if where == "prev":
                return (j - r) >= jnp.where(pos > 0, 0, far)
            return (r - j) >= jnp.where(pos < nblk - 1, 0, far)

        def block_of(where):
            return {"cur": pos, "prev": jnp.maximum(pos - 1, 0), "next": jnp.minimum(pos + 1, nblk - 1)}[where]

        def for_each_chunk(*fns, along=None):
            base = 0
            for src, where, nchunks, kc in segs:
                k_ref, vt_ref = kv_refs[src]
                if where is None:
                    row0, slab0 = 0, 0
                else:
                    slab0 = block_of(where)
                    row0 = pl.multiple_of(slab0 * BLOCK, BLOCK)
                if nchunks <= STATIC_CHUNKS:
                    for ci in range(nchunks):
                        for fn in fns:
                            fn(k_ref, vt_ref, row0 + ci * kc, slab0 + ci, base + ci * kc, kc, where)
                else:
                    def body(ci, carry, k_ref=k_ref, vt_ref=vt_ref, kc=kc, base=base):
                        r0 = pl.multiple_of(ci * kc, kc)
                        for fn in fns:
                            fn(k_ref, vt_ref, r0, ci, base + r0, kc, None)
                        if along is not None:
                            along(ci)
                        return carry
                    lax.fori_loop(0, nchunks, body, 0, unroll=4)
                base += nchunks * kc

        def start_scores(kvi):
            m_scr[kvi][...] = sink_row(kvi) if has_sink else jnp.full((1, width), NEG_INF, F32)

        def scores(kvi, qz, k_ref, vt_ref, r0, slab, srow, kc, where):
            s = jnp.dot(k_ref[pl.ds(r0, kc), :], qz, preferred_element_type=F32)
            mask = seg_mask(where)
            if mask is not None:
                s = jnp.where(mask, s, NEG_INF)
            s_scr[kvi][pl.ds(srow, kc), :] = s
            m_scr[kvi][...] = jnp.maximum(m_scr[kvi][...], jnp.max(s, axis=0, keepdims=True))

        def start_values(kvi):
            if has_sink:
                l_scr[kvi][...] = jnp.exp2(sink_row(kvi) - m_scr[kvi][...])
            else:
                l_scr[kvi][...] = jnp.zeros((1, width), F32)
            acc_scr[kvi][...] = jnp.zeros((HEAD_DIM, width), F32)

        def values(kvi, k_ref, vt_ref, r0, slab, srow, kc, where):
            p = jnp.exp2(s_scr[kvi][pl.ds(srow, kc), :] - m_scr[kvi][...])
            l_scr[kvi][...] += jnp.sum(p, axis=0, keepdims=True)
            acc_scr[kvi][...] += jnp.dot(vt_ref[slab, kvi * HEAD_DIM:(kvi + 1) * HEAD_DIM, :], p.astype(BF16),
                                         preferred_element_type=F32)

        def first_scores():
            start_scores(0)
            for_each_chunk(functools.partial(scores, 0, query_operand(q_ref[rows, :], 0)))

        def phase_a():
            if lookahead:
                pl.when(i == 0)(first_scores)
            else:
                first_scores()

        half = SCAN_TILE // 2

        def scan_first_half(c):
            _rglru_gates(xs_ref, wg_ref, br_ref, bi_ref, nsp_scr, a_scr, u_scr, c, slice(0, half))

        def scan_second_half(c):
            _rglru_gates(xs_ref, wg_ref, br_ref, bi_ref, nsp_scr, a_scr, u_scr, c, slice(half, SCAN_TILE))
            _rglru_scan(a_scr, u_scr, h_ref, carry_scr, c, scan == "bwd")

        def phase_b():
            start_scores(1)
            start_values(0)
            for_each_chunk(functools.partial(scores, 1, query_operand(q_ref[rows, :], 1)),
                           functools.partial(values, 0), along=scan_first_half if scan else None)

        def phase_c():
            start_values(1)
            if lookahead:
                start_scores(0)
                for_each_chunk(functools.partial(values, 1),
                               functools.partial(scores, 0, query_operand(qn_ref[...], 0)),
                               along=scan_second_half if scan else None)
            else:
                for_each_chunk(functools.partial(values, 1))

        def finish():
            outs = []
            for kvi in range(A_KV):
                o = acc_scr[kvi][...] / l_scr[kvi][...]
                outs += [o[:, hh * tq:(hh + 1) * tq] for hh in range(GROUP)]
            o_ref[rows, :] = jnp.concatenate(outs, axis=0).T.astype(BF16)

        return phase_a, phase_b, phase_c, finish

    if scan:
        nsp_scr[...] = _neg_softplus_neg(lam_ref[...])

        @pl.when(i == 0)
        def _():
            carry_scr[...] = h0_ref[...]

    for phases in zip(*[run_tile(sub) for sub in range(nsub)]):
        for phase in phases:
            phase()


def _attn_bounded_kernel(*refs, segs, n_src, tq, scan):
    refs = list(refs)
    q_ref = refs.pop(0)
    kv_refs = [(refs.pop(0), refs.pop(0)) for _ in range(n_src)]
    if scan:
        xs_ref, wg_ref, br_ref, bi_ref, lam_ref, h0_ref = [refs.pop(0) for _ in range(6)]
    o_ref = refs.pop(0)
    if scan:
        h_ref = refs.pop(0)
        carry_scr, nsp_scr = refs.pop(), refs.pop()
        gate_scr = [(refs.pop(), refs.pop()) for _ in range(2)]
    l_scr, acc_scr = refs[:A_KV], refs[A_KV:2 * A_KV]
    s_scr = [refs[2 * A_KV:3 * A_KV], refs[3 * A_KV:4 * A_KV]]
    i = pl.program_id(1)
    width = GROUP * tq

    if scan:
        nsp_scr[...] = _neg_softplus_neg(lam_ref[...])

        @pl.when(i == 0)
        def _():
            carry_scr[...] = h0_ref[...]

    qt = q_ref[...].T
    qz = []
    for kvi in range(A_KV):
        heads = jnp.concatenate(
            [qt[(GROUP * kvi + hh) * HEAD_DIM:(GROUP * kvi + hh + 1) * HEAD_DIM, :] for hh in range(GROUP)],
            axis=1)
        parts = [jnp.zeros((HEAD_DIM, width), BF16)] * A_KV
        parts[kvi] = heads
        qz.append(jnp.concatenate(parts, axis=0))
        l_scr[kvi][...] = jnp.zeros((1, width), F32)
        acc_scr[kvi][...] = jnp.zeros((HEAD_DIM, width), F32)

    def scores(k_ref, r0, kc, buf):
        k = k_ref[pl.ds(r0, kc), :]
        for kvi in range(A_KV):
            buf[kvi][0:kc, :] = jnp.dot(k, qz[kvi], preferred_element_type=F32)

    def values(vt_ref, slab, kc, buf):
        for kvi in range(A_KV):
            p = jnp.exp2(buf[kvi][0:kc, :])
            l_scr[kvi][...] += jnp.sum(p, axis=0, keepdims=True)
            acc_scr[kvi][...] += jnp.dot(vt_ref[slab, kvi * HEAD_DIM:(kvi + 1) * HEAD_DIM, :], p.astype(BF16),
                                         preferred_element_type=F32)

    def gates(c, buf):
        _rglru_gates(xs_ref, wg_ref, br_ref, bi_ref, nsp_scr, buf[0], buf[1], c, slice(None), to_chunk=True)

    def scan_chunk(c, buf):
        _rglru_scan(buf[0], buf[1], h_ref, carry_scr, c, scan == "bwd", from_chunk=True)

    (src0, _, nchunks, kc), rest = segs[0], segs[1:]
    k_ref, vt_ref = kv_refs[src0]
    assert nchunks % 2 == 0
    last = nchunks - 1
    scores(k_ref, 0, kc, s_scr[0])
    if scan:
        gates(0, gate_scr[0])

    def body(j, carry):
        even, odd = 2 * j, 2 * j + 1
        nxt = jnp.minimum(even + 2, last)
        scores(k_ref, pl.multiple_of(odd * kc, kc), kc, s_scr[1])
        values(vt_ref, even, kc, s_scr[0])
        if scan:
            gates(odd, gate_scr[1])
            scan_chunk(even, gate_scr[0])
        scores(k_ref, pl.multiple_of(nxt * kc, kc), kc, s_scr[0])
        values(vt_ref, odd, kc, s_scr[1])
        if scan:
            gates(nxt, gate_scr[0])
            scan_chunk(odd, gate_scr[1])
        return carry

    lax.fori_loop(0, nchunks // 2, body, 0, unroll=True)

    for src, where, nch, kcs in rest:
        assert where is None and nch <= STATIC_CHUNKS
        kr, vr = kv_refs[src]
        for ci in range(nch):
            scores(kr, ci * kcs, kcs, s_scr[ci % 2])
            values(vr, ci, kcs, s_scr[ci % 2])

    outs = []
    for kvi in range(A_KV):
        o = acc_scr[kvi][...] / l_scr[kvi][...]
        outs += [o[:, hh * tq:(hh + 1) * tq] for hh in range(GROUP)]
    o_ref[...] = jnp.concatenate(outs, axis=0).T.astype(BF16)


def _attention(q, kv_arrays, segs, sink, tq, nsub, name, lookahead=False, part=(0, 1), scan=None,
               bounded=False):
    assert not (lookahead and nsub != 1) and not (bounded and (lookahead or nsub != 1 or sink is not None))
    bsz, rows, _ = q.shape
    nb = rows // (tq * nsub * part[1])
    first = part[0] * nb
    has_sink = sink is not None
    in_specs, args = [], []
    if has_sink:
        in_specs.append(pl.BlockSpec(memory_space=pltpu.SMEM))
        args.append(sink)
    in_specs.append(pl.BlockSpec((None, tq * nsub, Q_COLS), lambda b, i: (b, first + i, 0)))
    args.append(q)
    if lookahead:
        in_specs.append(pl.BlockSpec((None, tq, Q_COLS), lambda b, i: (b, first + jnp.minimum(i + 1, nb - 1), 0)))
        args.append(q)
    for k, vt in kv_arrays:
        in_specs += [pl.BlockSpec((None,) + k.shape[1:], lambda b, i: (b, 0, 0)),
                     pl.BlockSpec((None,) + vt.shape[1:], lambda b, i: (b, 0, 0, 0))]
        args += [k, vt]
    width = GROUP * tq
    nk_total = sum(nchunks * kc for _, _, nchunks, kc in segs)
    per_head = [(nk_total, width), (1, width), (1, width), (HEAD_DIM, width)]
    if bounded:
        kc0 = segs[0][3]
        per_head = [(1, width), (HEAD_DIM, width), (kc0, width), (kc0, width)]
    out_specs = [pl.BlockSpec((None, tq * nsub, Q_COLS), lambda b, i: (b, i, 0))]
    out_shape = [jax.ShapeDtypeStruct((bsz, rows // part[1], Q_COLS), BF16)]
    scratch = [pltpu.VMEM(shape, F32) for _ in range(nsub) for shape in per_head for _ in range(A_KV)]
    direction = None
    if scan is not None:
        direction, xs, h0, wg, br, bi, lam = scan
        srows, swidth = xs.shape[1:]
        assert srows // SCAN_TILE == nb and swidth // LANES == segs[0][2]
        tile = pl.BlockSpec((None, SCAN_TILE, swidth),
                            (lambda b, i: (b, nb - 1 - i, 0)) if direction == "bwd" else (lambda b, i: (b, i, 0)))
        vec = _resident((1, swidth), lambda b, i: (0, 0))
        in_specs += [tile, _resident(wg.shape, lambda b, i: (0, 0, 0)), vec, vec, vec,
                     pl.BlockSpec((None, 1, swidth), lambda b, i: (b, 0, 0))]
        args += [xs, wg, br, bi, lam, h0]
        out_specs.append(tile)
        out_shape.append(jax.ShapeDtypeStruct(xs.shape, F32))
        if bounded:
            scratch += [pltpu.VMEM((SCAN_TILE, LANES), F32) for _ in range(4)]
            scratch += [pltpu.VMEM((1, swidth), F32), pltpu.VMEM((1, swidth), F32)]
        else:
            scratch += [pltpu.VMEM((1, swidth), F32), pltpu.VMEM((1, swidth), F32),
                        pltpu.VMEM((SCAN_TILE, swidth), F32), pltpu.VMEM((SCAN_TILE, swidth), F32)]
    if bounded:
        body = functools.partial(_attn_bounded_kernel, segs=tuple(segs), n_src=len(kv_arrays), tq=tq, scan=direction)
    else:
        body = functools.partial(_attn_kernel, segs=tuple(segs), n_src=len(kv_arrays), has_sink=has_sink, tq=tq,
                                 nsub=nsub, lookahead=lookahead, scan=direction)
    outs = pl.pallas_call(
        body,
        grid=(bsz, nb),
        in_specs=in_specs,
        out_specs=out_specs,
        out_shape=out_shape,
        scratch_shapes=scratch,
        compiler_params=_params("arbitrary", "arbitrary"),
        name=name,
    )(*args)
    return outs if scan is not None else outs[0]


def _whole_seg(src, vt):
    return (src, None, vt.shape[1], vt.shape[3])


def _attn_window(q, k, vt, k_ctx, vt_ctx, sink):
    assert vt.shape[3] == BLOCK
    segs = [(0, "cur", 1, BLOCK), (0, "prev", 1, BLOCK), (0, "next", 1, BLOCK), _whole_seg(1, vt_ctx)]
    return _attention(q, [(k, vt), (k_ctx, vt_ctx)], segs, sink, BLOCK, WINDOW_SUBTILES, "attn_window")


def _attn_global(q, kv_list, sink, tq, name, lookahead=False, part=(0, 1), scan=None, bounded=False):
    segs = [_whole_seg(src, vt) for src, (_, vt) in enumerate(kv_list)]
    return _attention(q, kv_list, segs, sink, tq, 1, name, lookahead, part, scan, bounded)


def _lane_chunk(c):
    if isinstance(c, int):
        return slice(c * LANES, (c + 1) * LANES)
    return pl.ds(pl.multiple_of(c * LANES, LANES), LANES)


def _neg_softplus_neg(lam):
    return -(jnp.maximum(-lam, 0.0) + jnp.log1p(jnp.exp(-jnp.abs(lam))))


def _rglru_gates(x_ref, wg_ref, br_ref, bi_ref, nsp_ref, a_scr, u_scr, c, rows, to_chunk=False):
    cs = _lane_chunk(c)
    dst = slice(None) if to_chunk else cs
    xcc = x_ref[rows, cs]
    g = jnp.dot(xcc.astype(BF16), wg_ref[c], preferred_element_type=F32)
    r = jax.nn.sigmoid(g[:, :LANES] + br_ref[:, cs])
    ig = jax.nn.sigmoid(g[:, LANES:] + bi_ref[:, cs])
    log_a = (LRU_C * r) * nsp_ref[:, cs]
    a = jnp.exp(log_a)
    w = -jnp.tanh(log_a) * (a * a + 1.0)
    a_scr[rows, dst] = a
    u_scr[rows, dst] = (w * lax.rsqrt(jnp.maximum(w, SQRT_FLOOR))) * (ig * xcc)


def _rglru_scan(a_scr, u_scr, h_ref, carry_scr, c, reverse, from_chunk=False):
    cs = _lane_chunk(c)
    src = slice(None) if from_chunk else cs
    ngroups = a_scr.shape[0] // SUBLANES
    sub = lax.broadcasted_iota(jnp.int32, (SUBLANES, LANES), 0)
    order = range(ngroups - 1, -1, -1) if reverse else range(ngroups)

    def rows(g):
        return slice(g * SUBLANES, (g + 1) * SUBLANES)

    h_loc = decay = None
    for gi in order:
        a_g, u_g = a_scr[rows(gi), src], u_scr[rows(gi), src]
        h_loc = u_g if h_loc is None else a_g * h_loc + u_g
        decay = a_g if decay is None else a_g * decay
        h_ref[rows(gi), cs] = h_loc
        a_scr[rows(gi), src] = decay

    end, dec = h_loc, decay
    for s in (1, 2, 4):
        if reverse:
            ok = sub < SUBLANES - s
            shift = SUBLANES - s
        else:
            ok = sub >= s
            shift = s
        end = end + dec * jnp.where(ok, pltpu.roll(end, shift, 0), 0.0)
        dec = dec * jnp.where(ok, pltpu.roll(dec, shift, 0), 1.0)
    carry = carry_scr[:, cs]
    seg_end = end + dec * carry
    if reverse:
        start = jnp.where(sub == SUBLANES - 1, carry, pltpu.roll(seg_end, SUBLANES - 1, 0))
        carry_scr[:, cs] = seg_end[0:1]
    else:
        start = jnp.where(sub == 0, carry, pltpu.roll(seg_end, 1, 0))
        carry_scr[:, cs] = seg_end[SUBLANES - 1:SUBLANES]
    for gi in order:
        h_ref[rows(gi), cs] = h_ref[rows(gi), cs] + a_scr[rows(gi), src] * start


def _rglru_kernel(x_ref, wg_ref, br_ref, bi_ref, lam_ref, h0_ref,
                  h_ref, hfin_ref, carry_scr, nsp_scr, a_scr, u_scr, *, reverse):
    j = pl.program_id(1)
    nt = pl.num_programs(1)

    @pl.when(j == 0)
    def _():
        carry_scr[...] = h0_ref[...]

    nsp_scr[...] = _neg_softplus_neg(lam_ref[...])

    def chunk(c, carry):
        _rglru_gates(x_ref, wg_ref, br_ref, bi_ref, nsp_scr, a_scr, u_scr, c, slice(None))
        _rglru_scan(a_scr, u_scr, h_ref, carry_scr, c, reverse)
        return carry

    lax.fori_loop(0, x_ref.shape[1] // LANES, chunk, 0)

    @pl.when(j == nt - 1)
    def _():
        hfin_ref[...] = carry_scr[...]


def _rglru(xs, h0, wg, br, bi, lam, reverse):
    bsz, rows, width = xs.shape
    nt = rows // SCAN_TILE

    def tpos(j):
        return nt - 1 - j if reverse else j

    vec = _resident((1, width), lambda b, j: (0, 0))
    tile = pl.BlockSpec((None, SCAN_TILE, width), lambda b, j: (b, tpos(j), 0))
    state = pl.BlockSpec((None, 1, width), lambda b, j: (b, 0, 0))
    return pl.pallas_call(
        functools.partial(_rglru_kernel, reverse=reverse),
        grid=(bsz, nt),
        in_specs=[tile, _resident(wg.shape, lambda b, j: (0, 0, 0)), vec, vec, vec, state],
        out_specs=[tile, state],
        out_shape=[jax.ShapeDtypeStruct((bsz, rows, width), F32),
                   jax.ShapeDtypeStruct((bsz, 1, width), F32)],
        scratch_shapes=[pltpu.VMEM((1, width), F32), pltpu.VMEM((1, width), F32),
                        pltpu.VMEM((SCAN_TILE, width), F32), pltpu.VMEM((SCAN_TILE, width), F32)],
        compiler_params=_params("arbitrary", "arbitrary"),
        name="rglru_bwd" if reverse else "rglru_fwd",
    )(xs, wg, br, bi, lam, h0)


def _merge_kernel(x_ref, sc_ref, sh_ref, gate_ref, ya_ref, yb_ref, hf_ref, hb_ref, perm_ref, perm_t_ref,
                  wy_ref, wg_ref, wa_ref, wb_ref, wc_ref, wo_ref, lng_ref, lnb_ref, o_ref):
    x = x_ref[...]
    d = x.shape[1]
    h = _modulate(x, sc_ref, sh_ref)
    def per_scan_tile(p_ref, rows):
        return jnp.concatenate(
            [jnp.dot(p_ref[...], rows[k:k + SCAN_TILE], preferred_element_type=F32).astype(BF16)
             for k in range(0, rows.shape[0], SCAN_TILE)], axis=0)

    y_rnn = jnp.dot(per_scan_tile(perm_ref, h), wy_ref[...], preferred_element_type=F32)
    yc = per_scan_tile(perm_t_ref, ((hf_ref[...] + hb_ref[...]) * jax.nn.gelu(y_rnn)).astype(BF16))
    zz = jnp.dot(h, wg_ref[...], preferred_element_type=F32)
    mix = (jax.nn.sigmoid(zz[:, 0:d]) * jnp.dot(ya_ref[...], wa_ref[...], preferred_element_type=F32)
           + jax.nn.sigmoid(zz[:, d:2 * d]) * jnp.dot(yb_ref[...], wb_ref[...], preferred_element_type=F32)
           + jax.nn.sigmoid(zz[:, 2 * d:3 * d]) * jnp.dot(yc, wc_ref[...], preferred_element_type=F32))
    o = jnp.dot(mix.astype(BF16), wo_ref[...], preferred_element_type=F32)
    o_ref[...] = _layernorm(ALPHA * x + gate_ref[...] * o, lng_ref[...], lnb_ref[...])


def _merge(x, sc, sh, gate, ya, yb, hf, hb, perm, perm_t, wy, wg, wa, wb, wc, wo, lng, lnb, tm):
    bsz, rows, d = x.shape
    mod_spec = pl.BlockSpec((None, 1, d), lambda b, i: (b, 0, 0))
    vec = _resident((1, d), lambda b, i: (0, 0))

    def tile(cols):
        return pl.BlockSpec((None, tm, cols), lambda b, i: (b, i, 0))

    def weight(w):
        return _resident(w.shape, lambda b, i: (0, 0))

    weights = [perm, perm_t, wy, wg, wa, wb, wc, wo]
    return pl.pallas_call(
        _merge_kernel,
        grid=(bsz, rows // tm),
        in_specs=[tile(d), mod_spec, mod_spec, mod_spec, tile(Q_COLS), tile(Q_COLS), tile(d), tile(d)]
                 + [weight(w) for w in weights] + [vec, vec],
        out_specs=tile(d),
        out_shape=jax.ShapeDtypeStruct((bsz, rows, d), F32),
        compiler_params=_params("arbitrary", "arbitrary"),
        name="merge_ln",
    )(x, sc, sh, gate, ya, yb, hf, hb, *weights, lng, lnb)


def _mlp_kernel(x_ref, sc_ref, sh_ref, gate_ref, w1_ref, w2_ref, lng_ref, lnb_ref, o_ref):
    x = x_ref[...]
    h = _modulate(x, sc_ref, sh_ref)
    u = jnp.dot(h, w1_ref[...], preferred_element_type=F32)
    u = jnp.square(jnp.maximum(u, 0.0)).astype(BF16)
    o = jnp.dot(u, w2_ref[...], preferred_element_type=F32)
    o_ref[...] = _layernorm(ALPHA * x + gate_ref[...] * o, lng_ref[...], lnb_ref[...])


def _mlp(x, sc, sh, gate, w1, w2, lng, lnb, tm):
    bsz, rows, d = x.shape
    mod_spec = pl.BlockSpec((None, 1, d), lambda b, i: (b, 0, 0))
    vec = _resident((1, d), lambda b, i: (0, 0))
    tile = pl.BlockSpec((None, tm, d), lambda b, i: (b, i, 0))
    return pl.pallas_call(
        _mlp_kernel,
        grid=(bsz, rows // tm),
        in_specs=[tile, mod_spec, mod_spec, mod_spec,
                  _resident(w1.shape, lambda b, i: (0, 0)), _resident(w2.shape, lambda b, i: (0, 0)), vec, vec],
        out_specs=tile,
        out_shape=jax.ShapeDtypeStruct((bsz, rows, d), F32),
        compiler_params=_params("arbitrary", "arbitrary"),
        name="mlp_ln",
    )(x, sc, sh, gate, w1, w2, lng, lnb)


def _rope_tables(n):
    pos = jnp.arange(n)
    row = (pos // GRID_W).astype(F32)
    colp = (pos % GRID_W).astype(F32)
    nf = HEAD_DIM // 4
    inv = ROPE_BASE ** (-jnp.arange(nf, dtype=F32) / nf)
    ang_r = row[:, None] * inv
    ang_c = colp[:, None] * inv
    cos_t = jnp.concatenate([jnp.cos(ang_r), jnp.cos(ang_r), jnp.cos(ang_c), jnp.cos(ang_c)], axis=-1)
    sin_t = jnp.concatenate([-jnp.sin(ang_r), jnp.sin(ang_r), -jnp.sin(ang_c), jnp.sin(ang_c)], axis=-1)
    return jnp.tile(cos_t, (1, 2)), jnp.tile(sin_t, (1, 2))


def _gate_weights(w_r, w_i):
    def pair(w):
        w = w.reshape(LRU_BLOCKS // 2, 2, LRU_BLOCK_DIM, LRU_BLOCK_DIM)
        z = jnp.zeros_like(w[:, 0])
        top = jnp.concatenate([w[:, 0], z], axis=-1)
        bot = jnp.concatenate([z, w[:, 1]], axis=-1)
        return jnp.concatenate([top, bot], axis=-2)
    return jnp.concatenate([pair(w_r), pair(w_i)], axis=-1).astype(BF16)


def kernel(x, c, ctx, c_ctx, w_ada, b_ada, w_in, a_sink, b_q_gain, b_k_gain, c_conv_w, c_conv_b,
           c_wr, c_br, c_wi, c_bi, c_lam, w_br_a, w_br_b, w_br_c, w_out, ln1_g, ln1_b,
           w_ff1, w_ff2, ln2_g, ln2_b):
    bsz, n, d = x.shape
    m = ctx.shape[1]
    depth = w_ada.shape[0]

    cc = jnp.zeros((SUBLANES, d), F32).at[:bsz].set(c).at[bsz].set(c_ctx)
    mod = _ada(cc, w_ada, b_ada)
    rope_tabs = _rope_tables(n)
    blk = jnp.arange(LANES) // HEAD_DIM
    ones = (blk[:, None] == blk[None, :]).astype(BF16)
    h_zero = jnp.zeros((bsz, 1, d), F32)
    slot = jnp.arange(SCAN_TILE)
    token_of_slot = SEG_LEN * (slot % SUBLANES) + slot // SUBLANES
    perm = (token_of_slot[:, None] == slot[None, :]).astype(BF16)
    perm_t = perm.T

    for l in range(depth):
        with_ctx = l < depth - 1
        mod_lat = mod[l, :bsz].reshape(bsz, 1, 6 * d)
        mod_ctx = jnp.broadcast_to(mod[l, bsz].reshape(1, 1, 6 * d), (bsz, 1, 6 * d))
        sh1, sc1, g1, sh2, sc2, g2 = [mod_lat[:, :, k * d:(k + 1) * d] for k in range(6)]
        csh1, csc1, cg1, csh2, csc2, cg2 = [mod_ctx[:, :, k * d:(k + 1) * d] for k in range(6)]

        w_attn = w_in[l, :, :ATTN_COLS].astype(BF16)
        w_rnn = w_in[l, :, ATTN_COLS:ATTN_COLS + LRU_WIDTH].astype(BF16)
        w_y = w_in[l, :, ATTN_COLS + LRU_WIDTH:ATTN_COLS + 2 * LRU_WIDTH].astype(BF16)
        w_g = w_in[l, :, ATTN_COLS + 2 * LRU_WIDTH:].astype(BF16)
        logit_bound = (HEAD_DIM * HEAD_DIM ** -0.5 * LOG2_E) * jnp.max(jnp.abs(b_q_gain[l])) * jnp.max(
            jnp.abs(b_k_gain[l]))
        gq = jnp.tile(b_q_gain[l], 2).reshape(1, LANES)
        gk = jnp.tile(b_k_gain[l], 2).reshape(1, LANES)

        cb = c_conv_b[l].reshape(1, d)
        conv = (perm, c_conv_w[l], cb)
        qa, ka, va, qb, kb, vb, xs = _inproj(x, sc1, sh1, w_attn, w_rnn, gq, gk, ones, *conv, rope_tabs, 512)
        qa_c, ka_c, va_c, qb_c, kb_c, vb_c, xs_c = _inproj(ctx, csc1, csh1, w_attn, w_rnn, gq, gk, ones, *conv,
                                                           None, m)

        ya = _attn_window(qa, ka, va, ka_c, va_c, a_sink[l])
        hs, hs_c, yb_parts = [], [], []
        for di, direction in enumerate(("fwd", "bwd")):
            wg = _gate_weights(c_wr[l, di], c_wi[l, di])
            vecs = [v[l, di].reshape(1, d) for v in (c_br, c_bi, c_lam)]
            h_c, h_fin = _rglru(xs_c, h_zero, wg, *vecs, direction == "bwd")
            scan = (direction, xs, h_fin, wg, *vecs)
            kv_b = [(kb, vb), (kb_c, vb_c)]
            yb_part, h_l = lax.cond(
                logit_bound <= LOGIT_RANGE,
                lambda: _attn_global(qb, kv_b, None, BLOCK, "attn_bounded_" + direction,
                                     part=(di, 2), scan=scan, bounded=True),
                lambda: _attn_global(qb, kv_b, None, BLOCK, "attn_global_" + direction,
                                     lookahead=True, part=(di, 2), scan=scan))
            yb_parts.append(yb_part)
            hs.append(h_l)
            hs_c.append(h_c)
        yb = jnp.concatenate(yb_parts, axis=1)

        merge_w = [perm, perm_t, w_y, w_g] + [w[l].astype(BF16) for w in (w_br_a, w_br_b, w_br_c, w_out)]
        ln1 = [ln1_g[l].reshape(1, d), ln1_b[l].reshape(1, d)]
        ln2 = [ln2_g[l].reshape(1, d), ln2_b[l].reshape(1, d)]
        w1 = w_ff1[l].astype(BF16)
        w2 = w_ff2[l].astype(BF16)

        x = _merge(x, sc1, sh1, g1, ya, yb, hs[0], hs[1], *merge_w, *ln1, 2 * SCAN_TILE)
        x = _mlp(x, sc2, sh2, g2, w1, w2, *ln2, 512)
        if with_ctx:
            ya_c = _attn_global(qa_c, [(ka_c, va_c)], a_sink[l], BLOCK, "attn_ctx_a")
            yb_c = _attn_global(qb_c, [(kb_c, vb_c)], None, BLOCK, "attn_ctx_b")
            ctx = _merge(ctx, csc1, csh1, cg1, ya_c, yb_c, hs_c[0], hs_c[1], *merge_w, *ln1, m)
            ctx = _mlp(ctx, csc2, csh2, cg2, w1, w2, *ln2, m)
    return x
```

```python
import functools

import jax
import jax.numpy as jnp
from jax import lax
from jax.experimental import pallas as pl
from jax.experimental.pallas import tpu as pltpu

D_MODEL = 1024
DEPTH = 2
GRID_W = 64
HEAD_DIM = 64
ROPE_BASE = 10000.0
BLOCK = 128
A_HEADS = 8
A_KV = 2
B_HEADS = 8
B_KV = 2
LRU_WIDTH = D_MODEL
LRU_BLOCKS = 16
LRU_BLOCK_DIM = LRU_WIDTH // LRU_BLOCKS
CONV_W = 4
LRU_C = 8.0
D_FF = 4 * D_MODEL
ALPHA = (2 * DEPTH) ** 0.25
LN_EPS = 1e-5
RMS_EPS = 1e-6
NEG_INF = -1e30
LOG2_E = 1.4426950408889634
Q_COLS = A_HEADS * HEAD_DIM
KV_COLS = A_KV * HEAD_DIM
ATTN_COLS = 2 * (Q_COLS + 2 * KV_COLS)
GROUP = A_HEADS // A_KV

LANES = 128
SUBLANES = 8
V7X_VMEM_BYTES = 64 * 1024 * 1024
VMEM_LIMIT_BYTES = V7X_VMEM_BYTES - 8 * 1024 * 1024

SCAN_TILE = 256
SEG_LEN = SCAN_TILE // SUBLANES
ROW_TILE = 2 * SCAN_TILE
LOGIT_RANGE = 100.0
STATIC_CHUNKS = 4
WINDOW_SUBTILES = 8
SQRT_FLOOR = 1e-30

F32 = jnp.float32
BF16 = jnp.bfloat16


def _params(*semantics):
    return pltpu.CompilerParams(dimension_semantics=semantics, vmem_limit_bytes=VMEM_LIMIT_BYTES)


def _resident(block_shape, index_map):
    return pl.BlockSpec(block_shape, index_map, pipeline_mode=pl.Buffered(1))


def _modulate(x, sc_ref, sh_ref):
    return (x * (1.0 + sc_ref[...]) + sh_ref[...]).astype(BF16)


def _layernorm(y, g, b):
    mu = jnp.mean(y, axis=-1, keepdims=True)
    yc = y - mu
    var = jnp.mean(yc * yc, axis=-1, keepdims=True)
    return yc * lax.rsqrt(var + LN_EPS) * g + b


def _ada_kernel(c_ref, w_ref, b_ref, o_ref):
    c = c_ref[...]
    s = c * jax.nn.sigmoid(c)
    o_ref[...] = jnp.dot(s, w_ref[...], preferred_element_type=F32,
                         precision=lax.Precision.HIGHEST) + b_ref[...]


def _ada(cc, w_ada, b_ada):
    depth, d, cols = w_ada.shape
    tn = cols // 4
    return pl.pallas_call(
        _ada_kernel,
        grid=(depth, cols // tn),
        in_specs=[pl.BlockSpec((SUBLANES, d), lambda l, j: (0, 0)),
                  pl.BlockSpec((None, d, tn), lambda l, j: (l, 0, j)),
                  pl.BlockSpec((None, 1, tn), lambda l, j: (l, 0, j))],
        out_specs=pl.BlockSpec((None, SUBLANES, tn), lambda l, j: (l, 0, j)),
        out_shape=jax.ShapeDtypeStruct((depth, SUBLANES, cols), F32),
        compiler_params=_params("arbitrary", "arbitrary"),
        name="ada_mod",
    )(cc, w_ada, b_ada.reshape(depth, 1, cols))


def _inproj_kernel(*refs, rope):
    refs = list(refs)
    (xp_ref, x_ref, xn_ref, sc_ref, sh_ref, wa_ref, wr_ref, gq_ref, gk_ref, ones_ref, perm_ref,
     cw_ref, cb_ref) = [refs.pop(0) for _ in range(13)]
    if rope:
        cos_ref, sin_ref = refs.pop(0), refs.pop(0)
    qa_ref, ka_ref, va_ref, qb_ref, kb_ref, vb_ref, xs_ref = refs
    i = pl.program_id(1)
    nt = pl.num_programs(1)
    h = _modulate(x_ref[...], sc_ref, sh_ref)
    tm = h.shape[0]

    z = jnp.dot(h, wa_ref[...], preferred_element_type=F32)
    lane = lax.broadcasted_iota(jnp.int32, (tm, LANES), 1)
    first16 = (lane & 31) < 16
    ones = ones_ref[...]
    scale = HEAD_DIM ** -0.5 * LOG2_E

    def rot(zc):
        if not rope:
            return zc
        partner = jnp.where(first16, pltpu.roll(zc, LANES - 16, 1), pltpu.roll(zc, 16, 1))
        return zc * cos_ref[...] + partner * sin_ref[...]

    def rms(zc, g_ref):
        sq = zc * zc
        hi = sq.astype(BF16)
        lo = (sq - hi.astype(F32)).astype(BF16)
        ss = (jnp.dot(hi, ones, preferred_element_type=F32)
              + jnp.dot(lo, ones, preferred_element_type=F32))
        return zc * lax.rsqrt(ss * (1.0 / HEAD_DIM) + RMS_EPS) * g_ref[...]

    def col(start):
        return z[:, start:start + LANES]

    for c in range(Q_COLS // LANES):
        qa_ref[:, c * LANES:(c + 1) * LANES] = (rot(col(c * LANES)) * scale).astype(BF16)
    base = Q_COLS
    ka_ref[...] = rot(col(base)).astype(BF16)
    va_t = col(base + KV_COLS).T.astype(BF16)
    for c in range(va_ref.shape[0]):
        va_ref[c] = va_t[:, c * BLOCK:(c + 1) * BLOCK]
    base = Q_COLS + 2 * KV_COLS
    for c in range(Q_COLS // LANES):
        qb_ref[:, c * LANES:(c + 1) * LANES] = (
            rot(rms(col(base + c * LANES), gq_ref)) * scale).astype(BF16)
    base += Q_COLS
    kb_ref[...] = rot(rms(col(base), gk_ref)).astype(BF16)
    vb_ref[0] = col(base + KV_COLS).T.astype(BF16)

    ntile = tm // SCAN_TILE
    hp = [jnp.dot(perm_ref[...], h[k * SCAN_TILE:(k + 1) * SCAN_TILE], preferred_element_type=F32).astype(BF16)
          for k in range(ntile)]
    halo = _modulate(jnp.concatenate([xp_ref[...], xn_ref[...]], axis=0), sc_ref, sh_ref)
    zall = jnp.dot(jnp.concatenate(hp + [halo], axis=0), wr_ref[...], preferred_element_type=F32)
    zs = [zall[k * SCAN_TILE:(k + 1) * SCAN_TILE] for k in range(ntile)]
    edge_prev = jnp.where(i > 0, zall[tm:tm + SUBLANES], 0.0)
    edge_next = jnp.where(i < nt - 1, zall[tm + SUBLANES:], 0.0)
    sub = lax.broadcasted_iota(jnp.int32, (SUBLANES, zall.shape[1]), 0)
    last_row = SCAN_TILE - 1
    for k, zk in enumerate(zs):
        tok_before = edge_prev[SUBLANES - 1:SUBLANES] if k == 0 else zs[k - 1][last_row:last_row + 1]
        tok_after1 = edge_next[0:1] if k == ntile - 1 else zs[k + 1][0:1]
        tok_after2 = edge_next[1:2] if k == ntile - 1 else zs[k + 1][SUBLANES:SUBLANES + 1]
        first, second, last = zk[0:SUBLANES], zk[SUBLANES:2 * SUBLANES], zk[SCAN_TILE - SUBLANES:]
        before = jnp.where(sub == 0, tok_before, pltpu.roll(last, 1, 0))
        after1 = jnp.where(sub == SUBLANES - 1, tok_after1, pltpu.roll(first, SUBLANES - 1, 0))
        after2 = jnp.where(sub == SUBLANES - 1, tok_after2, pltpu.roll(second, SUBLANES - 1, 0))
        taps = [jnp.concatenate([before, zk[:SCAN_TILE - SUBLANES]], axis=0),
                zk,
                jnp.concatenate([zk[SUBLANES:], after1], axis=0),
                jnp.concatenate([zk[2 * SUBLANES:], after1, after2], axis=0)]
        out = cb_ref[...]
        for tap in range(CONV_W):
            out = out + cw_ref[tap:tap + 1, :] * taps[tap]
        xs_ref[k * SCAN_TILE:(k + 1) * SCAN_TILE, :] = out


def _inproj(x, sc, sh, w_attn, w_rnn, gq, gk, ones, perm, cw, cb, rope_tabs, tm):
    bsz, rows, d = x.shape
    rope = rope_tabs is not None
    hb = tm // SUBLANES
    mod_spec = pl.BlockSpec((None, 1, d), lambda b, i: (b, 0, 0))
    vec_spec = _resident((1, LANES), lambda b, i: (0, 0))
    rnn_cols = w_rnn.shape[1]
    in_specs = [pl.BlockSpec((None, SUBLANES, d), lambda b, i: (b, jnp.maximum(i * hb - 1, 0), 0)),
                pl.BlockSpec((None, tm, d), lambda b, i: (b, i, 0)),
                pl.BlockSpec((None, SUBLANES, d),
                             lambda b, i: (b, jnp.minimum((i + 1) * hb, rows // SUBLANES - 1), 0)),
                mod_spec, mod_spec,
                _resident((d, ATTN_COLS), lambda b, i: (0, 0)), _resident((d, rnn_cols), lambda b, i: (0, 0)),
                vec_spec, vec_spec, _resident((LANES, LANES), lambda b, i: (0, 0)),
                _resident((SCAN_TILE, SCAN_TILE), lambda b, i: (0, 0)),
                _resident((CONV_W, rnn_cols), lambda b, i: (0, 0)), _resident((1, rnn_cols), lambda b, i: (0, 0))]
    args = [x, x, x, sc, sh, w_attn, w_rnn, gq, gk, ones, perm, cw, cb]
    if rope:
        tab_spec = pl.BlockSpec((tm, LANES), lambda b, i: (i, 0))
        in_specs += [tab_spec, tab_spec]
        args += list(rope_tabs)

    def out(cols, dtype=BF16):
        return (pl.BlockSpec((None, tm, cols), lambda b, i: (b, i, 0)),
                jax.ShapeDtypeStruct((bsz, rows, cols), dtype))

    def out_t(slab):
        return (pl.BlockSpec((None, tm // slab, KV_COLS, slab), lambda b, i: (b, i, 0, 0)),
                jax.ShapeDtypeStruct((bsz, rows // slab, KV_COLS, slab), BF16))

    outs = [out(Q_COLS), out(KV_COLS), out_t(BLOCK), out(Q_COLS), out(KV_COLS), out_t(tm), out(rnn_cols, F32)]
    return pl.pallas_call(
        functools.partial(_inproj_kernel, rope=rope),
        grid=(bsz, rows // tm),
        in_specs=in_specs,
        out_specs=[o[0] for o in outs],
        out_shape=[o[1] for o in outs],
        compiler_params=_params("arbitrary", "arbitrary"),
        name="inproj_rope" if rope else "inproj_ctx",
    )(*args)


def _attn_kernel(*refs, segs, n_src, has_sink, tq, nsub, lookahead, scan):
    refs = list(refs)
    sink_ref = refs.pop(0) if has_sink else None
    q_ref = refs.pop(0)
    qn_ref = refs.pop(0) if lookahead else None
    kv_refs = [(refs.pop(0), refs.pop(0)) for _ in range(n_src)]
    if scan:
        xs_ref, wg_ref, br_ref, bi_ref, lam_ref, h0_ref = [refs.pop(0) for _ in range(6)]
    o_ref = refs.pop(0)
    if scan:
        h_ref = refs.pop(0)
        u_scr, a_scr, nsp_scr, carry_scr = [refs.pop() for _ in range(4)]
    i = pl.program_id(1)
    nblk = pl.num_programs(1) * nsub
    width = GROUP * tq

    def sink_row(kvi):
        return jnp.concatenate(
            [jnp.full((1, tq), sink_ref[GROUP * kvi + hh] * LOG2_E, F32) for hh in range(GROUP)], axis=1)

    def query_operand(q, kvi):
        qt = q.T
        heads = jnp.concatenate(
            [qt[(GROUP * kvi + hh) * HEAD_DIM:(GROUP * kvi + hh + 1) * HEAD_DIM, :] for hh in range(GROUP)],
            axis=1)
        parts = [jnp.zeros((HEAD_DIM, width), BF16)] * A_KV
        parts[kvi] = heads
        return jnp.concatenate(parts, axis=0)

    def run_tile(sub):
        mine = refs[sub * 4 * A_KV:(sub + 1) * 4 * A_KV]
        s_scr, m_scr, l_scr, acc_scr = [mine[g * A_KV:(g + 1) * A_KV] for g in range(4)]
        pos = i * nsub + sub
        rows = slice(sub * tq, (sub + 1) * tq)

        def seg_mask(where):
            if where not in ("prev", "next"):
                return None
            j = lax.broadcasted_iota(jnp.int32, (BLOCK, width), 0)
            r = lax.broadcasted_iota(jnp.int32, (BLOCK, width), 1) & (tq - 1)
            far = jnp.int32(4 * BLOCK)
            if where == "prev":
                return (j - r) >= jnp.where(pos > 0, 0, far)
            return (r - j) >= jnp.where(pos < nblk - 1, 0, far)

        def block_of(where):
            return {"cur": pos, "prev": jnp.maximum(pos - 1, 0), "next": jnp.minimum(pos + 1, nblk - 1)}[where]

        def for_each_chunk(*fns, along=None):
            base = 0
            for src, where, nchunks, kc in segs:
                k_ref, vt_ref = kv_refs[src]
                if where is None:
                    row0, slab0 = 0, 0
                else:
                    slab0 = block_of(where)
                    row0 = pl.multiple_of(slab0 * BLOCK, BLOCK)
                if nchunks <= STATIC_CHUNKS:
                    for ci in range(nchunks):
                        for fn in fns:
                            fn(k_ref, vt_ref, row0 + ci * kc, slab0 + ci, base + ci * kc, kc, where)
                else:
                    def body(ci, carry, k_ref=k_ref, vt_ref=vt_ref, kc=kc, base=base):
                        r0 = pl.multiple_of(ci * kc, kc)
                        for fn in fns:
                            fn(k_ref, vt_ref, r0, ci, base + r0, kc, None)
                        if along is not None:
                            along(ci)
                        return carry
                    lax.fori_loop(0, nchunks, body, 0, unroll=4)
                base += nchunks * kc

        def start_scores(kvi):
            m_scr[kvi][...] = sink_row(kvi) if has_sink else jnp.full((1, width), NEG_INF, F32)

        def scores(kvi, qz, k_ref, vt_ref, r0, slab, srow, kc, where):
            s = jnp.dot(k_ref[pl.ds(r0, kc), :], qz, preferred_element_type=F32)
            mask = seg_mask(where)
            if mask is not None:
                s = jnp.where(mask, s, NEG_INF)
            s_scr[kvi][pl.ds(srow, kc), :] = s
            m_scr[kvi][...] = jnp.maximum(m_scr[kvi][...], jnp.max(s, axis=0, keepdims=True))

        def start_values(kvi):
            if has_sink:
                l_scr[kvi][...] = jnp.exp2(sink_row(kvi) - m_scr[kvi][...])
            else:
                l_scr[kvi][...] = jnp.zeros((1, width), F32)
            acc_scr[kvi][...] = jnp.zeros((HEAD_DIM, width), F32)

        def values(kvi, k_ref, vt_ref, r0, slab, srow, kc, where):
            p = jnp.exp2(s_scr[kvi][pl.ds(srow, kc), :] - m_scr[kvi][...])
            l_scr[kvi][...] += jnp.sum(p, axis=0, keepdims=True)
            acc_scr[kvi][...] += jnp.dot(vt_ref[slab, kvi * HEAD_DIM:(kvi + 1) * HEAD_DIM, :], p.astype(BF16),
                                         preferred_element_type=F32)

        def first_scores():
            start_scores(0)
            for_each_chunk(functools.partial(scores, 0, query_operand(q_ref[rows, :], 0)))

        def phase_a():
            if lookahead:
                pl.when(i == 0)(first_scores)
            else:
                first_scores()

        half = SCAN_TILE // 2

        def scan_first_half(c):
            _rglru_gates(xs_ref, wg_ref, br_ref, bi_ref, nsp_scr, a_scr, u_scr, c, slice(0, half))

        def scan_second_half(c):
            _rglru_gates(xs_ref, wg_ref, br_ref, bi_ref, nsp_scr, a_scr, u_scr, c, slice(half, SCAN_TILE))
            _rglru_scan(a_scr, u_scr, h_ref, carry_scr, c, scan == "bwd")

        def phase_b():
            start_scores(1)
            start_values(0)
            for_each_chunk(functools.partial(scores, 1, query_operand(q_ref[rows, :], 1)),
                           functools.partial(values, 0), along=scan_first_half if scan else None)

        def phase_c():
            start_values(1)
            if lookahead:
                start_scores(0)
                for_each_chunk(functools.partial(values, 1),
                               functools.partial(scores, 0, query_operand(qn_ref[...], 0)),
                               along=scan_second_half if scan else None)
            else:
                for_each_chunk(functools.partial(values, 1))

        def finish():
            outs = []
            for kvi in range(A_KV):
                o = acc_scr[kvi][...] / l_scr[kvi][...]
                outs += [o[:, hh * tq:(hh + 1) * tq] for hh in range(GROUP)]
            o_ref[rows, :] = jnp.concatenate(outs, axis=0).T.astype(BF16)

        return phase_a, phase_b, phase_c, finish

    if scan:
        nsp_scr[...] = _neg_softplus_neg(lam_ref[...])

        @pl.when(i == 0)
        def _():
            carry_scr[...] = h0_ref[...]

    for phases in zip(*[run_tile(sub) for sub in range(nsub)]):
        for phase in phases:
            phase()


def _attn_bounded_kernel(*refs, segs, n_src, tq, scan):
    refs = list(refs)
    q_ref = refs.pop(0)
    kv_refs = [(refs.pop(0), refs.pop(0)) for _ in range(n_src)]
    if scan:
        xs_ref, wg_ref, br_ref, bi_ref, lam_ref, h0_ref = [refs.pop(0) for _ in range(6)]
    o_ref = refs.pop(0)
    if scan:
        h_ref = refs.pop(0)
        carry_scr, nsp_scr = refs.pop(), refs.pop()
        gate_scr = [(refs.pop(), refs.pop()) for _ in range(2)]
    l_scr, acc_scr = refs[:A_KV], refs[A_KV:2 * A_KV]
    s_scr = [refs[2 * A_KV:3 * A_KV], refs[3 * A_KV:4 * A_KV]]
    i = pl.program_id(1)
    width = GROUP * tq

    if scan:
        nsp_scr[...] = _neg_softplus_neg(lam_ref[...])

        @pl.when(i == 0)
        def _():
            carry_scr[...] = h0_ref[...]

    qt = q_ref[...].T
    qz = []
    for kvi in range(A_KV):
        heads = jnp.concatenate(
            [qt[(GROUP * kvi + hh) * HEAD_DIM:(GROUP * kvi + hh + 1) * HEAD_DIM, :] for hh in range(GROUP)],
            axis=1)
        parts = [jnp.zeros((HEAD_DIM, width), BF16)] * A_KV
        parts[kvi] = heads
        qz.append(jnp.concatenate(parts, axis=0))
        l_scr[kvi][...] = jnp.zeros((1, width), F32)
        acc_scr[kvi][...] = jnp.zeros((HEAD_DIM, width), F32)

    def scores(k_ref, r0, kc, buf):
        k = k_ref[pl.ds(r0, kc), :]
        for kvi in range(A_KV):
            buf[kvi][0:kc, :] = jnp.dot(k, qz[kvi], preferred_element_type=F32)

    def values(vt_ref, slab, kc, buf):
        for kvi in range(A_KV):
            p = jnp.exp2(buf[kvi][0:kc, :])
            l_scr[kvi][...] += jnp.sum(p, axis=0, keepdims=True)
            acc_scr[kvi][...] += jnp.dot(vt_ref[slab, kvi * HEAD_DIM:(kvi + 1) * HEAD_DIM, :], p.astype(BF16),
                                         preferred_element_type=F32)

    def gates(c, buf):
        _rglru_gates(xs_ref, wg_ref, br_ref, bi_ref, nsp_scr, buf[0], buf[1], c, slice(None), to_chunk=True)

    def scan_chunk(c, buf):
        _rglru_scan(buf[0], buf[1], h_ref, carry_scr, c, scan == "bwd", from_chunk=True)

    (src0, _, nchunks, kc), rest = segs[0], segs[1:]
    k_ref, vt_ref = kv_refs[src0]
    assert nchunks % 2 == 0
    last = nchunks - 1
    scores(k_ref, 0, kc, s_scr[0])
    if scan:
        gates(0, gate_scr[0])

    def body(j, carry):
        even, odd = 2 * j, 2 * j + 1
        nxt = jnp.minimum(even + 2, last)
        scores(k_ref, pl.multiple_of(odd * kc, kc), kc, s_scr[1])
        values(vt_ref, even, kc, s_scr[0])
        if scan:
            gates(odd, gate_scr[1])
            scan_chunk(even, gate_scr[0])
        scores(k_ref, pl.multiple_of(nxt * kc, kc), kc, s_scr[0])
        values(vt_ref, odd, kc, s_scr[1])
        if scan:
            gates(nxt, gate_scr[0])
            scan_chunk(odd, gate_scr[1])
        return carry

    lax.fori_loop(0, nchunks // 2, body, 0, unroll=True)

    for src, where, nch, kcs in rest:
        assert where is None and nch <= STATIC_CHUNKS
        kr, vr = kv_refs[src]
        for ci in range(nch):
            scores(kr, ci * kcs, kcs, s_scr[ci % 2])
            values(vr, ci, kcs, s_scr[ci % 2])

    outs = []
    for kvi in range(A_KV):
        o = acc_scr[kvi][...] / l_scr[kvi][...]
        outs += [o[:, hh * tq:(hh + 1) * tq] for hh in range(GROUP)]
    o_ref[...] = jnp.concatenate(outs, axis=0).T.astype(BF16)


def _attention(q, kv_arrays, segs, sink, tq, nsub, name, lookahead=False, part=(0, 1), scan=None,
               bounded=False):
    assert not (lookahead and nsub != 1) and not (bounded and (lookahead or nsub != 1 or sink is not None))
    bsz, rows, _ = q.shape
    nb = rows // (tq * nsub * part[1])
    first = part[0] * nb
    has_sink = sink is not None
    in_specs, args = [], []
    if has_sink:
        in_specs.append(pl.BlockSpec(memory_space=pltpu.SMEM))
        args.append(sink)
    in_specs.append(pl.BlockSpec((None, tq * nsub, Q_COLS), lambda b, i: (b, first + i, 0)))
    args.append(q)
    if lookahead:
        in_specs.append(pl.BlockSpec((None, tq, Q_COLS), lambda b, i: (b, first + jnp.minimum(i + 1, nb - 1), 0)))
        args.append(q)
    for k, vt in kv_arrays:
        in_specs += [pl.BlockSpec((None,) + k.shape[1:], lambda b, i: (b, 0, 0)),
                     pl.BlockSpec((None,) + vt.shape[1:], lambda b, i: (b, 0, 0, 0))]
        args += [k, vt]
    width = GROUP * tq
    nk_total = sum(nchunks * kc for _, _, nchunks, kc in segs)
    per_head = [(nk_total, width), (1, width), (1, width), (HEAD_DIM, width)]
    if bounded:
        kc0 = segs[0][3]
        per_head = [(1, width), (HEAD_DIM, width), (kc0, width), (kc0, width)]
    out_specs = [pl.BlockSpec((None, tq * nsub, Q_COLS), lambda b, i: (b, i, 0))]
    out_shape = [jax.ShapeDtypeStruct((bsz, rows // part[1], Q_COLS), BF16)]
    scratch = [pltpu.VMEM(shape, F32) for _ in range(nsub) for shape in per_head for _ in range(A_KV)]
    direction = None
    if scan is not None:
        direction, xs, h0, wg, br, bi, lam = scan
        srows, swidth = xs.shape[1:]
        assert srows // SCAN_TILE == nb and swidth // LANES == segs[0][2]
        tile = pl.BlockSpec((None, SCAN_TILE, swidth),
                            (lambda b, i: (b, nb - 1 - i, 0)) if direction == "bwd" else (lambda b, i: (b, i, 0)))
        vec = _resident((1, swidth), lambda b, i: (0, 0))
        in_specs += [tile, _resident(wg.shape, lambda b, i: (0, 0, 0)), vec, vec, vec,
                     pl.BlockSpec((None, 1, swidth), lambda b, i: (b, 0, 0))]
        args += [xs, wg, br, bi, lam, h0]
        out_specs.append(tile)
        out_shape.append(jax.ShapeDtypeStruct(xs.shape, F32))
        if bounded:
            scratch += [pltpu.VMEM((SCAN_TILE, LANES), F32) for _ in range(4)]
            scratch += [pltpu.VMEM((1, swidth), F32), pltpu.VMEM((1, swidth), F32)]
        else:
            scratch += [pltpu.VMEM((1, swidth), F32), pltpu.VMEM((1, swidth), F32),
                        pltpu.VMEM((SCAN_TILE, swidth), F32), pltpu.VMEM((SCAN_TILE, swidth), F32)]
    if bounded:
        body = functools.partial(_attn_bounded_kernel, segs=tuple(segs), n_src=len(kv_arrays), tq=tq, scan=direction)
    else:
        body = functools.partial(_attn_kernel, segs=tuple(segs), n_src=len(kv_arrays), has_sink=has_sink, tq=tq,
                                 nsub=nsub, lookahead=lookahead, scan=direction)
    outs = pl.pallas_call(
        body,
        grid=(bsz, nb),
        in_specs=in_specs,
        out_specs=out_specs,
        out_shape=out_shape,
        scratch_shapes=scratch,
        compiler_params=_params("arbitrary", "arbitrary"),
        name=name,
    )(*args)
    return outs if scan is not None else outs[0]


def _whole_seg(src, vt):
    return (src, None, vt.shape[1], vt.shape[3])


def _attn_window(q, k, vt, k_ctx, vt_ctx, sink):
    assert vt.shape[3] == BLOCK
    segs = [(0, "cur", 1, BLOCK), (0, "prev", 1, BLOCK), (0, "next", 1, BLOCK), _whole_seg(1, vt_ctx)]
    return _attention(q, [(k, vt), (k_ctx, vt_ctx)], segs, sink, BLOCK, WINDOW_SUBTILES, "attn_window")


def _attn_global(q, kv_list, sink, tq, name, lookahead=False, part=(0, 1), scan=None, bounded=False):
    segs = [_whole_seg(src, vt) for src, (_, vt) in enumerate(kv_list)]
    return _attention(q, kv_list, segs, sink, tq, 1, name, lookahead, part, scan, bounded)


def _lane_chunk(c):
    if isinstance(c, int):
        return slice(c * LANES, (c + 1) * LANES)
    return pl.ds(pl.multiple_of(c * LANES, LANES), LANES)


def _neg_softplus_neg(lam):
    return -(jnp.maximum(-lam, 0.0) + jnp.log1p(jnp.exp(-jnp.abs(lam))))


def _rglru_gates(x_ref, wg_ref, br_ref, bi_ref, nsp_ref, a_scr, u_scr, c, rows, to_chunk=False):
    cs = _lane_chunk(c)
    dst = slice(None) if to_chunk else cs
    xcc = x_ref[rows, cs]
    g = jnp.dot(xcc.astype(BF16), wg_ref[c], preferred_element_type=F32)
    r = jax.nn.sigmoid(g[:, :LANES] + br_ref[:, cs])
    ig = jax.nn.sigmoid(g[:, LANES:] + bi_ref[:, cs])
    log_a = (LRU_C * r) * nsp_ref[:, cs]
    a = jnp.exp(log_a)
    w = -jnp.tanh(log_a) * (a * a + 1.0)
    a_scr[rows, dst] = a
    u_scr[rows, dst] = (w * lax.rsqrt(jnp.maximum(w, SQRT_FLOOR))) * (ig * xcc)


def _rglru_scan(a_scr, u_scr, h_ref, carry_scr, c, reverse, from_chunk=False):
    cs = _lane_chunk(c)
    src = slice(None) if from_chunk else cs
    ngroups = a_scr.shape[0] // SUBLANES
    sub = lax.broadcasted_iota(jnp.int32, (SUBLANES, LANES), 0)
    order = range(ngroups - 1, -1, -1) if reverse else range(ngroups)

    def rows(g):
        return slice(g * SUBLANES, (g + 1) * SUBLANES)

    h_loc = decay = None
    for gi in order:
        a_g, u_g = a_scr[rows(gi), src], u_scr[rows(gi), src]
        h_loc = u_g if h_loc is None else a_g * h_loc + u_g
        decay = a_g if decay is None else a_g * decay
        h_ref[rows(gi), cs] = h_loc
        a_scr[rows(gi), src] = decay

    end, dec = h_loc, decay
    for s in (1, 2, 4):
        if reverse:
            ok = sub < SUBLANES - s
            shift = SUBLANES - s
        else:
            ok = sub >= s
            shift = s
        end = end + dec * jnp.where(ok, pltpu.roll(end, shift, 0), 0.0)
        dec = dec * jnp.where(ok, pltpu.roll(dec, shift, 0), 1.0)
    carry = carry_scr[:, cs]
    seg_end = end + dec * carry
    if reverse:
        start = jnp.where(sub == SUBLANES - 1, carry, pltpu.roll(seg_end, SUBLANES - 1, 0))
        carry_scr[:, cs] = seg_end[0:1]
    else:
        start = jnp.where(sub == 0, carry, pltpu.roll(seg_end, 1, 0))
        carry_scr[:, cs] = seg_end[SUBLANES - 1:SUBLANES]
    for gi in order:
        h_ref[rows(gi), cs] = h_ref[rows(gi), cs] + a_scr[rows(gi), src] * start


def _rglru_kernel(x_ref, wg_ref, br_ref, bi_ref, lam_ref, h0_ref,
                  h_ref, hfin_ref, carry_scr, nsp_scr, a_scr, u_scr, *, reverse):
    j = pl.program_id(1)
    nt = pl.num_programs(1)

    @pl.when(j == 0)
    def _():
        carry_scr[...] = h0_ref[...]

    nsp_scr[...] = _neg_softplus_neg(lam_ref[...])

    for c in range(x_ref.shape[1] // LANES):
        _rglru_gates(x_ref, wg_ref, br_ref, bi_ref, nsp_scr, a_scr, u_scr, c, slice(None))
        _rglru_scan(a_scr, u_scr, h_ref, carry_scr, c, reverse)

    @pl.when(j == nt - 1)
    def _():
        hfin_ref[...] = carry_scr[...]


def _rglru(xs, h0, wg, br, bi, lam, reverse):
    bsz, rows, width = xs.shape
    nt = rows // SCAN_TILE

    def tpos(j):
        return nt - 1 - j if reverse else j

    vec = _resident((1, width), lambda b, j: (0, 0))
    tile = pl.BlockSpec((None, SCAN_TILE, width), lambda b, j: (b, tpos(j), 0))
    state = pl.BlockSpec((None, 1, width), lambda b, j: (b, 0, 0))
    return pl.pallas_call(
        functools.partial(_rglru_kernel, reverse=reverse),
        grid=(bsz, nt),
        in_specs=[tile, _resident(wg.shape, lambda b, j: (0, 0, 0)), vec, vec, vec, state],
        out_specs=[tile, state],
        out_shape=[jax.ShapeDtypeStruct((bsz, rows, width), F32),
                   jax.ShapeDtypeStruct((bsz, 1, width), F32)],
        scratch_shapes=[pltpu.VMEM((1, width), F32), pltpu.VMEM((1, width), F32),
                        pltpu.VMEM((SCAN_TILE, width), F32), pltpu.VMEM((SCAN_TILE, width), F32)],
        compiler_params=_params("arbitrary", "arbitrary"),
        name="rglru_bwd" if reverse else "rglru_fwd",
    )(xs, wg, br, bi, lam, h0)


def _merge_kernel(*refs, yb_parts):
    refs = list(refs)
    x_ref, sc_ref, sh_ref, gate_ref, ya_ref = [refs.pop(0) for _ in range(5)]
    yb_refs = [refs.pop(0) for _ in range(yb_parts)]
    (hf_ref, hb_ref, perm_ref, perm_t_ref, wy_ref, wg_ref, wa_ref, wb_ref, wc_ref, wo_ref,
     lng_ref, lnb_ref, o_ref) = refs
    per_part = pl.num_programs(1) // yb_parts
    yb = yb_refs[-1][...]
    for p in range(yb_parts - 2, -1, -1):
        yb = jnp.where(pl.program_id(1) < (p + 1) * per_part, yb_refs[p][...], yb)
    x = x_ref[...]
    d = x.shape[1]
    h = _modulate(x, sc_ref, sh_ref)
    def per_scan_tile(p_ref, rows):
        return jnp.concatenate(
            [jnp.dot(p_ref[...], rows[k:k + SCAN_TILE], preferred_element_type=F32).astype(BF16)
             for k in range(0, rows.shape[0], SCAN_TILE)], axis=0)

    y_rnn = jnp.dot(per_scan_tile(perm_ref, h), wy_ref[...], preferred_element_type=F32)
    yc = per_scan_tile(perm_t_ref, ((hf_ref[...] + hb_ref[...]) * jax.nn.gelu(y_rnn)).astype(BF16))
    zz = jnp.dot(h, wg_ref[...], preferred_element_type=F32)
    mix = (jax.nn.sigmoid(zz[:, 0:d]) * jnp.dot(ya_ref[...], wa_ref[...], preferred_element_type=F32)
           + jax.nn.sigmoid(zz[:, d:2 * d]) * jnp.dot(yb, wb_ref[...], preferred_element_type=F32)
           + jax.nn.sigmoid(zz[:, 2 * d:3 * d]) * jnp.dot(yc, wc_ref[...], preferred_element_type=F32))
    o = jnp.dot(mix.astype(BF16), wo_ref[...], preferred_element_type=F32)
    o_ref[...] = _layernorm(ALPHA * x + gate_ref[...] * o, lng_ref[...], lnb_ref[...])


def _merge(x, sc, sh, gate, ya, yb_parts, hf, hb, perm, perm_t, wy, wg, wa, wb, wc, wo, lng, lnb, tm):
    bsz, rows, d = x.shape
    per_part = rows // tm // len(yb_parts)

    def part_spec(p):
        return pl.BlockSpec((None, tm, Q_COLS), lambda b, i: (b, jnp.clip(i - p * per_part, 0, per_part - 1), 0))
    mod_spec = pl.BlockSpec((None, 1, d), lambda b, i: (b, 0, 0))
    vec = _resident((1, d), lambda b, i: (0, 0))

    def tile(cols):
        return pl.BlockSpec((None, tm, cols), lambda b, i: (b, i, 0))

    def weight(w):
        return _resident(w.shape, lambda b, i: (0, 0))

    weights = [perm, perm_t, wy, wg, wa, wb, wc, wo]
    return pl.pallas_call(
        functools.partial(_merge_kernel, yb_parts=len(yb_parts)),
        grid=(bsz, rows // tm),
        in_specs=[tile(d), mod_spec, mod_spec, mod_spec, tile(Q_COLS)]
                 + [part_spec(p) for p in range(len(yb_parts))] + [tile(d), tile(d)]
                 + [weight(w) for w in weights] + [vec, vec],
        out_specs=tile(d),
        out_shape=jax.ShapeDtypeStruct((bsz, rows, d), F32),
        compiler_params=_params("arbitrary", "arbitrary"),
        name="merge_ln",
    )(x, sc, sh, gate, ya, *yb_parts, hf, hb, *weights, lng, lnb)


def _mlp_kernel(x_ref, sc_ref, sh_ref, gate_ref, w1_ref, w2_ref, lng_ref, lnb_ref, o_ref):
    x = x_ref[...]
    h = _modulate(x, sc_ref, sh_ref)
    u = jnp.dot(h, w1_ref[...], preferred_element_type=F32)
    u = jnp.square(jnp.maximum(u, 0.0)).astype(BF16)
    o = jnp.dot(u, w2_ref[...], preferred_element_type=F32)
    o_ref[...] = _layernorm(ALPHA * x + gate_ref[...] * o, lng_ref[...], lnb_ref[...])


def _mlp(x, sc, sh, gate, w1, w2, lng, lnb, tm):
    bsz, rows, d = x.shape
    mod_spec = pl.BlockSpec((None, 1, d), lambda b, i: (b, 0, 0))
    vec = _resident((1, d), lambda b, i: (0, 0))
    tile = pl.BlockSpec((None, tm, d), lambda b, i: (b, i, 0))
    return pl.pallas_call(
        _mlp_kernel,
        grid=(bsz, rows // tm),
        in_specs=[tile, mod_spec, mod_spec, mod_spec,
                  _resident(w1.shape, lambda b, i: (0, 0)), _resident(w2.shape, lambda b, i: (0, 0)), vec, vec],
        out_specs=tile,
        out_shape=jax.ShapeDtypeStruct((bsz, rows, d), F32),
        compiler_params=_params("arbitrary", "arbitrary"),
        name="mlp_ln",
    )(x, sc, sh, gate, w1, w2, lng, lnb)


def _rope_tables(n):
    pos = jnp.arange(n)
    row = (pos // GRID_W).astype(F32)
    colp = (pos % GRID_W).astype(F32)
    nf = HEAD_DIM // 4
    inv = ROPE_BASE ** (-jnp.arange(nf, dtype=F32) / nf)
    ang_r = row[:, None] * inv
    ang_c = colp[:, None] * inv
    cos_t = jnp.concatenate([jnp.cos(ang_r), jnp.cos(ang_r), jnp.cos(ang_c), jnp.cos(ang_c)], axis=-1)
    sin_t = jnp.concatenate([-jnp.sin(ang_r), jnp.sin(ang_r), -jnp.sin(ang_c), jnp.sin(ang_c)], axis=-1)
    return jnp.tile(cos_t, (1, 2)), jnp.tile(sin_t, (1, 2))


def _gate_weights(w_r, w_i):
    def pair(w):
        w = w.reshape(LRU_BLOCKS // 2, 2, LRU_BLOCK_DIM, LRU_BLOCK_DIM)
        z = jnp.zeros_like(w[:, 0])
        top = jnp.concatenate([w[:, 0], z], axis=-1)
        bot = jnp.concatenate([z, w[:, 1]], axis=-1)
        return jnp.concatenate([top, bot], axis=-2)
    return jnp.concatenate([pair(w_r), pair(w_i)], axis=-1).astype(BF16)


def kernel(x, c, ctx, c_ctx, w_ada, b_ada, w_in, a_sink, b_q_gain, b_k_gain, c_conv_w, c_conv_b,
           c_wr, c_br, c_wi, c_bi, c_lam, w_br_a, w_br_b, w_br_c, w_out, ln1_g, ln1_b,
           w_ff1, w_ff2, ln2_g, ln2_b):
    bsz, n, d = x.shape
    m = ctx.shape[1]
    depth = w_ada.shape[0]

    cc = jnp.zeros((SUBLANES, d), F32).at[:bsz].set(c).at[bsz].set(c_ctx)
    mod = _ada(cc, w_ada, b_ada)
    rope_tabs = _rope_tables(n)
    blk = jnp.arange(LANES) // HEAD_DIM
    ones = (blk[:, None] == blk[None, :]).astype(BF16)
    h_zero = jnp.zeros((bsz, 1, d), F32)
    slot = jnp.arange(SCAN_TILE)
    token_of_slot = SEG_LEN * (slot % SUBLANES) + slot // SUBLANES
    perm = (token_of_slot[:, None] == slot[None, :]).astype(BF16)
    perm_t = perm.T

    for l in range(depth):
        with_ctx = l < depth - 1
        mod_lat = mod[l, :bsz].reshape(bsz, 1, 6 * d)
        mod_ctx = jnp.broadcast_to(mod[l, bsz].reshape(1, 1, 6 * d), (bsz, 1, 6 * d))
        sh1, sc1, g1, sh2, sc2, g2 = [mod_lat[:, :, k * d:(k + 1) * d] for k in range(6)]
        csh1, csc1, cg1, csh2, csc2, cg2 = [mod_ctx[:, :, k * d:(k + 1) * d] for k in range(6)]

        w_attn = w_in[l, :, :ATTN_COLS].astype(BF16)
        w_rnn = w_in[l, :, ATTN_COLS:ATTN_COLS + LRU_WIDTH].astype(BF16)
        w_y = w_in[l, :, ATTN_COLS + LRU_WIDTH:ATTN_COLS + 2 * LRU_WIDTH].astype(BF16)
        w_g = w_in[l, :, ATTN_COLS + 2 * LRU_WIDTH:].astype(BF16)
        logit_bound = (HEAD_DIM * HEAD_DIM ** -0.5 * LOG2_E) * jnp.max(jnp.abs(b_q_gain[l])) * jnp.max(
            jnp.abs(b_k_gain[l]))
        gq = jnp.tile(b_q_gain[l], 2).reshape(1, LANES)
        gk = jnp.tile(b_k_gain[l], 2).reshape(1, LANES)

        cb = c_conv_b[l].reshape(1, d)
        conv = (perm, c_conv_w[l], cb)
        qa, ka, va, qb, kb, vb, xs = _inproj(x, sc1, sh1, w_attn, w_rnn, gq, gk, ones, *conv, rope_tabs, ROW_TILE)
        qa_c, ka_c, va_c, qb_c, kb_c, vb_c, xs_c = _inproj(ctx, csc1, csh1, w_attn, w_rnn, gq, gk, ones, *conv,
                                                           None, m)

        ya = _attn_window(qa, ka, va, ka_c, va_c, a_sink[l])
        hs, hs_c, yb_parts = [], [], []
        for di, direction in enumerate(("fwd", "bwd")):
            wg = _gate_weights(c_wr[l, di], c_wi[l, di])
            vecs = [v[l, di].reshape(1, d) for v in (c_br, c_bi, c_lam)]
            h_c, h_fin = _rglru(xs_c, h_zero, wg, *vecs, direction == "bwd")
            scan = (direction, xs, h_fin, wg, *vecs)
            kv_b = [(kb, vb), (kb_c, vb_c)]
            yb_part, h_l = lax.cond(
                logit_bound <= LOGIT_RANGE,
                lambda: _attn_global(qb, kv_b, None, BLOCK, "attn_bounded_" + direction,
                                     part=(di, 2), scan=scan, bounded=True),
                lambda: _attn_global(qb, kv_b, None, BLOCK, "attn_global_" + direction,
                                     lookahead=True, part=(di, 2), scan=scan))
            yb_parts.append(yb_part)
            hs.append(h_l)
            hs_c.append(h_c)

        merge_w = [perm, perm_t, w_y, w_g] + [w[l].astype(BF16) for w in (w_br_a, w_br_b, w_br_c, w_out)]
        ln1 = [ln1_g[l].reshape(1, d), ln1_b[l].reshape(1, d)]
        ln2 = [ln2_g[l].reshape(1, d), ln2_b[l].reshape(1, d)]
        w1 = w_ff1[l].astype(BF16)
        w2 = w_ff2[l].astype(BF16)

        x = _merge(x, sc1, sh1, g1, ya, yb_parts, hs[0], hs[1], *merge_w, *ln1, ROW_TILE)
        x = _mlp(x, sc2, sh2, g2, w1, w2, *ln2, ROW_TILE)
        if with_ctx:
            ya_c = _attn_global(qa_c, [(ka_c, va_c)], a_sink[l], BLOCK, "attn_ctx_a")
            yb_c = _attn_global(qb_c, [(kb_c, vb_c)], None, BLOCK, "attn_ctx_b")
            ctx = _merge(ctx, csc1, csh1, cg1, ya_c, [yb_c], hs_c[0], hs_c[1], *merge_w, *ln1, m)
            ctx = _mlp(ctx, csc2, csh2, cg2, w1, w2, *ln2, m)
    return x
```

```python
import functools

import jax
import jax.numpy as jnp
from jax import lax
from jax.experimental import pallas as pl
from jax.experimental.pallas import tpu as pltpu

D_MODEL = 1024
DEPTH = 2
GRID_W = 64
HEAD_DIM = 64
ROPE_BASE = 10000.0
BLOCK = 128
A_HEADS = 8
A_KV = 2
B_HEADS = 8
B_KV = 2
LRU_WIDTH = D_MODEL
LRU_BLOCKS = 16
LRU_BLOCK_DIM = LRU_WIDTH // LRU_BLOCKS
CONV_W = 4
LRU_C = 8.0
D_FF = 4 * D_MODEL
ALPHA = (2 * DEPTH) ** 0.25
LN_EPS = 1e-5
RMS_EPS = 1e-6
NEG_INF = -1e30
LOG2_E = 1.4426950408889634
Q_COLS = A_HEADS * HEAD_DIM
KV_COLS = A_KV * HEAD_DIM
ATTN_COLS = 2 * (Q_COLS + 2 * KV_COLS)
GROUP = A_HEADS // A_KV

LANES = 128
SUBLANES = 8
V7X_VMEM_BYTES = 64 * 1024 * 1024
VMEM_LIMIT_BYTES = V7X_VMEM_BYTES - 8 * 1024 * 1024

SCAN_TILE = 256
SEG_LEN = SCAN_TILE // SUBLANES
ROW_TILE = 2 * SCAN_TILE
LOGIT_RANGE = 100.0
STATIC_CHUNKS = 4
WINDOW_SUBTILES = 8
SQRT_FLOOR = 1e-30

F32 = jnp.float32
BF16 = jnp.bfloat16


def _params(*semantics):
    return pltpu.CompilerParams(dimension_semantics=semantics, vmem_limit_bytes=VMEM_LIMIT_BYTES)


def _resident(block_shape, index_map):
    return pl.BlockSpec(block_shape, index_map, pipeline_mode=pl.Buffered(1))


def _modulate(x, sc_ref, sh_ref):
    return (x * (1.0 + sc_ref[...]) + sh_ref[...]).astype(BF16)


def _layernorm(y, g, b):
    mu = jnp.mean(y, axis=-1, keepdims=True)
    yc = y - mu
    var = jnp.mean(yc * yc, axis=-1, keepdims=True)
    return yc * lax.rsqrt(var + LN_EPS) * g + b


def _ada_kernel(c_ref, w_ref, b_ref, o_ref):
    c = c_ref[...]
    s = c * jax.nn.sigmoid(c)
    o_ref[...] = jnp.dot(s, w_ref[...], preferred_element_type=F32,
                         precision=lax.Precision.HIGHEST) + b_ref[...]


def _ada(cc, w_ada, b_ada):
    depth, d, cols = w_ada.shape
    tn = cols // 4
    return pl.pallas_call(
        _ada_kernel,
        grid=(depth, cols // tn),
        in_specs=[pl.BlockSpec((SUBLANES, d), lambda l, j: (0, 0)),
                  pl.BlockSpec((None, d, tn), lambda l, j: (l, 0, j)),
                  pl.BlockSpec((None, 1, tn), lambda l, j: (l, 0, j))],
        out_specs=pl.BlockSpec((None, SUBLANES, tn), lambda l, j: (l, 0, j)),
        out_shape=jax.ShapeDtypeStruct((depth, SUBLANES, cols), F32),
        compiler_params=_params("arbitrary", "arbitrary"),
        name="ada_mod",
    )(cc, w_ada, b_ada.reshape(depth, 1, cols))


def _inproj_kernel(*refs, rope):
    refs = list(refs)
    (xp_ref, x_ref, xn_ref, sc_ref, sh_ref, wa_ref, wr_ref, gq_ref, gk_ref, ones_ref, perm_ref,
     cw_ref, cb_ref) = [refs.pop(0) for _ in range(13)]
    if rope:
        cos_ref, sin_ref = refs.pop(0), refs.pop(0)
    qa_ref, ka_ref, va_ref, qb_ref, kb_ref, vb_ref, xs_ref = refs
    i = pl.program_id(1)
    nt = pl.num_programs(1)
    h = _modulate(x_ref[...], sc_ref, sh_ref)
    tm = h.shape[0]

    z = jnp.dot(h, wa_ref[...], preferred_element_type=F32)
    lane = lax.broadcasted_iota(jnp.int32, (tm, LANES), 1)
    first16 = (lane & 31) < 16
    ones = ones_ref[...]
    scale = HEAD_DIM ** -0.5 * LOG2_E

    def rot(zc):
        if not rope:
            return zc
        partner = jnp.where(first16, pltpu.roll(zc, LANES - 16, 1), pltpu.roll(zc, 16, 1))
        return zc * cos_ref[...] + partner * sin_ref[...]

    def rms(zc, g_ref):
        sq = zc * zc
        hi = sq.astype(BF16)
        lo = (sq - hi.astype(F32)).astype(BF16)
        ss = (jnp.dot(hi, ones, preferred_element_type=F32)
              + jnp.dot(lo, ones, preferred_element_type=F32))
        return zc * lax.rsqrt(ss * (1.0 / HEAD_DIM) + RMS_EPS) * g_ref[...]

    def col(start):
        return z[:, start:start + LANES]

    for c in range(Q_COLS // LANES):
        qa_ref[:, c * LANES:(c + 1) * LANES] = (rot(col(c * LANES)) * scale).astype(BF16)
    base = Q_COLS
    ka_ref[...] = rot(col(base)).astype(BF16)
    va_t = col(base + KV_COLS).T.astype(BF16)
    for c in range(va_ref.shape[0]):
        va_ref[c] = va_t[:, c * BLOCK:(c + 1) * BLOCK]
    base = Q_COLS + 2 * KV_COLS
    for c in range(Q_COLS // LANES):
        qb_ref[:, c * LANES:(c + 1) * LANES] = (
            rot(rms(col(base + c * LANES), gq_ref)) * scale).astype(BF16)
    base += Q_COLS
    kb_ref[...] = rot(rms(col(base), gk_ref)).astype(BF16)
    vb_ref[0] = col(base + KV_COLS).T.astype(BF16)

    ntile = tm // SCAN_TILE
    hp = [jnp.dot(perm_ref[...], h[k * SCAN_TILE:(k + 1) * SCAN_TILE], preferred_element_type=F32).astype(BF16)
          for k in range(ntile)]
    halo = _modulate(jnp.concatenate([xp_ref[...], xn_ref[...]], axis=0), sc_ref, sh_ref)
    zall = jnp.dot(jnp.concatenate(hp + [halo], axis=0), wr_ref[...], preferred_element_type=F32)
    zs = [zall[k * SCAN_TILE:(k + 1) * SCAN_TILE] for k in range(ntile)]
    edge_prev = jnp.where(i > 0, zall[tm:tm + SUBLANES], 0.0)
    edge_next = jnp.where(i < nt - 1, zall[tm + SUBLANES:], 0.0)
    sub = lax.broadcasted_iota(jnp.int32, (SUBLANES, zall.shape[1]), 0)
    last_row = SCAN_TILE - 1
    for k, zk in enumerate(zs):
        tok_before = edge_prev[SUBLANES - 1:SUBLANES] if k == 0 else zs[k - 1][last_row:last_row + 1]
        tok_after1 = edge_next[0:1] if k == ntile - 1 else zs[k + 1][0:1]
        tok_after2 = edge_next[1:2] if k == ntile - 1 else zs[k + 1][SUBLANES:SUBLANES + 1]
        first, second, last = zk[0:SUBLANES], zk[SUBLANES:2 * SUBLANES], zk[SCAN_TILE - SUBLANES:]
        before = jnp.where(sub == 0, tok_before, pltpu.roll(last, 1, 0))
        after1 = jnp.where(sub == SUBLANES - 1, tok_after1, pltpu.roll(first, SUBLANES - 1, 0))
        after2 = jnp.where(sub == SUBLANES - 1, tok_after2, pltpu.roll(second, SUBLANES - 1, 0))
        taps = [jnp.concatenate([before, zk[:SCAN_TILE - SUBLANES]], axis=0),
                zk,
                jnp.concatenate([zk[SUBLANES:], after1], axis=0),
                jnp.concatenate([zk[2 * SUBLANES:], after1, after2], axis=0)]
        out = cb_ref[...]
        for tap in range(CONV_W):
            out = out + cw_ref[tap:tap + 1, :] * taps[tap]
        xs_ref[k * SCAN_TILE:(k + 1) * SCAN_TILE, :] = out


def _inproj(x, sc, sh, w_attn, w_rnn, gq, gk, ones, perm, cw, cb, rope_tabs, tm):
    bsz, rows, d = x.shape
    rope = rope_tabs is not None
    hb = tm // SUBLANES
    mod_spec = pl.BlockSpec((None, 1, d), lambda b, i: (b, 0, 0))
    vec_spec = _resident((1, LANES), lambda b, i: (0, 0))
    rnn_cols = w_rnn.shape[1]
    in_specs = [pl.BlockSpec((None, SUBLANES, d), lambda b, i: (b, jnp.maximum(i * hb - 1, 0), 0)),
                pl.BlockSpec((None, tm, d), lambda b, i: (b, i, 0)),
                pl.BlockSpec((None, SUBLANES, d),
                             lambda b, i: (b, jnp.minimum((i + 1) * hb, rows // SUBLANES - 1), 0)),
                mod_spec, mod_spec,
                _resident((d, ATTN_COLS), lambda b, i: (0, 0)), _resident((d, rnn_cols), lambda b, i: (0, 0)),
                vec_spec, vec_spec, _resident((LANES, LANES), lambda b, i: (0, 0)),
                _resident((SCAN_TILE, SCAN_TILE), lambda b, i: (0, 0)),
                _resident((CONV_W, rnn_cols), lambda b, i: (0, 0)), _resident((1, rnn_cols), lambda b, i: (0, 0))]
    args = [x, x, x, sc, sh, w_attn, w_rnn, gq, gk, ones, perm, cw, cb]
    if rope:
        tab_spec = pl.BlockSpec((tm, LANES), lambda b, i: (i, 0))
        in_specs += [tab_spec, tab_spec]
        args += list(rope_tabs)

    def out(cols, dtype=BF16):
        return (pl.BlockSpec((None, tm, cols), lambda b, i: (b, i, 0)),
                jax.ShapeDtypeStruct((bsz, rows, cols), dtype))

    def out_t(slab):
        return (pl.BlockSpec((None, tm // slab, KV_COLS, slab), lambda b, i: (b, i, 0, 0)),
                jax.ShapeDtypeStruct((bsz, rows // slab, KV_COLS, slab), BF16))

    outs = [out(Q_COLS), out(KV_COLS), out_t(BLOCK), out(Q_COLS), out(KV_COLS), out_t(tm), out(rnn_cols, F32)]
    return pl.pallas_call(
        functools.partial(_inproj_kernel, rope=rope),
        grid=(bsz, rows // tm),
        in_specs=in_specs,
        out_specs=[o[0] for o in outs],
        out_shape=[o[1] for o in outs],
        compiler_params=_params("arbitrary", "arbitrary"),
        name="inproj_rope" if rope else "inproj_ctx",
    )(*args)


def _attn_kernel(*refs, segs, n_src, has_sink, tq, nsub, lookahead, scan):
    refs = list(refs)
    sink_ref = refs.pop(0) if has_sink else None
    q_ref = refs.pop(0)
    qn_ref = refs.pop(0) if lookahead else None
    kv_refs = [(refs.pop(0), refs.pop(0)) for _ in range(n_src)]
    if scan:
        xs_ref, wg_ref, br_ref, bi_ref, lam_ref, h0_ref = [refs.pop(0) for _ in range(6)]
    o_ref = refs.pop(0)
    if scan:
        h_ref = refs.pop(0)
        u_scr, a_scr, nsp_scr, carry_scr = [refs.pop() for _ in range(4)]
    i = pl.program_id(1)
    nblk = pl.num_programs(1) * nsub
    width = GROUP * tq

    def sink_row(kvi):
        return jnp.concatenate(
            [jnp.full((1, tq), sink_ref[GROUP * kvi + hh] * LOG2_E, F32) for hh in range(GROUP)], axis=1)

    def query_operand(q, kvi):
        qt = q.T
        heads = jnp.concatenate(
            [qt[(GROUP * kvi + hh) * HEAD_DIM:(GROUP * kvi + hh + 1) * HEAD_DIM, :] for hh in range(GROUP)],
            axis=1)
        parts = [jnp.zeros((HEAD_DIM, width), BF16)] * A_KV
        parts[kvi] = heads
        return jnp.concatenate(parts, axis=0)

    def run_tile(sub):
        mine = refs[sub * 4 * A_KV:(sub + 1) * 4 * A_KV]
        s_scr, m_scr, l_scr, acc_scr = [mine[g * A_KV:(g + 1) * A_KV] for g in range(4)]
        pos = i * nsub + sub
        rows = slice(sub * tq, (sub + 1) * tq)

        def seg_mask(where):
            if where not in ("prev", "next"):
                return None
            j = lax.broadcasted_iota(jnp.int32, (BLOCK, width), 0)
            r = lax.broadcasted_iota(jnp.int32, (BLOCK, width), 1) & (tq - 1)
            far = jnp.int32(4 * BLOCK)
            if where == "prev":
                return (j - r) >= jnp.where(pos > 0, 0, far)
            return (r - j) >= jnp.where(pos < nblk - 1, 0, far)

        def block_of(where):
            return {"cur": pos, "prev": jnp.maximum(pos - 1, 0), "next": jnp.minimum(pos + 1, nblk - 1)}[where]

        def for_each_chunk(*fns, along=None):
            base = 0
            for src, where, nchunks, kc in segs:
                k_ref, vt_ref = kv_refs[src]
                if where is None:
                    row0, slab0 = 0, 0
                else:
                    slab0 = block_of(where)
                    row0 = pl.multiple_of(slab0 * BLOCK, BLOCK)
                if nchunks <= STATIC_CHUNKS:
                    for ci in range(nchunks):
                        for fn in fns:
                            fn(k_ref, vt_ref, row0 + ci * kc, slab0 + ci, base + ci * kc, kc, where)
                else:
                    def body(ci, carry, k_ref=k_ref, vt_ref=vt_ref, kc=kc, base=base):
                        r0 = pl.multiple_of(ci * kc, kc)
                        for fn in fns:
                            fn(k_ref, vt_ref, r0, ci, base + r0, kc, None)
                        if along is not None:
                            along(ci)
                        return carry
                    lax.fori_loop(0, nchunks, body, 0, unroll=4)
                base += nchunks * kc

        def start_scores(kvi):
            m_scr[kvi][...] = sink_row(kvi) if has_sink else jnp.full((1, width), NEG_INF, F32)

        def scores(kvi, qz, k_ref, vt_ref, r0, slab, srow, kc, where):
            s = jnp.dot(k_ref[pl.ds(r0, kc), :], qz, preferred_element_type=F32)
            mask = seg_mask(where)
            if mask is not None:
                s = jnp.where(mask, s, NEG_INF)
            s_scr[kvi][pl.ds(srow, kc), :] = s
            m_scr[kvi][...] = jnp.maximum(m_scr[kvi][...], jnp.max(s, axis=0, keepdims=True))

        def start_values(kvi):
            if has_sink:
                l_scr[kvi][...] = jnp.exp2(sink_row(kvi) - m_scr[kvi][...])
            else:
                l_scr[kvi][...] = jnp.zeros((1, width), F32)
            acc_scr[kvi][...] = jnp.zeros((HEAD_DIM, width), F32)

        def values(kvi, k_ref, vt_ref, r0, slab, srow, kc, where):
            p = jnp.exp2(s_scr[kvi][pl.ds(srow, kc), :] - m_scr[kvi][...])
            l_scr[kvi][...] += jnp.sum(p, axis=0, keepdims=True)
            acc_scr[kvi][...] += jnp.dot(vt_ref[slab, kvi * HEAD_DIM:(kvi + 1) * HEAD_DIM, :], p.astype(BF16),
                                         preferred_element_type=F32)

        def first_scores():
            start_scores(0)
            for_each_chunk(functools.partial(scores, 0, query_operand(q_ref[rows, :], 0)))

        def phase_a():
            if lookahead:
                pl.when(i == 0)(first_scores)
            else:
                first_scores()

        half = SCAN_TILE // 2

        def scan_first_half(c):
            _rglru_gates(xs_ref, wg_ref, br_ref, bi_ref, nsp_scr, a_scr, u_scr, c, slice(0, half))

        def scan_second_half(c):
            _rglru_gates(xs_ref, wg_ref, br_ref, bi_ref, nsp_scr, a_scr, u_scr, c, slice(half, SCAN_TILE))
            _rglru_scan(a_scr, u_scr, h_ref, carry_scr, c, scan == "bwd")

        def phase_b():
            start_scores(1)
            start_values(0)
            for_each_chunk(functools.partial(scores, 1, query_operand(q_ref[rows, :], 1)),
                           functools.partial(values, 0), along=scan_first_half if scan else None)

        def phase_c():
            start_values(1)
            if lookahead:
                start_scores(0)
                for_each_chunk(functools.partial(values, 1),
                               functools.partial(scores, 0, query_operand(qn_ref[...], 0)),
                               along=scan_second_half if scan else None)
            else:
                for_each_chunk(functools.partial(values, 1))

        def finish():
            outs = []
            for kvi in range(A_KV):
                o = acc_scr[kvi][...] / l_scr[kvi][...]
                outs += [o[:, hh * tq:(hh + 1) * tq] for hh in range(GROUP)]
            o_ref[rows, :] = jnp.concatenate(outs, axis=0).T.astype(BF16)

        return phase_a, phase_b, phase_c, finish

    if scan:
        nsp_scr[...] = _neg_softplus_neg(lam_ref[...])

        @pl.when(i == 0)
        def _():
            carry_scr[...] = h0_ref[...]

    for phases in zip(*[run_tile(sub) for sub in range(nsub)]):
        for phase in phases:
            phase()


def _attn_bounded_kernel(*refs, segs, n_src, tq, scan):
    refs = list(refs)
    q_ref = refs.pop(0)
    kv_refs = [(refs.pop(0), refs.pop(0)) for _ in range(n_src)]
    if scan:
        xs_ref, wg_ref, br_ref, bi_ref, lam_ref, h0_ref = [refs.pop(0) for _ in range(6)]
    o_ref = refs.pop(0)
    if scan:
        h_ref = refs.pop(0)
        carry_scr, nsp_scr = refs.pop(), refs.pop()
        gate_scr = [(refs.pop(), refs.pop()) for _ in range(2)]
    l_scr, acc_scr = refs[:A_KV], refs[A_KV:2 * A_KV]
    s_scr = [refs[2 * A_KV:3 * A_KV], refs[3 * A_KV:4 * A_KV]]
    i = pl.program_id(1)
    width = GROUP * tq

    if scan:
        nsp_scr[...] = _neg_softplus_neg(lam_ref[...])

        @pl.when(i == 0)
        def _():
            carry_scr[...] = h0_ref[...]

    qt = q_ref[...].T
    qz = []
    for kvi in range(A_KV):
        heads = jnp.concatenate(
            [qt[(GROUP * kvi + hh) * HEAD_DIM:(GROUP * kvi + hh + 1) * HEAD_DIM, :] for hh in range(GROUP)],
            axis=1)
        parts = [jnp.zeros((HEAD_DIM, width), BF16)] * A_KV
        parts[kvi] = heads
        qz.append(jnp.concatenate(parts, axis=0))
        l_scr[kvi][...] = jnp.zeros((1, width), F32)
        acc_scr[kvi][...] = jnp.zeros((HEAD_DIM, width), F32)

    def scores(k_ref, r0, kc, buf):
        k = k_ref[pl.ds(r0, kc), :]
        for kvi in range(A_KV):
            buf[kvi][0:kc, :] = jnp.dot(k, qz[kvi], preferred_element_type=F32)

    def values(vt_ref, slab, kc, buf):
        for kvi in range(A_KV):
            p = jnp.exp2(buf[kvi][0:kc, :])
            l_scr[kvi][...] += jnp.sum(p, axis=0, keepdims=True)
            acc_scr[kvi][...] += jnp.dot(vt_ref[slab, kvi * HEAD_DIM:(kvi + 1) * HEAD_DIM, :], p.astype(BF16),
                                         preferred_element_type=F32)

    def gates(c, buf):
        _rglru_gates(xs_ref, wg_ref, br_ref, bi_ref, nsp_scr, buf[0], buf[1], c, slice(None), to_chunk=True)

    def scan_chunk(c, buf):
        _rglru_scan(buf[0], buf[1], h_ref, carry_scr, c, scan == "bwd", from_chunk=True)

    (src0, _, nchunks, kc), rest = segs[0], segs[1:]
    k_ref, vt_ref = kv_refs[src0]
    assert nchunks % 2 == 0
    last = nchunks - 1
    scores(k_ref, 0, kc, s_scr[0])
    if scan:
        gates(0, gate_scr[0])

    def body(j, carry):
        even, odd = 2 * j, 2 * j + 1
        nxt = jnp.minimum(even + 2, last)
        scores(k_ref, pl.multiple_of(odd * kc, kc), kc, s_scr[1])
        values(vt_ref, even, kc, s_scr[0])
        if scan:
            gates(odd, gate_scr[1])
            scan_chunk(even, gate_scr[0])
        scores(k_ref, pl.multiple_of(nxt * kc, kc), kc, s_scr[0])
        values(vt_ref, odd, kc, s_scr[1])
        if scan:
            gates(nxt, gate_scr[0])
            scan_chunk(odd, gate_scr[1])
        return carry

    lax.fori_loop(0, nchunks // 2, body, 0, unroll=True)

    for src, where, nch, kcs in rest:
        assert where is None and nch <= STATIC_CHUNKS
        kr, vr = kv_refs[src]
        for ci in range(nch):
            scores(kr, ci * kcs, kcs, s_scr[ci % 2])
            values(vr, ci, kcs, s_scr[ci % 2])

    outs = []
    for kvi in range(A_KV):
        o = acc_scr[kvi][...] / l_scr[kvi][...]
        outs += [o[:, hh * tq:(hh + 1) * tq] for hh in range(GROUP)]
    o_ref[...] = jnp.concatenate(outs, axis=0).T.astype(BF16)


def _attention(q, kv_arrays, segs, sink, tq, nsub, name, lookahead=False, part=(0, 1), scan=None,
               bounded=False):
    assert not (lookahead and nsub != 1) and not (bounded and (lookahead or nsub != 1 or sink is not None))
    bsz, rows, _ = q.shape
    nb = rows // (tq * nsub * part[1])
    first = part[0] * nb
    has_sink = sink is not None
    in_specs, args = [], []
    if has_sink:
        in_specs.append(pl.BlockSpec(memory_space=pltpu.SMEM))
        args.append(sink)
    in_specs.append(pl.BlockSpec((None, tq * nsub, Q_COLS), lambda b, i: (b, first + i, 0)))
    args.append(q)
    if lookahead:
        in_specs.append(pl.BlockSpec((None, tq, Q_COLS), lambda b, i: (b, first + jnp.minimum(i + 1, nb - 1), 0)))
        args.append(q)
    for k, vt in kv_arrays:
        in_specs += [pl.BlockSpec((None,) + k.shape[1:], lambda b, i: (b, 0, 0)),
                     pl.BlockSpec((None,) + vt.shape[1:], lambda b, i: (b, 0, 0, 0))]
        args += [k, vt]
    width = GROUP * tq
    nk_total = sum(nchunks * kc for _, _, nchunks, kc in segs)
    per_head = [(nk_total, width), (1, width), (1, width), (HEAD_DIM, width)]
    if bounded:
        kc0 = segs[0][3]
        per_head = [(1, width), (HEAD_DIM, width), (kc0, width), (kc0, width)]
    out_specs = [pl.BlockSpec((None, tq * nsub, Q_COLS), lambda b, i: (b, i, 0))]
    out_shape = [jax.ShapeDtypeStruct((bsz, rows // part[1], Q_COLS), BF16)]
    scratch = [pltpu.VMEM(shape, F32) for _ in range(nsub) for shape in per_head for _ in range(A_KV)]
    direction = None
    if scan is not None:
        direction, xs, h0, wg, br, bi, lam = scan
        srows, swidth = xs.shape[1:]
        assert srows // SCAN_TILE == nb and swidth // LANES == segs[0][2]
        tile = pl.BlockSpec((None, SCAN_TILE, swidth),
                            (lambda b, i: (b, nb - 1 - i, 0)) if direction == "bwd" else (lambda b, i: (b, i, 0)))
        vec = _resident((1, swidth), lambda b, i: (0, 0))
        in_specs += [tile, _resident(wg.shape, lambda b, i: (0, 0, 0)), vec, vec, vec,
                     pl.BlockSpec((None, 1, swidth), lambda b, i: (b, 0, 0))]
        args += [xs, wg, br, bi, lam, h0]
        out_specs.append(tile)
        out_shape.append(jax.ShapeDtypeStruct(xs.shape, F32))
        if bounded:
            scratch += [pltpu.VMEM((SCAN_TILE, LANES), F32) for _ in range(4)]
            scratch += [pltpu.VMEM((1, swidth), F32), pltpu.VMEM((1, swidth), F32)]
        else:
            scratch += [pltpu.VMEM((1, swidth), F32), pltpu.VMEM((1, swidth), F32),
                        pltpu.VMEM((SCAN_TILE, swidth), F32), pltpu.VMEM((SCAN_TILE, swidth), F32)]
    if bounded:
        body = functools.partial(_attn_bounded_kernel, segs=tuple(segs), n_src=len(kv_arrays), tq=tq, scan=direction)
    else:
        body = functools.partial(_attn_kernel, segs=tuple(segs), n_src=len(kv_arrays), has_sink=has_sink, tq=tq,
                                 nsub=nsub, lookahead=lookahead, scan=direction)
    outs = pl.pallas_call(
        body,
        grid=(bsz, nb),
        in_specs=in_specs,
        out_specs=out_specs,
        out_shape=out_shape,
        scratch_shapes=scratch,
        compiler_params=_params("arbitrary", "arbitrary"),
        name=name,
    )(*args)
    return outs if scan is not None else outs[0]


def _whole_seg(src, vt):
    return (src, None, vt.shape[1], vt.shape[3])


def _attn_window(q, k, vt, k_ctx, vt_ctx, sink):
    assert vt.shape[3] == BLOCK
    segs = [(0, "cur", 1, BLOCK), (0, "prev", 1, BLOCK), (0, "next", 1, BLOCK), _whole_seg(1, vt_ctx)]
    return _attention(q, [(k, vt), (k_ctx, vt_ctx)], segs, sink, BLOCK, WINDOW_SUBTILES, "attn_window")


def _attn_global(q, kv_list, sink, tq, name, lookahead=False, part=(0, 1), scan=None, bounded=False):
    segs = [_whole_seg(src, vt) for src, (_, vt) in enumerate(kv_list)]
    return _attention(q, kv_list, segs, sink, tq, 1, name, lookahead, part, scan, bounded)


def _lane_chunk(c):
    if isinstance(c, int):
        return slice(c * LANES, (c + 1) * LANES)
    return pl.ds(pl.multiple_of(c * LANES, LANES), LANES)


def _neg_softplus_neg(lam):
    return -(jnp.maximum(-lam, 0.0) + jnp.log1p(jnp.exp(-jnp.abs(lam))))


def _rglru_gates(x_ref, wg_ref, br_ref, bi_ref, nsp_ref, a_scr, u_scr, c, rows, to_chunk=False):
    cs = _lane_chunk(c)
    dst = slice(None) if to_chunk else cs
    xcc = x_ref[rows, cs]
    g = jnp.dot(xcc.astype(BF16), wg_ref[c], preferred_element_type=F32)
    r = jax.nn.sigmoid(g[:, :LANES] + br_ref[:, cs])
    ig = jax.nn.sigmoid(g[:, LANES:] + bi_ref[:, cs])
    log_a = (LRU_C * r) * nsp_ref[:, cs]
    a = jnp.exp(log_a)
    w = -jnp.tanh(log_a) * (a * a + 1.0)
    a_scr[rows, dst] = a
    u_scr[rows, dst] = (w * lax.rsqrt(jnp.maximum(w, SQRT_FLOOR))) * (ig * xcc)


def _rglru_scan(a_scr, u_scr, h_ref, carry_scr, c, reverse, from_chunk=False):
    cs = _lane_chunk(c)
    src = slice(None) if from_chunk else cs
    ngroups = a_scr.shape[0] // SUBLANES
    sub = lax.broadcasted_iota(jnp.int32, (SUBLANES, LANES), 0)
    order = range(ngroups - 1, -1, -1) if reverse else range(ngroups)

    def rows(g):
        return slice(g * SUBLANES, (g + 1) * SUBLANES)

    h_loc = decay = None
    for gi in order:
        a_g, u_g = a_scr[rows(gi), src], u_scr[rows(gi), src]
        h_loc = u_g if h_loc is None else a_g * h_loc + u_g
        decay = a_g if decay is None else a_g * decay
        h_ref[rows(gi), cs] = h_loc
        a_scr[rows(gi), src] = decay

    end, dec = h_loc, decay
    for s in (1, 2, 4):
        if reverse:
            ok = sub < SUBLANES - s
            shift = SUBLANES - s
        else:
            ok = sub >= s
            shift = s
        end = end + dec * jnp.where(ok, pltpu.roll(end, shift, 0), 0.0)
        dec = dec * jnp.where(ok, pltpu.roll(dec, shift, 0), 1.0)
    carry = carry_scr[:, cs]
    seg_end = end + dec * carry
    if reverse:
        start = jnp.where(sub == SUBLANES - 1, carry, pltpu.roll(seg_end, SUBLANES - 1, 0))
        carry_scr[:, cs] = seg_end[0:1]
    else:
        start = jnp.where(sub == 0, carry, pltpu.roll(seg_end, 1, 0))
        carry_scr[:, cs] = seg_end[SUBLANES - 1:SUBLANES]
    for gi in order:
        h_ref[rows(gi), cs] = h_ref[rows(gi), cs] + a_scr[rows(gi), src] * start


def _rglru_kernel(x_ref, wg_ref, br_ref, bi_ref, lam_ref, h0_ref,
                  h_ref, hfin_ref, carry_scr, nsp_scr, a_scr, u_scr, *, reverse):
    j = pl.program_id(1)
    nt = pl.num_programs(1)

    @pl.when(j == 0)
    def _():
        carry_scr[...] = h0_ref[...]

    nsp_scr[...] = _neg_softplus_neg(lam_ref[...])

    for c in range(x_ref.shape[1] // LANES):
        _rglru_gates(x_ref, wg_ref, br_ref, bi_ref, nsp_scr, a_scr, u_scr, c, slice(None))
        _rglru_scan(a_scr, u_scr, h_ref, carry_scr, c, reverse)

    @pl.when(j == nt - 1)
    def _():
        hfin_ref[...] = carry_scr[...]


def _rglru(xs, h0, wg, br, bi, lam, reverse):
    bsz, rows, width = xs.shape
    nt = rows // SCAN_TILE

    def tpos(j):
        return nt - 1 - j if reverse else j

    vec = _resident((1, width), lambda b, j: (0, 0))
    tile = pl.BlockSpec((None, SCAN_TILE, width), lambda b, j: (b, tpos(j), 0))
    state = pl.BlockSpec((None, 1, width), lambda b, j: (b, 0, 0))
    return pl.pallas_call(
        functools.partial(_rglru_kernel, reverse=reverse),
        grid=(bsz, nt),
        in_specs=[tile, _resident(wg.shape, lambda b, j: (0, 0, 0)), vec, vec, vec, state],
        out_specs=[tile, state],
        out_shape=[jax.ShapeDtypeStruct((bsz, rows, width), F32),
                   jax.ShapeDtypeStruct((bsz, 1, width), F32)],
        scratch_shapes=[pltpu.VMEM((1, width), F32), pltpu.VMEM((1, width), F32),
                        pltpu.VMEM((SCAN_TILE, width), F32), pltpu.VMEM((SCAN_TILE, width), F32)],
        compiler_params=_params("arbitrary", "arbitrary"),
        name="rglru_bwd" if reverse else "rglru_fwd",
    )(xs, wg, br, bi, lam, h0)


def _merge_kernel(*refs, yb_parts):
    refs = list(refs)
    x_ref, sc_ref, sh_ref, gate_ref, ya_ref = [refs.pop(0) for _ in range(5)]
    yb_refs = [refs.pop(0) for _ in range(yb_parts)]
    (hf_ref, hb_ref, perm_ref, perm_t_ref, wy_ref, wg_ref, wa_ref, wb_ref, wc_ref, wo_ref,
     lng_ref, lnb_ref, o_ref) = refs
    per_part = pl.num_programs(1) // yb_parts
    yb = yb_refs[-1][...]
    for p in range(yb_parts - 2, -1, -1):
        yb = jnp.where(pl.program_id(1) < (p + 1) * per_part, yb_refs[p][...], yb)
    d = x_ref.shape[1]

    def scan_tile(k):
        rows = slice(k * SCAN_TILE, (k + 1) * SCAN_TILE)
        st = {}

        def project():
            st["x"] = x_ref[rows, :]
            st["h"] = _modulate(st["x"], sc_ref, sh_ref)
            hp = jnp.dot(perm_ref[...], st["h"], preferred_element_type=F32).astype(BF16)
            st["y_rnn"] = jnp.dot(hp, wy_ref[...], preferred_element_type=F32)
            st["zz"] = jnp.dot(st["h"], wg_ref[...], preferred_element_type=F32)

        def branches():
            yc = ((hf_ref[rows, :] + hb_ref[rows, :]) * jax.nn.gelu(st["y_rnn"])).astype(BF16)
            yc = jnp.dot(perm_t_ref[...], yc, preferred_element_type=F32).astype(BF16)
            zz = st["zz"]
            st["mix"] = (
                jax.nn.sigmoid(zz[:, 0:d]) * jnp.dot(ya_ref[rows, :], wa_ref[...], preferred_element_type=F32)
                + jax.nn.sigmoid(zz[:, d:2 * d]) * jnp.dot(yb[rows, :], wb_ref[...], preferred_element_type=F32)
                + jax.nn.sigmoid(zz[:, 2 * d:3 * d]) * jnp.dot(yc, wc_ref[...], preferred_element_type=F32))

        def finish():
            o = jnp.dot(st["mix"].astype(BF16), wo_ref[...], preferred_element_type=F32)
            o_ref[rows, :] = _layernorm(ALPHA * st["x"] + gate_ref[...] * o, lng_ref[...], lnb_ref[...])

        return project, branches, finish

    for stages in zip(*[scan_tile(k) for k in range(x_ref.shape[0] // SCAN_TILE)]):
        for stage in stages:
            stage()


def _merge(x, sc, sh, gate, ya, yb_parts, hf, hb, perm, perm_t, wy, wg, wa, wb, wc, wo, lng, lnb, tm):
    bsz, rows, d = x.shape
    per_part = rows // tm // len(yb_parts)

    def part_spec(p):
        return pl.BlockSpec((None, tm, Q_COLS), lambda b, i: (b, jnp.clip(i - p * per_part, 0, per_part - 1), 0))
    mod_spec = pl.BlockSpec((None, 1, d), lambda b, i: (b, 0, 0))
    vec = _resident((1, d), lambda b, i: (0, 0))

    def tile(cols):
        return pl.BlockSpec((None, tm, cols), lambda b, i: (b, i, 0))

    def weight(w):
        return _resident(w.shape, lambda b, i: (0, 0))

    weights = [perm, perm_t, wy, wg, wa, wb, wc, wo]
    return pl.pallas_call(
        functools.partial(_merge_kernel, yb_parts=len(yb_parts)),
        grid=(bsz, rows // tm),
        in_specs=[tile(d), mod_spec, mod_spec, mod_spec, tile(Q_COLS)]
                 + [part_spec(p) for p in range(len(yb_parts))] + [tile(d), tile(d)]
                 + [weight(w) for w in weights] + [vec, vec],
        out_specs=tile(d),
        out_shape=jax.ShapeDtypeStruct((bsz, rows, d), F32),
        compiler_params=_params("arbitrary", "arbitrary"),
        name="merge_ln",
    )(x, sc, sh, gate, ya, *yb_parts, hf, hb, *weights, lng, lnb)


def _mlp_kernel(x_ref, sc_ref, sh_ref, gate_ref, w1_ref, w2_ref, lng_ref, lnb_ref, o_ref):
    x = x_ref[...]
    h = _modulate(x, sc_ref, sh_ref)
    u = jnp.dot(h, w1_ref[...], preferred_element_type=F32)
    u = jnp.square(jnp.maximum(u, 0.0)).astype(BF16)
    o = jnp.dot(u, w2_ref[...], preferred_element_type=F32)
    o_ref[...] = _layernorm(ALPHA * x + gate_ref[...] * o, lng_ref[...], lnb_ref[...])


def _mlp(x, sc, sh, gate, w1, w2, lng, lnb, tm):
    bsz, rows, d = x.shape
    mod_spec = pl.BlockSpec((None, 1, d), lambda b, i: (b, 0, 0))
    vec = _resident((1, d), lambda b, i: (0, 0))
    tile = pl.BlockSpec((None, tm, d), lambda b, i: (b, i, 0))
    return pl.pallas_call(
        _mlp_kernel,
        grid=(bsz, rows // tm),
        in_specs=[tile, mod_spec, mod_spec, mod_spec,
                  _resident(w1.shape, lambda b, i: (0, 0)), _resident(w2.shape, lambda b, i: (0, 0)), vec, vec],
        out_specs=tile,
        out_shape=jax.ShapeDtypeStruct((bsz, rows, d), F32),
        compiler_params=_params("arbitrary", "arbitrary"),
        name="mlp_ln",
    )(x, sc, sh, gate, w1, w2, lng, lnb)


def _rope_tables(n):
    pos = jnp.arange(n)
    row = (pos // GRID_W).astype(F32)
    colp = (pos % GRID_W).astype(F32)
    nf = HEAD_DIM // 4
    inv = ROPE_BASE ** (-jnp.arange(nf, dtype=F32) / nf)
    ang_r = row[:, None] * inv
    ang_c = colp[:, None] * inv
    cos_t = jnp.concatenate([jnp.cos(ang_r), jnp.cos(ang_r), jnp.cos(ang_c), jnp.cos(ang_c)], axis=-1)
    sin_t = jnp.concatenate([-jnp.sin(ang_r), jnp.sin(ang_r), -jnp.sin(ang_c), jnp.sin(ang_c)], axis=-1)
    return jnp.tile(cos_t, (1, 2)), jnp.tile(sin_t, (1, 2))


def _gate_weights(w_r, w_i):
    def pair(w):
        w = w.reshape(LRU_BLOCKS // 2, 2, LRU_BLOCK_DIM, LRU_BLOCK_DIM)
        z = jnp.zeros_like(w[:, 0])
        top = jnp.concatenate([w[:, 0], z], axis=-1)
        bot = jnp.concatenate([z, w[:, 1]], axis=-1)
        return jnp.concatenate([top, bot], axis=-2)
    return jnp.concatenate([pair(w_r), pair(w_i)], axis=-1).astype(BF16)


def kernel(x, c, ctx, c_ctx, w_ada, b_ada, w_in, a_sink, b_q_gain, b_k_gain, c_conv_w, c_conv_b,
           c_wr, c_br, c_wi, c_bi, c_lam, w_br_a, w_br_b, w_br_c, w_out, ln1_g, ln1_b,
           w_ff1, w_ff2, ln2_g, ln2_b):
    bsz, n, d = x.shape
    m = ctx.shape[1]
    depth = w_ada.shape[0]

    cc = jnp.zeros((SUBLANES, d), F32).at[:bsz].set(c).at[bsz].set(c_ctx)
    mod = _ada(cc, w_ada, b_ada)
    rope_tabs = _rope_tables(n)
    blk = jnp.arange(LANES) // HEAD_DIM
    ones = (blk[:, None] == blk[None, :]).astype(BF16)
    h_zero = jnp.zeros((bsz, 1, d), F32)
    slot = jnp.arange(SCAN_TILE)
    token_of_slot = SEG_LEN * (slot % SUBLANES) + slot // SUBLANES
    perm = (token_of_slot[:, None] == slot[None, :]).astype(BF16)
    perm_t = perm.T

    for l in range(depth):
        with_ctx = l < depth - 1
        mod_lat = mod[l, :bsz].reshape(bsz, 1, 6 * d)
        mod_ctx = jnp.broadcast_to(mod[l, bsz].reshape(1, 1, 6 * d), (bsz, 1, 6 * d))
        sh1, sc1, g1, sh2, sc2, g2 = [mod_lat[:, :, k * d:(k + 1) * d] for k in range(6)]
        csh1, csc1, cg1, csh2, csc2, cg2 = [mod_ctx[:, :, k * d:(k + 1) * d] for k in range(6)]

        w_attn = w_in[l, :, :ATTN_COLS].astype(BF16)
        w_rnn = w_in[l, :, ATTN_COLS:ATTN_COLS + LRU_WIDTH].astype(BF16)
        w_y = w_in[l, :, ATTN_COLS + LRU_WIDTH:ATTN_COLS + 2 * LRU_WIDTH].astype(BF16)
        w_g = w_in[l, :, ATTN_COLS + 2 * LRU_WIDTH:].astype(BF16)
        logit_bound = (HEAD_DIM * HEAD_DIM ** -0.5 * LOG2_E) * jnp.max(jnp.abs(b_q_gain[l])) * jnp.max(
            jnp.abs(b_k_gain[l]))
        gq = jnp.tile(b_q_gain[l], 2).reshape(1, LANES)
        gk = jnp.tile(b_k_gain[l], 2).reshape(1, LANES)

        cb = c_conv_b[l].reshape(1, d)
        conv = (perm, c_conv_w[l], cb)
        qa, ka, va, qb, kb, vb, xs = _inproj(x, sc1, sh1, w_attn, w_rnn, gq, gk, ones, *conv, rope_tabs, ROW_TILE)
        qa_c, ka_c, va_c, qb_c, kb_c, vb_c, xs_c = _inproj(ctx, csc1, csh1, w_attn, w_rnn, gq, gk, ones, *conv,
                                                           None, m)

        ya = _attn_window(qa, ka, va, ka_c, va_c, a_sink[l])
        hs, hs_c, yb_parts = [], [], []
        for di, direction in enumerate(("fwd", "bwd")):
            wg = _gate_weights(c_wr[l, di], c_wi[l, di])
            vecs = [v[l, di].reshape(1, d) for v in (c_br, c_bi, c_lam)]
            h_c, h_fin = _rglru(xs_c, h_zero, wg, *vecs, direction == "bwd")
            scan = (direction, xs, h_fin, wg, *vecs)
            kv_b = [(kb, vb), (kb_c, vb_c)]
            yb_part, h_l = lax.cond(
                logit_bound <= LOGIT_RANGE,
                lambda: _attn_global(qb, kv_b, None, BLOCK, "attn_bounded_" + direction,
                                     part=(di, 2), scan=scan, bounded=True),
                lambda: _attn_global(qb, kv_b, None, BLOCK, "attn_global_" + direction,
                                     lookahead=True, part=(di, 2), scan=scan))
            yb_parts.append(yb_part)
            hs.append(h_l)
            hs_c.append(h_c)

        merge_w = [perm, perm_t, w_y, w_g] + [w[l].astype(BF16) for w in (w_br_a, w_br_b, w_br_c, w_out)]
        ln1 = [ln1_g[l].reshape(1, d), ln1_b[l].reshape(1, d)]
        ln2 = [ln2_g[l].reshape(1, d), ln2_b[l].reshape(1, d)]
        w1 = w_ff1[l].astype(BF16)
        w2 = w_ff2[l].astype(BF16)

        x = _merge(x, sc1, sh1, g1, ya, yb_parts, hs[0], hs[1], *merge_w, *ln1, ROW_TILE)
        x = _mlp(x, sc2, sh2, g2, w1, w2, *ln2, ROW_TILE)
        if with_ctx:
            ya_c = _attn_global(qa_c, [(ka_c, va_c)], a_sink[l], BLOCK, "attn_ctx_a")
            yb_c = _attn_global(qb_c, [(kb_c, vb_c)], None, BLOCK, "attn_ctx_b")
            ctx = _merge(ctx, csc1, csh1, cg1, ya_c, [yb_c], hs_c[0], hs_c[1], *merge_w, *ln1, m)
            ctx = _mlp(ctx, csc2, csh2, cg2, w1, w2, *ln2, m)
    return x
```

```python
import functools

import jax
import jax.numpy as jnp
from jax import lax
from jax.experimental import pallas as pl
from jax.experimental.pallas import tpu as pltpu

D_MODEL = 1024
DEPTH = 2
GRID_W = 64
HEAD_DIM = 64
ROPE_BASE = 10000.0
BLOCK = 128
A_HEADS = 8
A_KV = 2
B_HEADS = 8
B_KV = 2
LRU_WIDTH = D_MODEL
LRU_BLOCKS = 16
LRU_BLOCK_DIM = LRU_WIDTH // LRU_BLOCKS
CONV_W = 4
LRU_C = 8.0
D_FF = 4 * D_MODEL
ALPHA = (2 * DEPTH) ** 0.25
LN_EPS = 1e-5
RMS_EPS = 1e-6
NEG_INF = -1e30
LOG2_E = 1.4426950408889634
Q_COLS = A_HEADS * HEAD_DIM
KV_COLS = A_KV * HEAD_DIM
ATTN_COLS = 2 * (Q_COLS + 2 * KV_COLS)
GROUP = A_HEADS // A_KV

LANES = 128
SUBLANES = 8
V7X_VMEM_BYTES = 64 * 1024 * 1024
VMEM_LIMIT_BYTES = V7X_VMEM_BYTES - 8 * 1024 * 1024

SCAN_TILE = 256
SEG_LEN = SCAN_TILE // SUBLANES
ROW_TILE = 2 * SCAN_TILE
LOGIT_RANGE = 100.0
STATIC_CHUNKS = 4
BOUNDED_TILES = 2
WINDOW_SUBTILES = 8
SQRT_FLOOR = 1e-30

F32 = jnp.float32
BF16 = jnp.bfloat16


def _params(*semantics):
    return pltpu.CompilerParams(dimension_semantics=semantics, vmem_limit_bytes=VMEM_LIMIT_BYTES)


def _resident(block_shape, index_map):
    return pl.BlockSpec(block_shape, index_map, pipeline_mode=pl.Buffered(1))


def _modulate(x, sc_ref, sh_ref):
    return (x * (1.0 + sc_ref[...]) + sh_ref[...]).astype(BF16)


def _layernorm(y, g, b):
    mu = jnp.mean(y, axis=-1, keepdims=True)
    yc = y - mu
    var = jnp.mean(yc * yc, axis=-1, keepdims=True)
    return yc * lax.rsqrt(var + LN_EPS) * g + b


def _ada_kernel(c_ref, w_ref, b_ref, o_ref):
    c = c_ref[...]
    s = c * jax.nn.sigmoid(c)
    o_ref[...] = jnp.dot(s, w_ref[...], preferred_element_type=F32,
                         precision=lax.Precision.HIGHEST) + b_ref[...]


def _ada(cc, w_ada, b_ada):
    depth, d, cols = w_ada.shape
    tn = cols // 4
    return pl.pallas_call(
        _ada_kernel,
        grid=(depth, cols // tn),
        in_specs=[pl.BlockSpec((SUBLANES, d), lambda l, j: (0, 0)),
                  pl.BlockSpec((None, d, tn), lambda l, j: (l, 0, j)),
                  pl.BlockSpec((None, 1, tn), lambda l, j: (l, 0, j))],
        out_specs=pl.BlockSpec((None, SUBLANES, tn), lambda l, j: (l, 0, j)),
        out_shape=jax.ShapeDtypeStruct((depth, SUBLANES, cols), F32),
        compiler_params=_params("arbitrary", "arbitrary"),
        name="ada_mod",
    )(cc, w_ada, b_ada.reshape(depth, 1, cols))


def _inproj_kernel(*refs, rope):
    refs = list(refs)
    (xp_ref, x_ref, xn_ref, sc_ref, sh_ref, wa_ref, wr_ref, gq_ref, gk_ref, ones_ref, perm_ref,
     cw_ref, cb_ref) = [refs.pop(0) for _ in range(13)]
    if rope:
        cos_ref, sin_ref = refs.pop(0), refs.pop(0)
    qa_ref, ka_ref, va_ref, qb_ref, kb_ref, vb_ref, xs_ref = refs
    i = pl.program_id(1)
    nt = pl.num_programs(1)
    h = _modulate(x_ref[...], sc_ref, sh_ref)
    tm = h.shape[0]

    z = jnp.dot(h, wa_ref[...], preferred_element_type=F32)
    lane = lax.broadcasted_iota(jnp.int32, (tm, LANES), 1)
    first16 = (lane & 31) < 16
    ones = ones_ref[...]
    scale = HEAD_DIM ** -0.5 * LOG2_E

    def rot(zc):
        if not rope:
            return zc
        partner = jnp.where(first16, pltpu.roll(zc, LANES - 16, 1), pltpu.roll(zc, 16, 1))
        return zc * cos_ref[...] + partner * sin_ref[...]

    def rms(zc, g_ref):
        sq = zc * zc
        hi = sq.astype(BF16)
        lo = (sq - hi.astype(F32)).astype(BF16)
        ss = (jnp.dot(hi, ones, preferred_element_type=F32)
              + jnp.dot(lo, ones, preferred_element_type=F32))
        return zc * lax.rsqrt(ss * (1.0 / HEAD_DIM) + RMS_EPS) * g_ref[...]

    def col(start):
        return z[:, start:start + LANES]

    for c in range(Q_COLS // LANES):
        qa_ref[:, c * LANES:(c + 1) * LANES] = (rot(col(c * LANES)) * scale).astype(BF16)
    base = Q_COLS
    ka_ref[...] = rot(col(base)).astype(BF16)
    va_t = col(base + KV_COLS).T.astype(BF16)
    for c in range(va_ref.shape[0]):
        va_ref[c] = va_t[:, c * BLOCK:(c + 1) * BLOCK]
    base = Q_COLS + 2 * KV_COLS
    for c in range(Q_COLS // LANES):
        qb_ref[:, c * LANES:(c + 1) * LANES] = (
            rot(rms(col(base + c * LANES), gq_ref)) * scale).astype(BF16)
    base += Q_COLS
    kb_ref[...] = rot(rms(col(base), gk_ref)).astype(BF16)
    vb_ref[0] = col(base + KV_COLS).T.astype(BF16)

    ntile = tm // SCAN_TILE
    hp = [jnp.dot(perm_ref[...], h[k * SCAN_TILE:(k + 1) * SCAN_TILE], preferred_element_type=F32).astype(BF16)
          for k in range(ntile)]
    halo = _modulate(jnp.concatenate([xp_ref[...], xn_ref[...]], axis=0), sc_ref, sh_ref)
    zall = jnp.dot(jnp.concatenate(hp + [halo], axis=0), wr_ref[...], preferred_element_type=F32)
    zs = [zall[k * SCAN_TILE:(k + 1) * SCAN_TILE] for k in range(ntile)]
    edge_prev = jnp.where(i > 0, zall[tm:tm + SUBLANES], 0.0)
    edge_next = jnp.where(i < nt - 1, zall[tm + SUBLANES:], 0.0)
    sub = lax.broadcasted_iota(jnp.int32, (SUBLANES, zall.shape[1]), 0)
    last_row = SCAN_TILE - 1
    for k, zk in enumerate(zs):
        tok_before = edge_prev[SUBLANES - 1:SUBLANES] if k == 0 else zs[k - 1][last_row:last_row + 1]
        tok_after1 = edge_next[0:1] if k == ntile - 1 else zs[k + 1][0:1]
        tok_after2 = edge_next[1:2] if k == ntile - 1 else zs[k + 1][SUBLANES:SUBLANES + 1]
        first, second, last = zk[0:SUBLANES], zk[SUBLANES:2 * SUBLANES], zk[SCAN_TILE - SUBLANES:]
        before = jnp.where(sub == 0, tok_before, pltpu.roll(last, 1, 0))
        after1 = jnp.where(sub == SUBLANES - 1, tok_after1, pltpu.roll(first, SUBLANES - 1, 0))
        after2 = jnp.where(sub == SUBLANES - 1, tok_after2, pltpu.roll(second, SUBLANES - 1, 0))
        taps = [jnp.concatenate([before, zk[:SCAN_TILE - SUBLANES]], axis=0),
                zk,
                jnp.concatenate([zk[SUBLANES:], after1], axis=0),
                jnp.concatenate([zk[2 * SUBLANES:], after1, after2], axis=0)]
        out = cb_ref[...]
        for tap in range(CONV_W):
            out = out + cw_ref[tap:tap + 1, :] * taps[tap]
        xs_ref[k * SCAN_TILE:(k + 1) * SCAN_TILE, :] = out


def _inproj(x, sc, sh, w_attn, w_rnn, gq, gk, ones, perm, cw, cb, rope_tabs, tm):
    bsz, rows, d = x.shape
    rope = rope_tabs is not None
    hb = tm // SUBLANES
    mod_spec = pl.BlockSpec((None, 1, d), lambda b, i: (b, 0, 0))
    vec_spec = _resident((1, LANES), lambda b, i: (0, 0))
    rnn_cols = w_rnn.shape[1]
    in_specs = [pl.BlockSpec((None, SUBLANES, d), lambda b, i: (b, jnp.maximum(i * hb - 1, 0), 0)),
                pl.BlockSpec((None, tm, d), lambda b, i: (b, i, 0)),
                pl.BlockSpec((None, SUBLANES, d),
                             lambda b, i: (b, jnp.minimum((i + 1) * hb, rows // SUBLANES - 1), 0)),
                mod_spec, mod_spec,
                _resident((d, ATTN_COLS), lambda b, i: (0, 0)), _resident((d, rnn_cols), lambda b, i: (0, 0)),
                vec_spec, vec_spec, _resident((LANES, LANES), lambda b, i: (0, 0)),
                _resident((SCAN_TILE, SCAN_TILE), lambda b, i: (0, 0)),
                _resident((CONV_W, rnn_cols), lambda b, i: (0, 0)), _resident((1, rnn_cols), lambda b, i: (0, 0))]
    args = [x, x, x, sc, sh, w_attn, w_rnn, gq, gk, ones, perm, cw, cb]
    if rope:
        tab_spec = pl.BlockSpec((tm, LANES), lambda b, i: (i, 0))
        in_specs += [tab_spec, tab_spec]
        args += list(rope_tabs)

    def out(cols, dtype=BF16):
        return (pl.BlockSpec((None, tm, cols), lambda b, i: (b, i, 0)),
                jax.ShapeDtypeStruct((bsz, rows, cols), dtype))

    def out_t(slab):
        return (pl.BlockSpec((None, tm // slab, KV_COLS, slab), lambda b, i: (b, i, 0, 0)),
                jax.ShapeDtypeStruct((bsz, rows // slab, KV_COLS, slab), BF16))

    outs = [out(Q_COLS), out(KV_COLS), out_t(BLOCK), out(Q_COLS), out(KV_COLS), out_t(tm), out(rnn_cols, F32)]
    return pl.pallas_call(
        functools.partial(_inproj_kernel, rope=rope),
        grid=(bsz, rows // tm),
        in_specs=in_specs,
        out_specs=[o[0] for o in outs],
        out_shape=[o[1] for o in outs],
        compiler_params=_params("arbitrary", "arbitrary"),
        name="inproj_rope" if rope else "inproj_ctx",
    )(*args)


def _attn_kernel(*refs, segs, n_src, has_sink, tq, nsub, lookahead, scan):
    refs = list(refs)
    sink_ref = refs.pop(0) if has_sink else None
    q_ref = refs.pop(0)
    qn_ref = refs.pop(0) if lookahead else None
    kv_refs = [(refs.pop(0), refs.pop(0)) for _ in range(n_src)]
    if scan:
        xs_ref, wg_ref, br_ref, bi_ref, lam_ref, h0_ref = [refs.pop(0) for _ in range(6)]
    o_ref = refs.pop(0)
    if scan:
        h_ref = refs.pop(0)
        u_scr, a_scr, nsp_scr, carry_scr = [refs.pop() for _ in range(4)]
    i = pl.program_id(1)
    nblk = pl.num_programs(1) * nsub
    width = GROUP * tq

    def sink_row(kvi):
        return jnp.concatenate(
            [jnp.full((1, tq), sink_ref[GROUP * kvi + hh] * LOG2_E, F32) for hh in range(GROUP)], axis=1)

    def query_operand(q, kvi):
        qt = q.T
        heads = jnp.concatenate(
            [qt[(GROUP * kvi + hh) * HEAD_DIM:(GROUP * kvi + hh + 1) * HEAD_DIM, :] for hh in range(GROUP)],
            axis=1)
        parts = [jnp.zeros((HEAD_DIM, width), BF16)] * A_KV
        parts[kvi] = heads
        return jnp.concatenate(parts, axis=0)

    def run_tile(sub):
        mine = refs[sub * 4 * A_KV:(sub + 1) * 4 * A_KV]
        s_scr, m_scr, l_scr, acc_scr = [mine[g * A_KV:(g + 1) * A_KV] for g in range(4)]
        pos = i * nsub + sub
        rows = slice(sub * tq, (sub + 1) * tq)

        def seg_mask(where):
            if where not in ("prev", "next"):
                return None
            j = lax.broadcasted_iota(jnp.int32, (BLOCK, width), 0)
            r = lax.broadcasted_iota(jnp.int32, (BLOCK, width), 1) & (tq - 1)
            far = jnp.int32(4 * BLOCK)
            if where == "prev":
                return (j - r) >= jnp.where(pos > 0, 0, far)
            return (r - j) >= jnp.where(pos < nblk - 1, 0, far)

        def block_of(where):
            return {"cur": pos, "prev": jnp.maximum(pos - 1, 0), "next": jnp.minimum(pos + 1, nblk - 1)}[where]

        def for_each_chunk(*fns, along=None):
            base = 0
            for src, where, nchunks, kc in segs:
                k_ref, vt_ref = kv_refs[src]
                if where is None:
                    row0, slab0 = 0, 0
                else:
                    slab0 = block_of(where)
                    row0 = pl.multiple_of(slab0 * BLOCK, BLOCK)
                if nchunks <= STATIC_CHUNKS:
                    for ci in range(nchunks):
                        for fn in fns:
                            fn(k_ref, vt_ref, row0 + ci * kc, slab0 + ci, base + ci * kc, kc, where)
                else:
                    def body(ci, carry, k_ref=k_ref, vt_ref=vt_ref, kc=kc, base=base):
                        r0 = pl.multiple_of(ci * kc, kc)
                        for fn in fns:
                            fn(k_ref, vt_ref, r0, ci, base + r0, kc, None)
                        if along is not None:
                            along(ci)
                        return carry
                    lax.fori_loop(0, nchunks, body, 0, unroll=4)
                base += nchunks * kc

        def start_scores(kvi):
            m_scr[kvi][...] = sink_row(kvi) if has_sink else jnp.full((1, width), NEG_INF, F32)

        def scores(kvi, qz, k_ref, vt_ref, r0, slab, srow, kc, where):
            s = jnp.dot(k_ref[pl.ds(r0, kc), :], qz, preferred_element_type=F32)
            mask = seg_mask(where)
            if mask is not None:
                s = jnp.where(mask, s, NEG_INF)
            s_scr[kvi][pl.ds(srow, kc), :] = s
            m_scr[kvi][...] = jnp.maximum(m_scr[kvi][...], jnp.max(s, axis=0, keepdims=True))

        def start_values(kvi):
            if has_sink:
                l_scr[kvi][...] = jnp.exp2(sink_row(kvi) - m_scr[kvi][...])
            else:
                l_scr[kvi][...] = jnp.zeros((1, width), F32)
            acc_scr[kvi][...] = jnp.zeros((HEAD_DIM, width), F32)

        def values(kvi, k_ref, vt_ref, r0, slab, srow, kc, where):
            p = jnp.exp2(s_scr[kvi][pl.ds(srow, kc), :] - m_scr[kvi][...])
            l_scr[kvi][...] += jnp.sum(p, axis=0, keepdims=True)
            acc_scr[kvi][...] += jnp.dot(vt_ref[slab, kvi * HEAD_DIM:(kvi + 1) * HEAD_DIM, :], p.astype(BF16),
                                         preferred_element_type=F32)

        def first_scores():
            start_scores(0)
            for_each_chunk(functools.partial(scores, 0, query_operand(q_ref[rows, :], 0)))

        def phase_a():
            if lookahead:
                pl.when(i == 0)(first_scores)
            else:
                first_scores()

        half = SCAN_TILE // 2

        def scan_first_half(c):
            _rglru_gates(xs_ref, wg_ref, br_ref, bi_ref, nsp_scr, a_scr, u_scr, c, slice(0, half))

        def scan_second_half(c):
            _rglru_gates(xs_ref, wg_ref, br_ref, bi_ref, nsp_scr, a_scr, u_scr, c, slice(half, SCAN_TILE))
            _rglru_scan(a_scr, u_scr, h_ref, carry_scr, c, scan == "bwd")

        def phase_b():
            start_scores(1)
            start_values(0)
            for_each_chunk(functools.partial(scores, 1, query_operand(q_ref[rows, :], 1)),
                           functools.partial(values, 0), along=scan_first_half if scan else None)

        def phase_c():
            start_values(1)
            if lookahead:
                start_scores(0)
                for_each_chunk(functools.partial(values, 1),
                               functools.partial(scores, 0, query_operand(qn_ref[...], 0)),
                               along=scan_second_half if scan else None)
            else:
                for_each_chunk(functools.partial(values, 1))

        def finish():
            outs = []
            for kvi in range(A_KV):
                o = acc_scr[kvi][...] / l_scr[kvi][...]
                outs += [o[:, hh * tq:(hh + 1) * tq] for hh in range(GROUP)]
            o_ref[rows, :] = jnp.concatenate(outs, axis=0).T.astype(BF16)

        return phase_a, phase_b, phase_c, finish

    if scan:
        nsp_scr[...] = _neg_softplus_neg(lam_ref[...])

        @pl.when(i == 0)
        def _():
            carry_scr[...] = h0_ref[...]

    for phases in zip(*[run_tile(sub) for sub in range(nsub)]):
        for phase in phases:
            phase()


def _attn_bounded_kernel(*refs, segs, n_src, tq, tiles, scan):
    refs = list(refs)
    q_ref = refs.pop(0)
    kv_refs = [(refs.pop(0), refs.pop(0)) for _ in range(n_src)]
    if scan:
        xs_ref, wg_ref, br_ref, bi_ref, lam_ref, h0_ref = [refs.pop(0) for _ in range(6)]
    o_ref = refs.pop(0)
    if scan:
        h_ref = refs.pop(0)
        carry_scr, nsp_scr = refs.pop(), refs.pop()
        gate_scr = [(refs.pop(), refs.pop()) for _ in range(2)]
    i = pl.program_id(1)
    width = GROUP * tq

    if scan:
        nsp_scr[...] = _neg_softplus_neg(lam_ref[...])

        @pl.when(i == 0)
        def _():
            carry_scr[...] = h0_ref[...]

    def run_tile(t):
        mine = refs[t * 4 * A_KV:(t + 1) * 4 * A_KV]
        l_scr, acc_scr = mine[:A_KV], mine[A_KV:2 * A_KV]
        s_scr = [mine[2 * A_KV:3 * A_KV], mine[3 * A_KV:4 * A_KV]]
        qrows = slice(t * tq, (t + 1) * tq)
        if scan:
            xs_t = xs_ref.at[t * SCAN_TILE:(t + 1) * SCAN_TILE, :]
            h_t = h_ref.at[t * SCAN_TILE:(t + 1) * SCAN_TILE, :]
        qz = []

        def scores(k_ref, r0, kc, buf):
            k = k_ref[pl.ds(r0, kc), :]
            for kvi in range(A_KV):
                buf[kvi][0:kc, :] = jnp.dot(k, qz[kvi], preferred_element_type=F32)

        def values(vt_ref, slab, kc, buf):
            for kvi in range(A_KV):
                p = jnp.exp2(buf[kvi][0:kc, :])
                l_scr[kvi][...] += jnp.sum(p, axis=0, keepdims=True)
                acc_scr[kvi][...] += jnp.dot(vt_ref[slab, kvi * HEAD_DIM:(kvi + 1) * HEAD_DIM, :], p.astype(BF16),
                                             preferred_element_type=F32)

        def gates(c, buf):
            _rglru_gates(xs_t, wg_ref, br_ref, bi_ref, nsp_scr, buf[0], buf[1], c, slice(None), to_chunk=True)

        def scan_chunk(c, buf):
            _rglru_scan(buf[0], buf[1], h_t, carry_scr, c, scan == "bwd", from_chunk=True)

        (src0, _, nchunks, kc), rest = segs[0], segs[1:]
        k_ref, vt_ref = kv_refs[src0]
        assert nchunks % 2 == 0
        last = nchunks - 1

        def head():
            qt = q_ref[qrows, :].T
            for kvi in range(A_KV):
                heads = jnp.concatenate(
                    [qt[(GROUP * kvi + hh) * HEAD_DIM:(GROUP * kvi + hh + 1) * HEAD_DIM, :]
                     for hh in range(GROUP)], axis=1)
                parts = [jnp.zeros((HEAD_DIM, width), BF16)] * A_KV
                parts[kvi] = heads
                qz.append(jnp.concatenate(parts, axis=0))
                l_scr[kvi][...] = jnp.zeros((1, width), F32)
                acc_scr[kvi][...] = jnp.zeros((HEAD_DIM, width), F32)
            scores(k_ref, 0, kc, s_scr[0])

        def main():
            if scan:
                gates(0, gate_scr[0])
            for j in range(nchunks // 2):
                even, odd = 2 * j, 2 * j + 1
                nxt = min(even + 2, last)
                scores(k_ref, odd * kc, kc, s_scr[1])
                values(vt_ref, even, kc, s_scr[0])
                if scan:
                    gates(odd, gate_scr[1])
                    scan_chunk(even, gate_scr[0])
                scores(k_ref, nxt * kc, kc, s_scr[0])
                values(vt_ref, odd, kc, s_scr[1])
                if scan:
                    gates(nxt, gate_scr[0])
                    scan_chunk(odd, gate_scr[1])
            for src, where, nch, kcs in rest:
                assert where is None and nch <= STATIC_CHUNKS
                kr, vr = kv_refs[src]
                for ci in range(nch):
                    scores(kr, ci * kcs, kcs, s_scr[ci % 2])
                    values(vr, ci, kcs, s_scr[ci % 2])

        def tail():
            outs = []
            for kvi in range(A_KV):
                o = acc_scr[kvi][...] / l_scr[kvi][...]
                outs += [o[:, hh * tq:(hh + 1) * tq] for hh in range(GROUP)]
            o_ref[qrows, :] = jnp.concatenate(outs, axis=0).T.astype(BF16)

        return head, main, tail

    order = range(tiles - 1, -1, -1) if scan == "bwd" else range(tiles)
    pending_tail = None
    for t in order:
        head, main, tail = run_tile(t)
        head()
        if pending_tail is not None:
            pending_tail()
        main()
        pending_tail = tail
    pending_tail()


def _attention(q, kv_arrays, segs, sink, tq, nsub, name, lookahead=False, part=(0, 1), scan=None,
               bounded=False):
    assert not (lookahead and nsub != 1) and not (bounded and (lookahead or sink is not None))
    bsz, rows, _ = q.shape
    nb = rows // (tq * nsub * part[1])
    first = part[0] * nb
    has_sink = sink is not None
    in_specs, args = [], []
    if has_sink:
        in_specs.append(pl.BlockSpec(memory_space=pltpu.SMEM))
        args.append(sink)
    in_specs.append(pl.BlockSpec((None, tq * nsub, Q_COLS), lambda b, i: (b, first + i, 0)))
    args.append(q)
    if lookahead:
        in_specs.append(pl.BlockSpec((None, tq, Q_COLS), lambda b, i: (b, first + jnp.minimum(i + 1, nb - 1), 0)))
        args.append(q)
    for k, vt in kv_arrays:
        in_specs += [pl.BlockSpec((None,) + k.shape[1:], lambda b, i: (b, 0, 0)),
                     pl.BlockSpec((None,) + vt.shape[1:], lambda b, i: (b, 0, 0, 0))]
        args += [k, vt]
    width = GROUP * tq
    nk_total = sum(nchunks * kc for _, _, nchunks, kc in segs)
    per_head = [(nk_total, width), (1, width), (1, width), (HEAD_DIM, width)]
    if bounded:
        kc0 = segs[0][3]
        per_head = [(1, width), (HEAD_DIM, width), (kc0, width), (kc0, width)]
    out_specs = [pl.BlockSpec((None, tq * nsub, Q_COLS), lambda b, i: (b, i, 0))]
    out_shape = [jax.ShapeDtypeStruct((bsz, rows // part[1], Q_COLS), BF16)]
    scratch = [pltpu.VMEM(shape, F32) for _ in range(nsub) for shape in per_head for _ in range(A_KV)]
    direction = None
    if scan is not None:
        direction, xs, h0, wg, br, bi, lam = scan
        srows, swidth = xs.shape[1:]
        scan_rows = SCAN_TILE * (nsub if bounded else 1)
        assert srows // scan_rows == nb and swidth // LANES == segs[0][2]
        tile = pl.BlockSpec((None, scan_rows, swidth),
                            (lambda b, i: (b, nb - 1 - i, 0)) if direction == "bwd" else (lambda b, i: (b, i, 0)))
        vec = _resident((1, swidth), lambda b, i: (0, 0))
        in_specs += [tile, _resident(wg.shape, lambda b, i: (0, 0, 0)), vec, vec, vec,
                     pl.BlockSpec((None, 1, swidth), lambda b, i: (b, 0, 0))]
        args += [xs, wg, br, bi, lam, h0]
        out_specs.append(tile)
        out_shape.append(jax.ShapeDtypeStruct(xs.shape, F32))
        if bounded:
            scratch += [pltpu.VMEM((SCAN_TILE, LANES), F32) for _ in range(4)]
            scratch += [pltpu.VMEM((1, swidth), F32), pltpu.VMEM((1, swidth), F32)]
        else:
            scratch += [pltpu.VMEM((1, swidth), F32), pltpu.VMEM((1, swidth), F32),
                        pltpu.VMEM((SCAN_TILE, swidth), F32), pltpu.VMEM((SCAN_TILE, swidth), F32)]
    if bounded:
        body = functools.partial(_attn_bounded_kernel, segs=tuple(segs), n_src=len(kv_arrays), tq=tq, tiles=nsub,
                                 scan=direction)
    else:
        body = functools.partial(_attn_kernel, segs=tuple(segs), n_src=len(kv_arrays), has_sink=has_sink, tq=tq,
                                 nsub=nsub, lookahead=lookahead, scan=direction)
    outs = pl.pallas_call(
        body,
        grid=(bsz, nb),
        in_specs=in_specs,
        out_specs=out_specs,
        out_shape=out_shape,
        scratch_shapes=scratch,
        compiler_params=_params("arbitrary", "arbitrary"),
        name=name,
    )(*args)
    return outs if scan is not None else outs[0]


def _whole_seg(src, vt):
    return (src, None, vt.shape[1], vt.shape[3])


def _attn_window(q, k, vt, k_ctx, vt_ctx, sink):
    assert vt.shape[3] == BLOCK
    segs = [(0, "cur", 1, BLOCK), (0, "prev", 1, BLOCK), (0, "next", 1, BLOCK), _whole_seg(1, vt_ctx)]
    return _attention(q, [(k, vt), (k_ctx, vt_ctx)], segs, sink, BLOCK, WINDOW_SUBTILES, "attn_window")


def _attn_global(q, kv_list, sink, tq, name, lookahead=False, part=(0, 1), scan=None, bounded=False, nsub=1):
    segs = [_whole_seg(src, vt) for src, (_, vt) in enumerate(kv_list)]
    return _attention(q, kv_list, segs, sink, tq, nsub, name, lookahead, part, scan, bounded)


def _lane_chunk(c):
    if isinstance(c, int):
        return slice(c * LANES, (c + 1) * LANES)
    return pl.ds(pl.multiple_of(c * LANES, LANES), LANES)


def _neg_softplus_neg(lam):
    return -(jnp.maximum(-lam, 0.0) + jnp.log1p(jnp.exp(-jnp.abs(lam))))


def _rglru_gates(x_ref, wg_ref, br_ref, bi_ref, nsp_ref, a_scr, u_scr, c, rows, to_chunk=False):
    cs = _lane_chunk(c)
    dst = slice(None) if to_chunk else cs
    xcc = x_ref[rows, cs]
    g = jnp.dot(xcc.astype(BF16), wg_ref[c], preferred_element_type=F32)
    r = jax.nn.sigmoid(g[:, :LANES] + br_ref[:, cs])
    ig = jax.nn.sigmoid(g[:, LANES:] + bi_ref[:, cs])
    log_a = (LRU_C * r) * nsp_ref[:, cs]
    a = jnp.exp(log_a)
    w = -jnp.tanh(log_a) * (a * a + 1.0)
    a_scr[rows, dst] = a
    u_scr[rows, dst] = (w * lax.rsqrt(jnp.maximum(w, SQRT_FLOOR))) * (ig * xcc)


def _rglru_scan(a_scr, u_scr, h_ref, carry_scr, c, reverse, from_chunk=False):
    cs = _lane_chunk(c)
    src = slice(None) if from_chunk else cs
    ngroups = a_scr.shape[0] // SUBLANES
    sub = lax.broadcasted_iota(jnp.int32, (SUBLANES, LANES), 0)
    order = range(ngroups - 1, -1, -1) if reverse else range(ngroups)

    def rows(g):
        return slice(g * SUBLANES, (g + 1) * SUBLANES)

    h_loc = decay = None
    for gi in order:
        a_g, u_g = a_scr[rows(gi), src], u_scr[rows(gi), src]
        h_loc = u_g if h_loc is None else a_g * h_loc + u_g
        decay = a_g if decay is None else a_g * decay
        h_ref[rows(gi), cs] = h_loc
        a_scr[rows(gi), src] = decay

    end, dec = h_loc, decay
    for s in (1, 2, 4):
        if reverse:
            ok = sub < SUBLANES - s
            shift = SUBLANES - s
        else:
            ok = sub >= s
            shift = s
        end = end + dec * jnp.where(ok, pltpu.roll(end, shift, 0), 0.0)
        dec = dec * jnp.where(ok, pltpu.roll(dec, shift, 0), 1.0)
    carry = carry_scr[:, cs]
    seg_end = end + dec * carry
    if reverse:
        start = jnp.where(sub == SUBLANES - 1, carry, pltpu.roll(seg_end, SUBLANES - 1, 0))
        carry_scr[:, cs] = seg_end[0:1]
    else:
        start = jnp.where(sub == 0, carry, pltpu.roll(seg_end, 1, 0))
        carry_scr[:, cs] = seg_end[SUBLANES - 1:SUBLANES]
    for gi in order:
        h_ref[rows(gi), cs] = h_ref[rows(gi), cs] + a_scr[rows(gi), src] * start


def _rglru_kernel(x_ref, wg_ref, br_ref, bi_ref, lam_ref, h0_ref,
                  h_ref, hfin_ref, carry_scr, nsp_scr, a_scr, u_scr, *, reverse):
    j = pl.program_id(1)
    nt = pl.num_programs(1)

    @pl.when(j == 0)
    def _():
        carry_scr[...] = h0_ref[...]

    nsp_scr[...] = _neg_softplus_neg(lam_ref[...])

    for c in range(x_ref.shape[1] // LANES):
        _rglru_gates(x_ref, wg_ref, br_ref, bi_ref, nsp_scr, a_scr, u_scr, c, slice(None))
        _rglru_scan(a_scr, u_scr, h_ref, carry_scr, c, reverse)

    @pl.when(j == nt - 1)
    def _():
        hfin_ref[...] = carry_scr[...]


def _rglru(xs, h0, wg, br, bi, lam, reverse):
    bsz, rows, width = xs.shape
    nt = rows // SCAN_TILE

    def tpos(j):
        return nt - 1 - j if reverse else j

    vec = _resident((1, width), lambda b, j: (0, 0))
    tile = pl.BlockSpec((None, SCAN_TILE, width), lambda b, j: (b, tpos(j), 0))
    state = pl.BlockSpec((None, 1, width), lambda b, j: (b, 0, 0))
    return pl.pallas_call(
        functools.partial(_rglru_kernel, reverse=reverse),
        grid=(bsz, nt),
        in_specs=[tile, _resident(wg.shape, lambda b, j: (0, 0, 0)), vec, vec, vec, state],
        out_specs=[tile, state],
        out_shape=[jax.ShapeDtypeStruct((bsz, rows, width), F32),
                   jax.ShapeDtypeStruct((bsz, 1, width), F32)],
        scratch_shapes=[pltpu.VMEM((1, width), F32), pltpu.VMEM((1, width), F32),
                        pltpu.VMEM((SCAN_TILE, width), F32), pltpu.VMEM((SCAN_TILE, width), F32)],
        compiler_params=_params("arbitrary", "arbitrary"),
        name="rglru_bwd" if reverse else "rglru_fwd",
    )(xs, wg, br, bi, lam, h0)


def _merge_kernel(*refs, yb_parts):
    refs = list(refs)
    x_ref, sc_ref, sh_ref, gate_ref, ya_ref = [refs.pop(0) for _ in range(5)]
    yb_refs = [refs.pop(0) for _ in range(yb_parts)]
    (hf_ref, hb_ref, perm_ref, perm_t_ref, wy_ref, wg_ref, wa_ref, wb_ref, wc_ref, wo_ref,
     lng_ref, lnb_ref, o_ref) = refs
    per_part = pl.num_programs(1) // yb_parts
    yb = yb_refs[-1][...]
    for p in range(yb_parts - 2, -1, -1):
        yb = jnp.where(pl.program_id(1) < (p + 1) * per_part, yb_refs[p][...], yb)
    d = x_ref.shape[1]

    def scan_tile(k):
        rows = slice(k * SCAN_TILE, (k + 1) * SCAN_TILE)
        st = {}

        def project():
            st["x"] = x_ref[rows, :]
            st["h"] = _modulate(st["x"], sc_ref, sh_ref)
            hp = jnp.dot(perm_ref[...], st["h"], preferred_element_type=F32).astype(BF16)
            st["y_rnn"] = jnp.dot(hp, wy_ref[...], preferred_element_type=F32)
            st["zz"] = jnp.dot(st["h"], wg_ref[...], preferred_element_type=F32)

        def branches():
            yc = ((hf_ref[rows, :] + hb_ref[rows, :]) * jax.nn.gelu(st["y_rnn"])).astype(BF16)
            yc = jnp.dot(perm_t_ref[...], yc, preferred_element_type=F32).astype(BF16)
            zz = st["zz"]
            st["mix"] = (
                jax.nn.sigmoid(zz[:, 0:d]) * jnp.dot(ya_ref[rows, :], wa_ref[...], preferred_element_type=F32)
                + jax.nn.sigmoid(zz[:, d:2 * d]) * jnp.dot(yb[rows, :], wb_ref[...], preferred_element_type=F32)
                + jax.nn.sigmoid(zz[:, 2 * d:3 * d]) * jnp.dot(yc, wc_ref[...], preferred_element_type=F32))

        def finish():
            o = jnp.dot(st["mix"].astype(BF16), wo_ref[...], preferred_element_type=F32)
            o_ref[rows, :] = _layernorm(ALPHA * st["x"] + gate_ref[...] * o, lng_ref[...], lnb_ref[...])

        return project, branches, finish

    for stages in zip(*[scan_tile(k) for k in range(x_ref.shape[0] // SCAN_TILE)]):
        for stage in stages:
            stage()


def _merge(x, sc, sh, gate, ya, yb_parts, hf, hb, perm, perm_t, wy, wg, wa, wb, wc, wo, lng, lnb, tm):
    bsz, rows, d = x.shape
    per_part = rows // tm // len(yb_parts)

    def part_spec(p):
        return pl.BlockSpec((None, tm, Q_COLS), lambda b, i: (b, jnp.clip(i - p * per_part, 0, per_part - 1), 0))
    mod_spec = pl.BlockSpec((None, 1, d), lambda b, i: (b, 0, 0))
    vec = _resident((1, d), lambda b, i: (0, 0))

    def tile(cols):
        return pl.BlockSpec((None, tm, cols), lambda b, i: (b, i, 0))

    def weight(w):
        return _resident(w.shape, lambda b, i: (0, 0))

    weights = [perm, perm_t, wy, wg, wa, wb, wc, wo]
    return pl.pallas_call(
        functools.partial(_merge_kernel, yb_parts=len(yb_parts)),
        grid=(bsz, rows // tm),
        in_specs=[tile(d), mod_spec, mod_spec, mod_spec, tile(Q_COLS)]
                 + [part_spec(p) for p in range(len(yb_parts))] + [tile(d), tile(d)]
                 + [weight(w) for w in weights] + [vec, vec],
        out_specs=tile(d),
        out_shape=jax.ShapeDtypeStruct((bsz, rows, d), F32),
        compiler_params=_params("arbitrary", "arbitrary"),
        name="merge_ln",
    )(x, sc, sh, gate, ya, *yb_parts, hf, hb, *weights, lng, lnb)


def _mlp_kernel(x_ref, sc_ref, sh_ref, gate_ref, w1_ref, w2_ref, lng_ref, lnb_ref, o_ref):
    x = x_ref[...]
    h = _modulate(x, sc_ref, sh_ref)
    u = jnp.dot(h, w1_ref[...], preferred_element_type=F32)
    u = jnp.square(jnp.maximum(u, 0.0)).astype(BF16)
    o = jnp.dot(u, w2_ref[...], preferred_element_type=F32)
    o_ref[...] = _layernorm(ALPHA * x + gate_ref[...] * o, lng_ref[...], lnb_ref[...])


def _mlp(x, sc, sh, gate, w1, w2, lng, lnb, tm):
    bsz, rows, d = x.shape
    mod_spec = pl.BlockSpec((None, 1, d), lambda b, i: (b, 0, 0))
    vec = _resident((1, d), lambda b, i: (0, 0))
    tile = pl.BlockSpec((None, tm, d), lambda b, i: (b, i, 0))
    return pl.pallas_call(
        _mlp_kernel,
        grid=(bsz, rows // tm),
        in_specs=[tile, mod_spec, mod_spec, mod_spec,
                  _resident(w1.shape, lambda b, i: (0, 0)), _resident(w2.shape, lambda b, i: (0, 0)), vec, vec],
        out_specs=tile,
        out_shape=jax.ShapeDtypeStruct((bsz, rows, d), F32),
        compiler_params=_params("arbitrary", "arbitrary"),
        name="mlp_ln",
    )(x, sc, sh, gate, w1, w2, lng, lnb)


def _rope_tables(n):
    pos = jnp.arange(n)
    row = (pos // GRID_W).astype(F32)
    colp = (pos % GRID_W).astype(F32)
    nf = HEAD_DIM // 4
    inv = ROPE_BASE ** (-jnp.arange(nf, dtype=F32) / nf)
    ang_r = row[:, None] * inv
    ang_c = colp[:, None] * inv
    cos_t = jnp.concatenate([jnp.cos(ang_r), jnp.cos(ang_r), jnp.cos(ang_c), jnp.cos(ang_c)], axis=-1)
    sin_t = jnp.concatenate([-jnp.sin(ang_r), jnp.sin(ang_r), -jnp.sin(ang_c), jnp.sin(ang_c)], axis=-1)
    return jnp.tile(cos_t, (1, 2)), jnp.tile(sin_t, (1, 2))


def _gate_weights(w_r, w_i):
    def pair(w):
        w = w.reshape(LRU_BLOCKS // 2, 2, LRU_BLOCK_DIM, LRU_BLOCK_DIM)
        z = jnp.zeros_like(w[:, 0])
        top = jnp.concatenate([w[:, 0], z], axis=-1)
        bot = jnp.concatenate([z, w[:, 1]], axis=-1)
        return jnp.concatenate([top, bot], axis=-2)
    return jnp.concatenate([pair(w_r), pair(w_i)], axis=-1).astype(BF16)


def kernel(x, c, ctx, c_ctx, w_ada, b_ada, w_in, a_sink, b_q_gain, b_k_gain, c_conv_w, c_conv_b,
           c_wr, c_br, c_wi, c_bi, c_lam, w_br_a, w_br_b, w_br_c, w_out, ln1_g, ln1_b,
           w_ff1, w_ff2, ln2_g, ln2_b):
    bsz, n, d = x.shape
    m = ctx.shape[1]
    depth = w_ada.shape[0]

    cc = jnp.zeros((SUBLANES, d), F32).at[:bsz].set(c).at[bsz].set(c_ctx)
    mod = _ada(cc, w_ada, b_ada)
    rope_tabs = _rope_tables(n)
    blk = jnp.arange(LANES) // HEAD_DIM
    ones = (blk[:, None] == blk[None, :]).astype(BF16)
    h_zero = jnp.zeros((bsz, 1, d), F32)
    slot = jnp.arange(SCAN_TILE)
    token_of_slot = SEG_LEN * (slot % SUBLANES) + slot // SUBLANES
    perm = (token_of_slot[:, None] == slot[None, :]).astype(BF16)
    perm_t = perm.T

    for l in range(depth):
        with_ctx = l < depth - 1
        mod_lat = mod[l, :bsz].reshape(bsz, 1, 6 * d)
        mod_ctx = jnp.broadcast_to(mod[l, bsz].reshape(1, 1, 6 * d), (bsz, 1, 6 * d))
        sh1, sc1, g1, sh2, sc2, g2 = [mod_lat[:, :, k * d:(k + 1) * d] for k in range(6)]
        csh1, csc1, cg1, csh2, csc2, cg2 = [mod_ctx[:, :, k * d:(k + 1) * d] for k in range(6)]

        w_attn = w_in[l, :, :ATTN_COLS].astype(BF16)
        w_rnn = w_in[l, :, ATTN_COLS:ATTN_COLS + LRU_WIDTH].astype(BF16)
        w_y = w_in[l, :, ATTN_COLS + LRU_WIDTH:ATTN_COLS + 2 * LRU_WIDTH].astype(BF16)
        w_g = w_in[l, :, ATTN_COLS + 2 * LRU_WIDTH:].astype(BF16)
        logit_bound = (HEAD_DIM * HEAD_DIM ** -0.5 * LOG2_E) * jnp.max(jnp.abs(b_q_gain[l])) * jnp.max(
            jnp.abs(b_k_gain[l]))
        gq = jnp.tile(b_q_gain[l], 2).reshape(1, LANES)
        gk = jnp.tile(b_k_gain[l], 2).reshape(1, LANES)

        cb = c_conv_b[l].reshape(1, d)
        conv = (perm, c_conv_w[l], cb)
        qa, ka, va, qb, kb, vb, xs = _inproj(x, sc1, sh1, w_attn, w_rnn, gq, gk, ones, *conv, rope_tabs, ROW_TILE)
        qa_c, ka_c, va_c, qb_c, kb_c, vb_c, xs_c = _inproj(ctx, csc1, csh1, w_attn, w_rnn, gq, gk, ones, *conv,
                                                           None, m)

        ya = _attn_window(qa, ka, va, ka_c, va_c, a_sink[l])
        hs, hs_c, yb_parts = [], [], []
        for di, direction in enumerate(("fwd", "bwd")):
            wg = _gate_weights(c_wr[l, di], c_wi[l, di])
            vecs = [v[l, di].reshape(1, d) for v in (c_br, c_bi, c_lam)]
            h_c, h_fin = _rglru(xs_c, h_zero, wg, *vecs, direction == "bwd")
            scan = (direction, xs, h_fin, wg, *vecs)
            kv_b = [(kb, vb), (kb_c, vb_c)]
            yb_part, h_l = lax.cond(
                logit_bound <= LOGIT_RANGE,
                lambda: _attn_global(qb, kv_b, None, BLOCK, "attn_bounded_" + direction,
                                     part=(di, 2), scan=scan, bounded=True, nsub=BOUNDED_TILES),
                lambda: _attn_global(qb, kv_b, None, BLOCK, "attn_global_" + direction,
                                     lookahead=True, part=(di, 2), scan=scan))
            yb_parts.append(yb_part)
            hs.append(h_l)
            hs_c.append(h_c)

        merge_w = [perm, perm_t, w_y, w_g] + [w[l].astype(BF16) for w in (w_br_a, w_br_b, w_br_c, w_out)]
        ln1 = [ln1_g[l].reshape(1, d), ln1_b[l].reshape(1, d)]
        ln2 = [ln2_g[l].reshape(1, d), ln2_b[l].reshape(1, d)]
        w1 = w_ff1[l].astype(BF16)
        w2 = w_ff2[l].astype(BF16)

        x = _merge(x, sc1, sh1, g1, ya, yb_parts, hs[0], hs[1], *merge_w, *ln1, ROW_TILE)
        x = _mlp(x, sc2, sh2, g2, w1, w2, *ln2, ROW_TILE)
        if with_ctx:
            ya_c = _attn_global(qa_c, [(ka_c, va_c)], a_sink[l], BLOCK, "attn_ctx_a")
            yb_c = _attn_global(qb_c, [(kb_c, vb_c)], None, BLOCK, "attn_ctx_b")
            ctx = _merge(ctx, csc1, csh1, cg1, ya_c, [yb_c], hs_c[0], hs_c[1], *merge_w, *ln1, m)
            ctx = _mlp(ctx, csc2, csh2, cg2, w1, w2, *ln2, m)
    return x
```

```python
import functools

import jax
import jax.numpy as jnp
from jax import lax
from jax.experimental import pallas as pl
from jax.experimental.pallas import tpu as pltpu

D_MODEL = 1024
DEPTH = 2
GRID_W = 64
HEAD_DIM = 64
ROPE_BASE = 10000.0
BLOCK = 128
A_HEADS = 8
A_KV = 2
B_HEADS = 8
B_KV = 2
LRU_WIDTH = D_MODEL
LRU_BLOCKS = 16
LRU_BLOCK_DIM = LRU_WIDTH // LRU_BLOCKS
CONV_W = 4
LRU_C = 8.0
D_FF = 4 * D_MODEL
ALPHA = (2 * DEPTH) ** 0.25
LN_EPS = 1e-5
RMS_EPS = 1e-6
NEG_INF = -1e30
LOG2_E = 1.4426950408889634
Q_COLS = A_HEADS * HEAD_DIM
KV_COLS = A_KV * HEAD_DIM
ATTN_COLS = 2 * (Q_COLS + 2 * KV_COLS)
GROUP = A_HEADS // A_KV

LANES = 128
SUBLANES = 8
V7X_VMEM_BYTES = 64 * 1024 * 1024
VMEM_LIMIT_BYTES = V7X_VMEM_BYTES - 8 * 1024 * 1024

SCAN_TILE = 256
SEG_LEN = SCAN_TILE // SUBLANES
ROW_TILE = 2 * SCAN_TILE
LOGIT_RANGE = 100.0
ONES_ROWS = 16
STATIC_CHUNKS = 4
BOUNDED_TILES = 2
WINDOW_SUBTILES = 8
SQRT_FLOOR = 1e-30

F32 = jnp.float32
BF16 = jnp.bfloat16


def _params(*semantics):
    return pltpu.CompilerParams(dimension_semantics=semantics, vmem_limit_bytes=VMEM_LIMIT_BYTES)


def _resident(block_shape, index_map):
    return pl.BlockSpec(block_shape, index_map, pipeline_mode=pl.Buffered(1))


def _modulate(x, sc_ref, sh_ref):
    return (x * (1.0 + sc_ref[...]) + sh_ref[...]).astype(BF16)


def _layernorm(y, g, b):
    mu = jnp.mean(y, axis=-1, keepdims=True)
    yc = y - mu
    var = jnp.mean(yc * yc, axis=-1, keepdims=True)
    return yc * lax.rsqrt(var + LN_EPS) * g + b


def _ada_kernel(c_ref, w_ref, b_ref, o_ref):
    c = c_ref[...]
    s = c * jax.nn.sigmoid(c)
    o_ref[...] = jnp.dot(s, w_ref[...], preferred_element_type=F32,
                         precision=lax.Precision.HIGHEST) + b_ref[...]


def _ada(cc, w_ada, b_ada):
    depth, d, cols = w_ada.shape
    tn = cols // 4
    return pl.pallas_call(
        _ada_kernel,
        grid=(depth, cols // tn),
        in_specs=[pl.BlockSpec((SUBLANES, d), lambda l, j: (0, 0)),
                  pl.BlockSpec((None, d, tn), lambda l, j: (l, 0, j)),
                  pl.BlockSpec((None, 1, tn), lambda l, j: (l, 0, j))],
        out_specs=pl.BlockSpec((None, SUBLANES, tn), lambda l, j: (l, 0, j)),
        out_shape=jax.ShapeDtypeStruct((depth, SUBLANES, cols), F32),
        compiler_params=_params("arbitrary", "arbitrary"),
        name="ada_mod",
    )(cc, w_ada, b_ada.reshape(depth, 1, cols))


def _inproj_kernel(*refs, rope):
    refs = list(refs)
    (xp_ref, x_ref, xn_ref, sc_ref, sh_ref, wa_ref, wr_ref, gq_ref, gk_ref, ones_ref, perm_ref,
     cw_ref, cb_ref) = [refs.pop(0) for _ in range(13)]
    if rope:
        cos_ref, sin_ref = refs.pop(0), refs.pop(0)
    qa_ref, ka_ref, va_ref, qb_ref, kb_ref, vb_ref, xs_ref = refs
    i = pl.program_id(1)
    nt = pl.num_programs(1)
    h = _modulate(x_ref[...], sc_ref, sh_ref)
    tm = h.shape[0]

    z = jnp.dot(h, wa_ref[...], preferred_element_type=F32)
    lane = lax.broadcasted_iota(jnp.int32, (tm, LANES), 1)
    first16 = (lane & 31) < 16
    ones = ones_ref[...]
    scale = HEAD_DIM ** -0.5 * LOG2_E

    def rot(zc):
        if not rope:
            return zc
        partner = jnp.where(first16, pltpu.roll(zc, LANES - 16, 1), pltpu.roll(zc, 16, 1))
        return zc * cos_ref[...] + partner * sin_ref[...]

    def rms(zc, g_ref):
        sq = zc * zc
        hi = sq.astype(BF16)
        lo = (sq - hi.astype(F32)).astype(BF16)
        ss = (jnp.dot(hi, ones, preferred_element_type=F32)
              + jnp.dot(lo, ones, preferred_element_type=F32))
        return zc * lax.rsqrt(ss * (1.0 / HEAD_DIM) + RMS_EPS) * g_ref[...]

    def col(start):
        return z[:, start:start + LANES]

    for c in range(Q_COLS // LANES):
        qa_ref[:, c * LANES:(c + 1) * LANES] = (rot(col(c * LANES)) * scale).astype(BF16)
    base = Q_COLS
    ka_ref[...] = rot(col(base)).astype(BF16)
    va_t = col(base + KV_COLS).T.astype(BF16)
    for c in range(va_ref.shape[0]):
        va_ref[c] = va_t[:, c * BLOCK:(c + 1) * BLOCK]
    base = Q_COLS + 2 * KV_COLS
    for c in range(Q_COLS // LANES):
        qb_ref[:, c * LANES:(c + 1) * LANES] = (
            rot(rms(col(base + c * LANES), gq_ref)) * scale).astype(BF16)
    base += Q_COLS
    kb_ref[...] = rot(rms(col(base), gk_ref)).astype(BF16)
    vb_ref[0] = col(base + KV_COLS).T.astype(BF16)

    ntile = tm // SCAN_TILE
    hp = [jnp.dot(perm_ref[...], h[k * SCAN_TILE:(k + 1) * SCAN_TILE], preferred_element_type=F32).astype(BF16)
          for k in range(ntile)]
    halo = _modulate(jnp.concatenate([xp_ref[...], xn_ref[...]], axis=0), sc_ref, sh_ref)
    zall = jnp.dot(jnp.concatenate(hp + [halo], axis=0), wr_ref[...], preferred_element_type=F32)
    zs = [zall[k * SCAN_TILE:(k + 1) * SCAN_TILE] for k in range(ntile)]
    edge_prev = jnp.where(i > 0, zall[tm:tm + SUBLANES], 0.0)
    edge_next = jnp.where(i < nt - 1, zall[tm + SUBLANES:], 0.0)
    sub = lax.broadcasted_iota(jnp.int32, (SUBLANES, zall.shape[1]), 0)
    last_row = SCAN_TILE - 1
    for k, zk in enumerate(zs):
        tok_before = edge_prev[SUBLANES - 1:SUBLANES] if k == 0 else zs[k - 1][last_row:last_row + 1]
        tok_after1 = edge_next[0:1] if k == ntile - 1 else zs[k + 1][0:1]
        tok_after2 = edge_next[1:2] if k == ntile - 1 else zs[k + 1][SUBLANES:SUBLANES + 1]
        first, second, last = zk[0:SUBLANES], zk[SUBLANES:2 * SUBLANES], zk[SCAN_TILE - SUBLANES:]
        before = jnp.where(sub == 0, tok_before, pltpu.roll(last, 1, 0))
        after1 = jnp.where(sub == SUBLANES - 1, tok_after1, pltpu.roll(first, SUBLANES - 1, 0))
        after2 = jnp.where(sub == SUBLANES - 1, tok_after2, pltpu.roll(second, SUBLANES - 1, 0))
        taps = [jnp.concatenate([before, zk[:SCAN_TILE - SUBLANES]], axis=0),
                zk,
                jnp.concatenate([zk[SUBLANES:], after1], axis=0),
                jnp.concatenate([zk[2 * SUBLANES:], after1, after2], axis=0)]
        out = cb_ref[...]
        for tap in range(CONV_W):
            out = out + cw_ref[tap:tap + 1, :] * taps[tap]
        xs_ref[k * SCAN_TILE:(k + 1) * SCAN_TILE, :] = out


def _inproj(x, sc, sh, w_attn, w_rnn, gq, gk, ones, perm, cw, cb, rope_tabs, tm):
    bsz, rows, d = x.shape
    rope = rope_tabs is not None
    hb = tm // SUBLANES
    mod_spec = pl.BlockSpec((None, 1, d), lambda b, i: (b, 0, 0))
    vec_spec = _resident((1, LANES), lambda b, i: (0, 0))
    rnn_cols = w_rnn.shape[1]
    in_specs = [pl.BlockSpec((None, SUBLANES, d), lambda b, i: (b, jnp.maximum(i * hb - 1, 0), 0)),
                pl.BlockSpec((None, tm, d), lambda b, i: (b, i, 0)),
                pl.BlockSpec((None, SUBLANES, d),
                             lambda b, i: (b, jnp.minimum((i + 1) * hb, rows // SUBLANES - 1), 0)),
                mod_spec, mod_spec,
                _resident((d, ATTN_COLS), lambda b, i: (0, 0)), _resident((d, rnn_cols), lambda b, i: (0, 0)),
                vec_spec, vec_spec, _resident((LANES, LANES), lambda b, i: (0, 0)),
                _resident((SCAN_TILE, SCAN_TILE), lambda b, i: (0, 0)),
                _resident((CONV_W, rnn_cols), lambda b, i: (0, 0)), _resident((1, rnn_cols), lambda b, i: (0, 0))]
    args = [x, x, x, sc, sh, w_attn, w_rnn, gq, gk, ones, perm, cw, cb]
    if rope:
        tab_spec = pl.BlockSpec((tm, LANES), lambda b, i: (i, 0))
        in_specs += [tab_spec, tab_spec]
        args += list(rope_tabs)

    def out(cols, dtype=BF16):
        return (pl.BlockSpec((None, tm, cols), lambda b, i: (b, i, 0)),
                jax.ShapeDtypeStruct((bsz, rows, cols), dtype))

    def out_t(slab):
        return (pl.BlockSpec((None, tm // slab, KV_COLS, slab), lambda b, i: (b, i, 0, 0)),
                jax.ShapeDtypeStruct((bsz, rows // slab, KV_COLS, slab), BF16))

    outs = [out(Q_COLS), out(KV_COLS), out_t(BLOCK), out(Q_COLS), out(KV_COLS), out_t(tm), out(rnn_cols, F32)]
    return pl.pallas_call(
        functools.partial(_inproj_kernel, rope=rope),
        grid=(bsz, rows // tm),
        in_specs=in_specs,
        out_specs=[o[0] for o in outs],
        out_shape=[o[1] for o in outs],
        compiler_params=_params("arbitrary", "arbitrary"),
        name="inproj_rope" if rope else "inproj_ctx",
    )(*args)


def _attn_kernel(*refs, segs, n_src, has_sink, tq, nsub, lookahead, scan):
    refs = list(refs)
    sink_ref = refs.pop(0) if has_sink else None
    q_ref = refs.pop(0)
    qn_ref = refs.pop(0) if lookahead else None
    kv_refs = [(refs.pop(0), refs.pop(0)) for _ in range(n_src)]
    if scan:
        xs_ref, wg_ref, br_ref, bi_ref, lam_ref, h0_ref = [refs.pop(0) for _ in range(6)]
    o_ref = refs.pop(0)
    if scan:
        h_ref = refs.pop(0)
        u_scr, a_scr, nsp_scr, carry_scr = [refs.pop() for _ in range(4)]
    i = pl.program_id(1)
    nblk = pl.num_programs(1) * nsub
    width = GROUP * tq

    def sink_row(kvi):
        return jnp.concatenate(
            [jnp.full((1, tq), sink_ref[GROUP * kvi + hh] * LOG2_E, F32) for hh in range(GROUP)], axis=1)

    def query_operand(q, kvi):
        qt = q.T
        heads = jnp.concatenate(
            [qt[(GROUP * kvi + hh) * HEAD_DIM:(GROUP * kvi + hh + 1) * HEAD_DIM, :] for hh in range(GROUP)],
            axis=1)
        parts = [jnp.zeros((HEAD_DIM, width), BF16)] * A_KV
        parts[kvi] = heads
        return jnp.concatenate(parts, axis=0)

    def run_tile(sub):
        mine = refs[sub * 4 * A_KV:(sub + 1) * 4 * A_KV]
        s_scr, m_scr, l_scr, acc_scr = [mine[g * A_KV:(g + 1) * A_KV] for g in range(4)]
        pos = i * nsub + sub
        rows = slice(sub * tq, (sub + 1) * tq)

        def seg_mask(where):
            if where not in ("prev", "next"):
                return None
            j = lax.broadcasted_iota(jnp.int32, (BLOCK, width), 0)
            r = lax.broadcasted_iota(jnp.int32, (BLOCK, width), 1) & (tq - 1)
            far = jnp.int32(4 * BLOCK)
            if where == "prev":
                return (j - r) >= jnp.where(pos > 0, 0, far)
            return (r - j) >= jnp.where(pos < nblk - 1, 0, far)

        def block_of(where):
            return {"cur": pos, "prev": jnp.maximum(pos - 1, 0), "next": jnp.minimum(pos + 1, nblk - 1)}[where]

        def for_each_chunk(*fns, along=None):
            base = 0
            for src, where, nchunks, kc in segs:
                k_ref, vt_ref = kv_refs[src]
                if where is None:
                    row0, slab0 = 0, 0
                else:
                    slab0 = block_of(where)
                    row0 = pl.multiple_of(slab0 * BLOCK, BLOCK)
                if nchunks <= STATIC_CHUNKS:
                    for ci in range(nchunks):
                        for fn in fns:
                            fn(k_ref, vt_ref, row0 + ci * kc, slab0 + ci, base + ci * kc, kc, where)
                else:
                    def body(ci, carry, k_ref=k_ref, vt_ref=vt_ref, kc=kc, base=base):
                        r0 = pl.multiple_of(ci * kc, kc)
                        for fn in fns:
                            fn(k_ref, vt_ref, r0, ci, base + r0, kc, None)
                        if along is not None:
                            along(ci)
                        return carry
                    lax.fori_loop(0, nchunks, body, 0, unroll=4)
                base += nchunks * kc

        def start_scores(kvi):
            m_scr[kvi][...] = sink_row(kvi) if has_sink else jnp.full((1, width), NEG_INF, F32)

        def scores(kvi, qz, k_ref, vt_ref, r0, slab, srow, kc, where):
            s = jnp.dot(k_ref[pl.ds(r0, kc), :], qz, preferred_element_type=F32)
            mask = seg_mask(where)
            if mask is not None:
                s = jnp.where(mask, s, NEG_INF)
            s_scr[kvi][pl.ds(srow, kc), :] = s
            m_scr[kvi][...] = jnp.maximum(m_scr[kvi][...], jnp.max(s, axis=0, keepdims=True))

        def start_values(kvi):
            if has_sink:
                l_scr[kvi][...] = jnp.exp2(sink_row(kvi) - m_scr[kvi][...])
            else:
                l_scr[kvi][...] = jnp.zeros((1, width), F32)
            acc_scr[kvi][...] = jnp.zeros((HEAD_DIM + ONES_ROWS, width), F32)

        def values(kvi, k_ref, vt_ref, r0, slab, srow, kc, where):
            p = jnp.exp2(s_scr[kvi][pl.ds(srow, kc), :] - m_scr[kvi][...])
            lhs = jnp.concatenate([vt_ref[slab, kvi * HEAD_DIM:(kvi + 1) * HEAD_DIM, :],
                                   jnp.ones((ONES_ROWS, kc), BF16)], axis=0)
            acc_scr[kvi][...] += jnp.dot(lhs, p.astype(BF16), preferred_element_type=F32)

        def first_scores():
            start_scores(0)
            for_each_chunk(functools.partial(scores, 0, query_operand(q_ref[rows, :], 0)))

        def phase_a():
            if lookahead:
                pl.when(i == 0)(first_scores)
            else:
                first_scores()

        half = SCAN_TILE // 2

        def scan_first_half(c):
            _rglru_gates(xs_ref, wg_ref, br_ref, bi_ref, nsp_scr, a_scr, u_scr, c, slice(0, half))

        def scan_second_half(c):
            _rglru_gates(xs_ref, wg_ref, br_ref, bi_ref, nsp_scr, a_scr, u_scr, c, slice(half, SCAN_TILE))
            _rglru_scan(a_scr, u_scr, h_ref, carry_scr, c, scan == "bwd")

        def phase_b():
            start_scores(1)
            start_values(0)
            for_each_chunk(functools.partial(scores, 1, query_operand(q_ref[rows, :], 1)),
                           functools.partial(values, 0), along=scan_first_half if scan else None)

        def phase_c():
            start_values(1)
            if lookahead:
                start_scores(0)
                for_each_chunk(functools.partial(values, 1),
                               functools.partial(scores, 0, query_operand(qn_ref[...], 0)),
                               along=scan_second_half if scan else None)
            else:
                for_each_chunk(functools.partial(values, 1))

        def finish():
            outs = []
            for kvi in range(A_KV):
                acc = acc_scr[kvi][...]
                den = acc[HEAD_DIM:HEAD_DIM + 1] + l_scr[kvi][...]
                o = acc[:HEAD_DIM] / den
                outs += [o[:, hh * tq:(hh + 1) * tq] for hh in range(GROUP)]
            o_ref[rows, :] =jnp.concatenate(outs, axis=0).T.astype(BF16)

        return phase_a, phase_b, phase_c, finish

    if scan:
        nsp_scr[...] = _neg_softplus_neg(lam_ref[...])

        @pl.when(i == 0)
        def _():
            carry_scr[...] = h0_ref[...]

    for phases in zip(*[run_tile(sub) for sub in range(nsub)]):
        for phase in phases:
            phase()


def _attn_bounded_kernel(*refs, segs, n_src, tq, tiles, scan):
    refs = list(refs)
    q_ref = refs.pop(0)
    kv_refs = [(refs.pop(0), refs.pop(0)) for _ in range(n_src)]
    if scan:
        xs_ref, wg_ref, br_ref, bi_ref, lam_ref, h0_ref = [refs.pop(0) for _ in range(6)]
    o_ref = refs.pop(0)
    if scan:
        h_ref = refs.pop(0)
        carry_scr, nsp_scr = refs.pop(), refs.pop()
        gate_scr = [(refs.pop(), refs.pop()) for _ in range(2)]
    i = pl.program_id(1)
    width = GROUP * tq

    if scan:
        nsp_scr[...] = _neg_softplus_neg(lam_ref[...])

        @pl.when(i == 0)
        def _():
            carry_scr[...] = h0_ref[...]

    def run_tile(t):
        mine = refs[t * 4 * A_KV:(t + 1) * 4 * A_KV]
        l_scr, acc_scr = mine[:A_KV], mine[A_KV:2 * A_KV]
        s_scr = [mine[2 * A_KV:3 * A_KV], mine[3 * A_KV:4 * A_KV]]
        qrows = slice(t * tq, (t + 1) * tq)
        if scan:
            xs_t = xs_ref.at[t * SCAN_TILE:(t + 1) * SCAN_TILE, :]
            h_t = h_ref.at[t * SCAN_TILE:(t + 1) * SCAN_TILE, :]
        qz = []

        def scores(k_ref, r0, kc, buf):
            k = k_ref[pl.ds(r0, kc), :]
            for kvi in range(A_KV):
                buf[kvi][0:kc, :] = jnp.dot(k, qz[kvi], preferred_element_type=F32)

        def values(vt_ref, slab, kc, buf):
            for kvi in range(A_KV):
                p = jnp.exp2(buf[kvi][0:kc, :])
                l_scr[kvi][...] += jnp.sum(p, axis=0, keepdims=True)
                acc_scr[kvi][...] += jnp.dot(vt_ref[slab, kvi * HEAD_DIM:(kvi + 1) * HEAD_DIM, :], p.astype(BF16),
                                             preferred_element_type=F32)

        def gates(c, buf):
            _rglru_gates(xs_t, wg_ref, br_ref, bi_ref, nsp_scr, buf[0], buf[1], c, slice(None), to_chunk=True)

        def scan_chunk(c, buf):
            _rglru_scan(buf[0], buf[1], h_t, carry_scr, c, scan == "bwd", from_chunk=True)

        (src0, _, nchunks, kc), rest = segs[0], segs[1:]
        k_ref, vt_ref = kv_refs[src0]
        assert nchunks % 2 == 0
        last = nchunks - 1

        def head():
            qt = q_ref[qrows, :].T
            for kvi in range(A_KV):
                heads = jnp.concatenate(
                    [qt[(GROUP * kvi + hh) * HEAD_DIM:(GROUP * kvi + hh + 1) * HEAD_DIM, :]
                     for hh in range(GROUP)], axis=1)
                parts = [jnp.zeros((HEAD_DIM, width), BF16)] * A_KV
                parts[kvi] = heads
                qz.append(jnp.concatenate(parts, axis=0))
                l_scr[kvi][...] = jnp.zeros((1, width), F32)
                acc_scr[kvi][...] = jnp.zeros((HEAD_DIM, width), F32)
            scores(k_ref, 0, kc, s_scr[0])

        def main():
            if scan:
                gates(0, gate_scr[0])
            for j in range(nchunks // 2):
                even, odd = 2 * j, 2 * j + 1
                nxt = min(even + 2, last)
                scores(k_ref, odd * kc, kc, s_scr[1])
                values(vt_ref, even, kc, s_scr[0])
                if scan:
                    gates(odd, gate_scr[1])
                    scan_chunk(even, gate_scr[0])
                scores(k_ref, nxt * kc, kc, s_scr[0])
                values(vt_ref, odd, kc, s_scr[1])
                if scan:
                    gates(nxt, gate_scr[0])
                    scan_chunk(odd, gate_scr[1])
            for src, where, nch, kcs in rest:
                assert where is None and nch <= STATIC_CHUNKS
                kr, vr = kv_refs[src]
                for ci in range(nch):
                    scores(kr, ci * kcs, kcs, s_scr[ci % 2])
                    values(vr, ci, kcs, s_scr[ci % 2])

        def tail():
            outs = []
            for kvi in range(A_KV):
                o = acc_scr[kvi][...] / l_scr[kvi][...]
                outs += [o[:, hh * tq:(hh + 1) * tq] for hh in range(GROUP)]
            o_ref[qrows, :] = jnp.concatenate(outs, axis=0).T.astype(BF16)

        return head, main, tail

    order = range(tiles - 1, -1, -1) if scan == "bwd" else range(tiles)
    pending_tail = None
    for t in order:
        head, main, tail = run_tile(t)
        head()
        if pending_tail is not None:
            pending_tail()
        main()
        pending_tail = tail
    pending_tail()


def _attention(q, kv_arrays, segs, sink, tq, nsub, name, lookahead=False, part=(0, 1), scan=None,
               bounded=False):
    assert not (lookahead and nsub != 1) and not (bounded and (lookahead or sink is not None))
    bsz, rows, _ = q.shape
    nb = rows // (tq * nsub * part[1])
    first = part[0] * nb
    has_sink = sink is not None
    in_specs, args = [], []
    if has_sink:
        in_specs.append(pl.BlockSpec(memory_space=pltpu.SMEM))
        args.append(sink)
    in_specs.append(pl.BlockSpec((None, tq * nsub, Q_COLS), lambda b, i: (b, first + i, 0)))
    args.append(q)
    if lookahead:
        in_specs.append(pl.BlockSpec((None, tq, Q_COLS), lambda b, i: (b, first + jnp.minimum(i + 1, nb - 1), 0)))
        args.append(q)
    for k, vt in kv_arrays:
        in_specs += [pl.BlockSpec((None,) + k.shape[1:], lambda b, i: (b, 0, 0)),
                     pl.BlockSpec((None,) + vt.shape[1:], lambda b, i: (b, 0, 0, 0))]
        args += [k, vt]
    width = GROUP * tq
    nk_total = sum(nchunks * kc for _, _, nchunks, kc in segs)
    per_head = [(nk_total, width), (1, width), (1, width), (HEAD_DIM + ONES_ROWS, width)]
    if bounded:
        kc0 = segs[0][3]
        per_head = [(1, width), (HEAD_DIM, width), (kc0, width), (kc0, width)]
    out_specs = [pl.BlockSpec((None, tq * nsub, Q_COLS), lambda b, i: (b, i, 0))]
    out_shape = [jax.ShapeDtypeStruct((bsz, rows // part[1], Q_COLS), BF16)]
    scratch = [pltpu.VMEM(shape, F32) for _ in range(nsub) for shape in per_head for _ in range(A_KV)]
    direction = None
    if scan is not None:
        direction, xs, h0, wg, br, bi, lam = scan
        srows, swidth = xs.shape[1:]
        scan_rows = SCAN_TILE * (nsub if bounded else 1)
        assert srows // scan_rows == nb and swidth // LANES == segs[0][2]
        tile = pl.BlockSpec((None, scan_rows, swidth),
                            (lambda b, i: (b, nb - 1 - i, 0)) if direction == "bwd" else (lambda b, i: (b, i, 0)))
        vec = _resident((1, swidth), lambda b, i: (0, 0))
        in_specs += [tile, _resident(wg.shape, lambda b, i: (0, 0, 0)), vec, vec, vec,
                     pl.BlockSpec((None, 1, swidth), lambda b, i: (b, 0, 0))]
        args += [xs, wg, br, bi, lam, h0]
        out_specs.append(tile)
        out_shape.append(jax.ShapeDtypeStruct(xs.shape, F32))
        if bounded:
            scratch += [pltpu.VMEM((SCAN_TILE, LANES), F32) for _ in range(4)]
            scratch += [pltpu.VMEM((1, swidth), F32), pltpu.VMEM((1, swidth), F32)]
        else:
            scratch += [pltpu.VMEM((1, swidth), F32), pltpu.VMEM((1, swidth), F32),
                        pltpu.VMEM((SCAN_TILE, swidth), F32), pltpu.VMEM((SCAN_TILE, swidth), F32)]
    if bounded:
        body = functools.partial(_attn_bounded_kernel, segs=tuple(segs), n_src=len(kv_arrays), tq=tq, tiles=nsub,
                                 scan=direction)
    else:
        body = functools.partial(_attn_kernel, segs=tuple(segs), n_src=len(kv_arrays), has_sink=has_sink, tq=tq,
                                 nsub=nsub, lookahead=lookahead, scan=direction)
    outs = pl.pallas_call(
        body,
        grid=(bsz, nb),
        in_specs=in_specs,
        out_specs=out_specs,
        out_shape=out_shape,
        scratch_shapes=scratch,
        compiler_params=_params("arbitrary", "arbitrary"),
        name=name,
    )(*args)
    return outs if scan is not None else outs[0]


def _whole_seg(src, vt):
    return (src, None, vt.shape[1], vt.shape[3])


def _attn_window(q, k, vt, k_ctx, vt_ctx, sink):
    assert vt.shape[3] == BLOCK
    segs = [(0, "cur", 1, BLOCK), (0, "prev", 1, BLOCK), (0, "next", 1, BLOCK), _whole_seg(1, vt_ctx)]
    return _attention(q, [(k, vt), (k_ctx, vt_ctx)], segs, sink, BLOCK, WINDOW_SUBTILES, "attn_window")


def _attn_global(q, kv_list, sink, tq, name, lookahead=False, part=(0, 1), scan=None, bounded=False, nsub=1):
    segs = [_whole_seg(src, vt) for src, (_, vt) in enumerate(kv_list)]
    return _attention(q, kv_list, segs, sink, tq, nsub, name, lookahead, part, scan, bounded)


def _lane_chunk(c):
    if isinstance(c, int):
        return slice(c * LANES, (c + 1) * LANES)
    return pl.ds(pl.multiple_of(c * LANES, LANES), LANES)


def _neg_softplus_neg(lam):
    return -(jnp.maximum(-lam, 0.0) + jnp.log1p(jnp.exp(-jnp.abs(lam))))


def _rglru_gates(x_ref, wg_ref, br_ref, bi_ref, nsp_ref, a_scr, u_scr, c, rows, to_chunk=False):
    cs = _lane_chunk(c)
    dst = slice(None) if to_chunk else cs
    xcc = x_ref[rows, cs]
    g = jnp.dot(xcc.astype(BF16), wg_ref[c], preferred_element_type=F32)
    r = jax.nn.sigmoid(g[:, :LANES] + br_ref[:, cs])
    ig = jax.nn.sigmoid(g[:, LANES:] + bi_ref[:, cs])
    log_a = (LRU_C * r) * nsp_ref[:, cs]
    a = jnp.exp(log_a)
    w = -jnp.tanh(log_a) * (a * a + 1.0)
    a_scr[rows, dst] = a
    u_scr[rows, dst] = (w * lax.rsqrt(jnp.maximum(w, SQRT_FLOOR))) * (ig * xcc)


def _rglru_scan(a_scr, u_scr, h_ref, carry_scr, c, reverse, from_chunk=False):
    cs = _lane_chunk(c)
    src = slice(None) if from_chunk else cs
    ngroups = a_scr.shape[0] // SUBLANES
    sub = lax.broadcasted_iota(jnp.int32, (SUBLANES, LANES), 0)
    order = range(ngroups - 1, -1, -1) if reverse else range(ngroups)

    def rows(g):
        return slice(g * SUBLANES, (g + 1) * SUBLANES)

    h_loc = decay = None
    for gi in order:
        a_g, u_g = a_scr[rows(gi), src], u_scr[rows(gi), src]
        h_loc = u_g if h_loc is None else a_g * h_loc + u_g
        decay = a_g if decay is None else a_g * decay
        h_ref[rows(gi), cs] = h_loc
        a_scr[rows(gi), src] = decay

    end, dec = h_loc, decay
    for s in (1, 2, 4):
        if reverse:
            ok = sub < SUBLANES - s
            shift = SUBLANES - s
        else:
            ok = sub >= s
            shift = s
        end = end + dec * jnp.where(ok, pltpu.roll(end, shift, 0), 0.0)
        dec = dec * jnp.where(ok, pltpu.roll(dec, shift, 0), 1.0)
    carry = carry_scr[:, cs]
    seg_end = end + dec * carry
    if reverse:
        start = jnp.where(sub == SUBLANES - 1, carry, pltpu.roll(seg_end, SUBLANES - 1, 0))
        carry_scr[:, cs] = seg_end[0:1]
    else:
        start = jnp.where(sub == 0, carry, pltpu.roll(seg_end, 1, 0))
        carry_scr[:, cs] = seg_end[SUBLANES - 1:SUBLANES]
    for gi in order:
        h_ref[rows(gi), cs] = h_ref[rows(gi), cs] + a_scr[rows(gi), src] * start


def _rglru_kernel(x_ref, wg_ref, br_ref, bi_ref, lam_ref, h0_ref,
                  h_ref, hfin_ref, carry_scr, nsp_scr, a_scr, u_scr, *, reverse):
    j = pl.program_id(1)
    nt = pl.num_programs(1)

    @pl.when(j == 0)
    def _():
        carry_scr[...] = h0_ref[...]

    nsp_scr[...] = _neg_softplus_neg(lam_ref[...])

    for c in range(x_ref.shape[1] // LANES):
        _rglru_gates(x_ref, wg_ref, br_ref, bi_ref, nsp_scr, a_scr, u_scr, c, slice(None))
        _rglru_scan(a_scr, u_scr, h_ref, carry_scr, c, reverse)

    @pl.when(j == nt - 1)
    def _():
        hfin_ref[...] = carry_scr[...]


def _rglru(xs, h0, wg, br, bi, lam, reverse):
    bsz, rows, width = xs.shape
    nt = rows // SCAN_TILE

    def tpos(j):
        return nt - 1 - j if reverse else j

    vec = _resident((1, width), lambda b, j: (0, 0))
    tile = pl.BlockSpec((None, SCAN_TILE, width), lambda b, j: (b, tpos(j), 0))
    state = pl.BlockSpec((None, 1, width), lambda b, j: (b, 0, 0))
    return pl.pallas_call(
        functools.partial(_rglru_kernel, reverse=reverse),
        grid=(bsz, nt),
        in_specs=[tile, _resident(wg.shape, lambda b, j: (0, 0, 0)), vec, vec, vec, state],
        out_specs=[tile, state],
        out_shape=[jax.ShapeDtypeStruct((bsz, rows, width), F32),
                   jax.ShapeDtypeStruct((bsz, 1, width), F32)],
        scratch_shapes=[pltpu.VMEM((1, width), F32), pltpu.VMEM((1, width), F32),
                        pltpu.VMEM((SCAN_TILE, width), F32), pltpu.VMEM((SCAN_TILE, width), F32)],
        compiler_params=_params("arbitrary", "arbitrary"),
        name="rglru_bwd" if reverse else "rglru_fwd",
    )(xs, wg, br, bi, lam, h0)


def _merge_kernel(*refs, yb_parts):
    refs = list(refs)
    x_ref, sc_ref, sh_ref, gate_ref, ya_ref = [refs.pop(0) for _ in range(5)]
    yb_refs = [refs.pop(0) for _ in range(yb_parts)]
    (hf_ref, hb_ref, perm_ref, perm_t_ref, wy_ref, wg_ref, wa_ref, wb_ref, wc_ref, wo_ref,
     lng_ref, lnb_ref, o_ref) = refs
    per_part = pl.num_programs(1) // yb_parts
    yb = yb_refs[-1][...]
    for p in range(yb_parts - 2, -1, -1):
        yb = jnp.where(pl.program_id(1) < (p + 1) * per_part, yb_refs[p][...], yb)
    d = x_ref.shape[1]

    def scan_tile(k):
        rows = slice(k * SCAN_TILE, (k + 1) * SCAN_TILE)
        st = {}

        def project():
            st["x"] = x_ref[rows, :]
            st["h"] = _modulate(st["x"], sc_ref, sh_ref)
            hp = jnp.dot(perm_ref[...], st["h"], preferred_element_type=F32).astype(BF16)
            st["y_rnn"] = jnp.dot(hp, wy_ref[...], preferred_element_type=F32)
            st["zz"] = jnp.dot(st["h"], wg_ref[...], preferred_element_type=F32)

        def branches():
            yc = ((hf_ref[rows, :] + hb_ref[rows, :]) * jax.nn.gelu(st["y_rnn"])).astype(BF16)
            yc = jnp.dot(perm_t_ref[...], yc, preferred_element_type=F32).astype(BF16)
            zz = st["zz"]
            st["mix"] = (
                jax.nn.sigmoid(zz[:, 0:d]) * jnp.dot(ya_ref[rows, :], wa_ref[...], preferred_element_type=F32)
                + jax.nn.sigmoid(zz[:, d:2 * d]) * jnp.dot(yb[rows, :], wb_ref[...], preferred_element_type=F32)
                + jax.nn.sigmoid(zz[:, 2 * d:3 * d]) * jnp.dot(yc, wc_ref[...], preferred_element_type=F32))

        def finish():
            o = jnp.dot(st["mix"].astype(BF16), wo_ref[...], preferred_element_type=F32)
            o_ref[rows, :] = _layernorm(ALPHA * st["x"] + gate_ref[...] * o, lng_ref[...], lnb_ref[...])

        return project, branches, finish

    for stages in zip(*[scan_tile(k) for k in range(x_ref.shape[0] // SCAN_TILE)]):
        for stage in stages:
            stage()


def _merge(x, sc, sh, gate, ya, yb_parts, hf, hb, perm, perm_t, wy, wg, wa, wb, wc, wo, lng, lnb, tm):
    bsz, rows, d = x.shape
    per_part = rows // tm // len(yb_parts)

    def part_spec(p):
        return pl.BlockSpec((None, tm, Q_COLS), lambda b, i: (b, jnp.clip(i - p * per_part, 0, per_part - 1), 0))
    mod_spec = pl.BlockSpec((None, 1, d), lambda b, i: (b, 0, 0))
    vec = _resident((1, d), lambda b, i: (0, 0))

    def tile(cols):
        return pl.BlockSpec((None, tm, cols), lambda b, i: (b, i, 0))

    def weight(w):
        return _resident(w.shape, lambda b, i: (0, 0))

    weights = [perm, perm_t, wy, wg, wa, wb, wc, wo]
    return pl.pallas_call(
        functools.partial(_merge_kernel, yb_parts=len(yb_parts)),
        grid=(bsz, rows // tm),
        in_specs=[tile(d), mod_spec, mod_spec, mod_spec, tile(Q_COLS)]
                 + [part_spec(p) for p in range(len(yb_parts))] + [tile(d), tile(d)]
                 + [weight(w) for w in weights] + [vec, vec],
        out_specs=tile(d),
        out_shape=jax.ShapeDtypeStruct((bsz, rows, d), F32),
        compiler_params=_params("arbitrary", "arbitrary"),
        name="merge_ln",
    )(x, sc, sh, gate, ya, *yb_parts, hf, hb, *weights, lng, lnb)


def _mlp_kernel(x_ref, sc_ref, sh_ref, gate_ref, w1_ref, w2_ref, lng_ref, lnb_ref, o_ref):
    x = x_ref[...]
    h = _modulate(x, sc_ref, sh_ref)
    u = jnp.dot(h, w1_ref[...], preferred_element_type=F32)
    u = jnp.square(jnp.maximum(u, 0.0)).astype(BF16)
    o = jnp.dot(u, w2_ref[...], preferred_element_type=F32)
    o_ref[...] = _layernorm(ALPHA * x + gate_ref[...] * o, lng_ref[...], lnb_ref[...])


def _mlp(x, sc, sh, gate, w1, w2, lng, lnb, tm):
    bsz, rows, d = x.shape
    mod_spec = pl.BlockSpec((None, 1, d), lambda b, i: (b, 0, 0))
    vec = _resident((1, d), lambda b, i: (0, 0))
    tile = pl.BlockSpec((None, tm, d), lambda b, i: (b, i, 0))
    return pl.pallas_call(
        _mlp_kernel,
        grid=(bsz, rows // tm),
        in_specs=[tile, mod_spec, mod_spec, mod_spec,
                  _resident(w1.shape, lambda b, i: (0, 0)), _resident(w2.shape, lambda b, i: (0, 0)), vec, vec],
        out_specs=tile,
        out_shape=jax.ShapeDtypeStruct((bsz, rows, d), F32),
        compiler_params=_params("arbitrary", "arbitrary"),
        name="mlp_ln",
    )(x, sc, sh, gate, w1, w2, lng, lnb)


def _rope_tables(n):
    pos = jnp.arange(n)
    row = (pos // GRID_W).astype(F32)
    colp = (pos % GRID_W).astype(F32)
    nf = HEAD_DIM // 4
    inv = ROPE_BASE ** (-jnp.arange(nf, dtype=F32) / nf)
    ang_r = row[:, None] * inv
    ang_c = colp[:, None] * inv
    cos_t = jnp.concatenate([jnp.cos(ang_r), jnp.cos(ang_r), jnp.cos(ang_c), jnp.cos(ang_c)], axis=-1)
    sin_t = jnp.concatenate([-jnp.sin(ang_r), jnp.sin(ang_r), -jnp.sin(ang_c), jnp.sin(ang_c)], axis=-1)
    return jnp.tile(cos_t, (1, 2)), jnp.tile(sin_t, (1, 2))


def _gate_weights(w_r, w_i):
    def pair(w):
        w = w.reshape(LRU_BLOCKS // 2, 2, LRU_BLOCK_DIM, LRU_BLOCK_DIM)
        z = jnp.zeros_like(w[:, 0])
        top = jnp.concatenate([w[:, 0], z], axis=-1)
        bot = jnp.concatenate([z, w[:, 1]], axis=-1)
        return jnp.concatenate([top, bot], axis=-2)
    return jnp.concatenate([pair(w_r), pair(w_i)], axis=-1).astype(BF16)


def kernel(x, c, ctx, c_ctx, w_ada, b_ada, w_in, a_sink, b_q_gain, b_k_gain, c_conv_w, c_conv_b,
           c_wr, c_br, c_wi, c_bi, c_lam, w_br_a, w_br_b, w_br_c, w_out, ln1_g, ln1_b,
           w_ff1, w_ff2, ln2_g, ln2_b):
    bsz, n, d = x.shape
    m = ctx.shape[1]
    depth = w_ada.shape[0]

    cc = jnp.zeros((SUBLANES, d), F32).at[:bsz].set(c).at[bsz].set(c_ctx)
    mod = _ada(cc, w_ada, b_ada)
    rope_tabs = _rope_tables(n)
    blk = jnp.arange(LANES) // HEAD_DIM
    ones = (blk[:, None] == blk[None, :]).astype(BF16)
    h_zero = jnp.zeros((bsz, 1, d), F32)
    slot = jnp.arange(SCAN_TILE)
    token_of_slot = SEG_LEN * (slot % SUBLANES) + slot // SUBLANES
    perm = (token_of_slot[:, None] == slot[None, :]).astype(BF16)
    perm_t = perm.T

    for l in range(depth):
        with_ctx = l < depth - 1
        mod_lat = mod[l, :bsz].reshape(bsz, 1, 6 * d)
        mod_ctx = jnp.broadcast_to(mod[l, bsz].reshape(1, 1, 6 * d), (bsz, 1, 6 * d))
        sh1, sc1, g1, sh2, sc2, g2 = [mod_lat[:, :, k * d:(k + 1) * d] for k in range(6)]
        csh1, csc1, cg1, csh2, csc2, cg2 = [mod_ctx[:, :, k * d:(k + 1) * d] for k in range(6)]

        w_attn = w_in[l, :, :ATTN_COLS].astype(BF16)
        w_rnn = w_in[l, :, ATTN_COLS:ATTN_COLS + LRU_WIDTH].astype(BF16)
        w_y = w_in[l, :, ATTN_COLS + LRU_WIDTH:ATTN_COLS + 2 * LRU_WIDTH].astype(BF16)
        w_g = w_in[l, :, ATTN_COLS + 2 * LRU_WIDTH:].astype(BF16)
        logit_bound = (HEAD_DIM * HEAD_DIM ** -0.5 * LOG2_E) * jnp.max(jnp.abs(b_q_gain[l])) * jnp.max(
            jnp.abs(b_k_gain[l]))
        gq = jnp.tile(b_q_gain[l], 2).reshape(1, LANES)
        gk = jnp.tile(b_k_gain[l], 2).reshape(1, LANES)

        cb = c_conv_b[l].reshape(1, d)
        conv = (perm, c_conv_w[l], cb)
        qa, ka, va, qb, kb, vb, xs = _inproj(x, sc1, sh1, w_attn, w_rnn, gq, gk, ones, *conv, rope_tabs, ROW_TILE)
        qa_c, ka_c, va_c, qb_c, kb_c, vb_c, xs_c = _inproj(ctx, csc1, csh1, w_attn, w_rnn, gq, gk, ones, *conv,
                                                           None, m)

        ya = _attn_window(qa, ka, va, ka_c, va_c, a_sink[l])
        hs, hs_c, yb_parts = [], [], []
        for di, direction in enumerate(("fwd", "bwd")):
            wg = _gate_weights(c_wr[l, di], c_wi[l, di])
            vecs = [v[l, di].reshape(1, d) for v in (c_br, c_bi, c_lam)]
            h_c, h_fin = _rglru(xs_c, h_zero, wg, *vecs, direction == "bwd")
            scan = (direction, xs, h_fin, wg, *vecs)
            kv_b = [(kb, vb), (kb_c, vb_c)]
            yb_part, h_l = lax.cond(
                logit_bound <= LOGIT_RANGE,
                lambda: _attn_global(qb, kv_b, None, BLOCK, "attn_bounded_" + direction,
                                     part=(di, 2), scan=scan, bounded=True, nsub=BOUNDED_TILES),
                lambda: _attn_global(qb, kv_b, None, BLOCK, "attn_global_" + direction,
                                     lookahead=True, part=(di, 2), scan=scan))
            yb_parts.append(yb_part)
            hs.append(h_l)
            hs_c.append(h_c)

        merge_w = [perm, perm_t, w_y, w_g] + [w[l].astype(BF16) for w in (w_br_a, w_br_b, w_br_c, w_out)]
        ln1 = [ln1_g[l].reshape(1, d), ln1_b[l].reshape(1, d)]
        ln2 = [ln2_g[l].reshape(1, d), ln2_b[l].reshape(1, d)]
        w1 = w_ff1[l].astype(BF16)
        w2 = w_ff2[l].astype(BF16)

        x = _merge(x, sc1, sh1, g1, ya, yb_parts, hs[0], hs[1], *merge_w, *ln1, ROW_TILE)
        x = _mlp(x, sc2, sh2, g2, w1, w2, *ln2, ROW_TILE)
        if with_ctx:
            ya_c = _attn_global(qa_c, [(ka_c, va_c)], a_sink[l], BLOCK, "attn_ctx_a")
            yb_c = _attn_global(qb_c, [(kb_c, vb_c)], None, BLOCK, "attn_ctx_b")
            ctx = _merge(ctx, csc1, csh1, cg1, ya_c, [yb_c], hs_c[0], hs_c[1], *merge_w, *ln1, m)
            ctx = _mlp(ctx, csc2, csh2, cg2, w1, w2, *ln2, m)
    return x
```
